```python
import math
import jax
import jax.numpy as jnp
from jax import lax
import numpy as np

D_MODEL = 1024
BATCH = 1
SEQ = 16384
DEPTH = 2

GRID_W = 64
CTX_LEN = 256
GROUP_WIDTH = D_MODEL // 2
MLA_HEADS = 8
MLA_NOPE = 64
MLA_ROPE = 32
MLA_V = GROUP_WIDTH // MLA_HEADS
MLA_Q_RANK = 384
MLA_KV_RANK = 256
S5_CH = 16
S5_GROUPS = GROUP_WIDTH // S5_CH
S5_STATE = 64
DIFF_HEADS = 4
DIFF_QK = 64
DIFF_V = GROUP_WIDTH // DIFF_HEADS
SSD_INNER = GROUP_WIDTH
SSD_HEAD_DIM = 64
SSD_HEADS = SSD_INNER // SSD_HEAD_DIM
SSD_GROUPS = 2
SSD_STATE = 128
SSD_CONV = 5
SSD_CHUNK = 128
D_FF = -(-8 * D_MODEL // (3 * 256)) * 256
Q_BLOCK = 128
ROPE_THETA = 10000.0
EPS = 1e-6
EVEN_SPLIT = (MLA_Q_RANK, MLA_KV_RANK, MLA_ROPE, GROUP_WIDTH)
DIFF_QK_W = DIFF_HEADS * 2 * DIFF_QK
SSD_XBC = SSD_INNER + 2 * SSD_GROUPS * SSD_STATE
ODD_SPLIT = (DIFF_QK_W, DIFF_QK_W, DIFF_HEADS * DIFF_V, SSD_INNER, SSD_XBC, 2 * SSD_HEADS)

kernel_name = 'hybrid_mla_s5_diffattn_ssd_prefix_trunk'


def split_cols(t, sizes):
    return jnp.split(t, np.cumsum(sizes)[:-1].tolist(), axis=-1)


def rms_norm(x, gain=None):
    xf = x.astype(jnp.float32)
    y = (xf * lax.rsqrt(jnp.mean(xf * xf, axis=-1, keepdims=True) + EPS)).astype(x.dtype)
    return y if gain is None else y * gain


def modulate(x, shift, scale):
    return rms_norm(x) * (1.0 + scale) + shift


def swiglu(h, w1, w3, w2):
    return (jax.nn.silu(h @ w1) * (h @ w3)) @ w2


def rope_1d(x, pos):
    n = x.shape[-1] // 2
    inv = ROPE_THETA ** (-jnp.arange(n, dtype=jnp.float32) / n)
    ang = pos.astype(jnp.float32)[:, None] * inv[None, :]
    ang = ang.reshape((1, ang.shape[0]) + (1,) * (x.ndim - 3) + (n,))
    cos = jnp.cos(ang).astype(x.dtype)
    sin = jnp.sin(ang).astype(x.dtype)
    x1, x2 = x[..., :n], x[..., n:]
    return jnp.concatenate([x1 * cos - x2 * sin, x1 * sin + x2 * cos], axis=-1)


def rope_2d(x, pos):
    row, col = pos
    h = x.shape[-1] // 2
    return jnp.concatenate([rope_1d(x[..., :h], row), rope_1d(x[..., h:], col)], axis=-1)


def dense_attention(q, k, v, map_w):
    bsz, lq, nh, nm, dk = q.shape
    dv = v.shape[-1]
    scale = dk ** -0.5

    def one_block(qb):
        s = jnp.einsum('bqhmd,bkhmd->bhmqk', qb, k).astype(jnp.float32) * scale
        p = jnp.einsum('bhmqk,m->bhqk', jax.nn.softmax(s, axis=-1), map_w.astype(jnp.float32))
        return jnp.einsum('bhqk,bkhd->bqhd', p.astype(v.dtype), v)

    nb = lq // Q_BLOCK
    qb = q.reshape(bsz, nb, Q_BLOCK, nh, nm, dk).transpose(1, 0, 2, 3, 4, 5)
    out = lax.map(one_block, qb)
    return out.transpose(1, 0, 2, 3, 4).reshape(bsz, lq, nh, dv)


def mla_queries(cq, g_qa, w_qb, g_qn, pos):
    bsz, seq, _ = cq.shape
    q = (rms_norm(cq, g_qa) @ w_qb).reshape(bsz, seq, MLA_HEADS, MLA_NOPE + MLA_ROPE)
    q = rms_norm(q, g_qn)
    if pos is not None:
        q = jnp.concatenate([q[..., :MLA_NOPE], rope_2d(q[..., MLA_NOPE:], pos)], axis=-1)
    return q[:, :, :, None, :]


def mla_keys_values(ckv, kr, g_kva, w_kvb, g_kn, pos):
    bsz, seq, _ = ckv.shape
    kv = (rms_norm(ckv, g_kva) @ w_kvb).reshape(bsz, seq, MLA_HEADS, MLA_NOPE + MLA_V)
    k_nope, v = kv[..., :MLA_NOPE], kv[..., MLA_NOPE:]
    k_rope = jnp.broadcast_to(kr[:, :, None, :], (bsz, seq, MLA_HEADS, MLA_ROPE))
    k = rms_norm(jnp.concatenate([k_nope, k_rope], axis=-1), g_kn)
    if pos is not None:
        k = jnp.concatenate([k[..., :MLA_NOPE], rope_2d(k[..., MLA_NOPE:], pos)], axis=-1)
    return k[:, :, :, None, :], v


def _complex_affine_combine(e1, e2):
    a1r, a1i, b1r, b1i = e1
    a2r, a2i, b2r, b2i = e2
    return (a2r * a1r - a2i * a1i,
            a2r * a1i + a2i * a1r,
            a2r * b1r - a2i * b1i + b2r,
            a2r * b1i + a2i * b1r + b2i)


def s5_scan(u, lam_re, lam_im, log_dt, b_re, b_im, c_re, c_im, h0, reverse, need_y):
    dt = jnp.exp(log_dt)[:, None]
    mag = jnp.exp(lam_re * dt)
    ang = lam_im * dt
    a_re = mag * jnp.cos(ang)
    a_im = mag * jnp.sin(ang)
    den = lam_re * lam_re + lam_im * lam_im
    k_re = ((a_re - 1.0) * lam_re + a_im * lam_im) / den
    k_im = (a_im * lam_re - (a_re - 1.0) * lam_im) / den
    bb_re = k_re[..., None] * b_re - k_im[..., None] * b_im
    bb_im = k_re[..., None] * b_im + k_im[..., None] * b_re
    bu_re = jnp.einsum('blgc,gnc->blgn', u, bb_re)
    bu_im = jnp.einsum('blgc,gnc->blgn', u, bb_im)
    a_re = jnp.broadcast_to(a_re, bu_re.shape)
    a_im = jnp.broadcast_to(a_im, bu_re.shape)
    p_re, p_im, h_re, h_im = lax.associative_scan(
        _complex_affine_combine, (a_re, a_im, bu_re, bu_im), reverse=reverse, axis=1)
    if h0 is not None:
        h0_re, h0_im = h0[0][:, None], h0[1][:, None]
        h_re, h_im = h_re + p_re * h0_re - p_im * h0_im, h_im + p_re * h0_im + p_im * h0_re
    end = 0 if reverse else -1
    final = (h_re[:, end], h_im[:, end])
    if not need_y:
        return None, final
    y = jnp.einsum('blgn,gcn->blgc', h_re, c_re) - jnp.einsum('blgn,gcn->blgc', h_im, c_im)
    return y, final


def s5_glu(y, w_glu, b_glu):
    g = jax.nn.gelu(y)
    return g * jax.nn.sigmoid(g @ w_glu + b_glu)


def s5_mixer(u, u_c, need_ctx, lam_re, lam_im, log_dt, b_re, b_im, c_re, c_im, d_skip, w_glu, b_glu):
    bsz, seq, _ = u.shape
    seq_c = u_c.shape[1]
    ug = u.reshape(bsz, seq, S5_GROUPS, S5_CH)
    ug_c = u_c.reshape(bsz, seq_c, S5_GROUPS, S5_CH)
    y = ug * d_skip
    y_c = ug_c * d_skip
    for di in range(2):
        rev = di == 1
        yd_c, h_c = s5_scan(ug_c, lam_re[di], lam_im[di], log_dt[di], b_re[di], b_im[di],
                            c_re[di], c_im[di], None, rev, need_ctx)
        yd, _ = s5_scan(ug, lam_re[di], lam_im[di], log_dt[di], b_re[di], b_im[di],
                        c_re[di], c_im[di], h_c, rev, True)
        y = y + yd
        if need_ctx:
            y_c = y_c + yd_c
    out = s5_glu(y.reshape(bsz, seq, GROUP_WIDTH), w_glu, b_glu)
    out_c = s5_glu(y_c.reshape(bsz, seq_c, GROUP_WIDTH), w_glu, b_glu) if need_ctx else None
    return out, out_c


def even_mixer(h, h_c, pos, need_ctx, w_in, w_out, g_qa, w_qb, g_kva, w_kvb, g_qn, g_kn,
               lam_re, lam_im, log_dt, b_re, b_im, c_re, c_im, d_skip, w_glu, b_glu):
    bsz, seq, _ = h.shape
    seq_c = h_c.shape[1]
    cq, ckv, kr, u = split_cols(h @ w_in, EVEN_SPLIT)
    cq_c, ckv_c, kr_c, u_c = split_cols(h_c @ w_in, EVEN_SPLIT)
    one_map = jnp.ones((1,), jnp.float32)
    q = mla_queries(cq, g_qa, w_qb, g_qn, pos)
    k, v = mla_keys_values(ckv, kr, g_kva, w_kvb, g_kn, pos)
    k_c, v_c = mla_keys_values(ckv_c, kr_c, g_kva, w_kvb, g_kn, None)
    att = dense_attention(q, jnp.concatenate([k_c, k], axis=1), jnp.concatenate([v_c, v], axis=1), one_map)
    ssm, ssm_c = s5_mixer(u, u_c, need_ctx, lam_re, lam_im, log_dt, b_re, b_im, c_re, c_im,
                          d_skip, w_glu, b_glu)
    y = jnp.concatenate([att.reshape(bsz, seq, GROUP_WIDTH), ssm], axis=-1) @ w_out
    if not need_ctx:
        return y, None
    att_c = dense_attention(mla_queries(cq_c, g_qa, w_qb, g_qn, None), k_c, v_c, one_map)
    y_c = jnp.concatenate([att_c.reshape(bsz, seq_c, GROUP_WIDTH), ssm_c], axis=-1) @ w_out
    return y, y_c


def segsum(x):
    t = x.shape[-1]
    idx = jnp.arange(t)
    xx = jnp.where(idx[:, None] > idx[None, :], x[..., :, None], 0.0)
    ss = jnp.cumsum(xx, axis=-2)
    return jnp.where(idx[:, None] >= idx[None, :], ss, -jnp.inf)


def ssd_chunked(x, dt, a, bm, cm, h0, need_y):
    bsz, seq, nh, hp = x.shape
    ns = bm.shape[-1]
    nc = seq // SSD_CHUNK
    x = x.reshape(bsz, nc, SSD_CHUNK, nh, hp)
    dt = dt.reshape(bsz, nc, SSD_CHUNK, nh)
    bm = bm.reshape(bsz, nc, SSD_CHUNK, nh, ns)
    cm = cm.reshape(bsz, nc, SSD_CHUNK, nh, ns)
    da = jnp.moveaxis(dt * a, -1, 1)
    a_cs = jnp.cumsum(da, axis=-1)
    xdt = x * dt[..., None]
    decay_states = jnp.exp(a_cs[..., -1:] - a_cs)
    states = jnp.einsum('bcshn,bhcs,bcshp->bchpn', bm, decay_states, xdt)
    states = jnp.concatenate([h0[:, None].astype(states.dtype), states], axis=1)
    chunk_decay = jnp.exp(segsum(jnp.pad(a_cs[..., -1], ((0, 0), (0, 0), (1, 0)))))
    states = jnp.einsum('bhzc,bchpn->bzhpn', chunk_decay, states)
    final = states[:, -1]
    if not need_y:
        return None, final
    y_diag = jnp.einsum('bclhn,bcshn,bhcls,bcshp->bclhp', cm, bm, jnp.exp(segsum(da)), xdt)
    y_off = jnp.einsum('bclhn,bchpn,bhcl->bclhp', cm, states[:, :-1], jnp.exp(a_cs))
    return (y_diag + y_off).reshape(bsz, seq, nh, hp), final


def short_conv(t, w, b):
    pad = w.shape[0] // 2
    out = lax.conv_general_dilated(t, w[:, None, :].astype(t.dtype), window_strides=(1,),
                                   padding=[(pad, pad)], dimension_numbers=('NWC', 'WIO', 'NWC'),
                                   feature_group_count=t.shape[-1])
    return out + b


def ssd_streams(xbc, conv_w, conv_b):
    bsz, seq, _ = xbc.shape
    xbc = jax.nn.silu(short_conv(xbc, conv_w, conv_b))
    x, bm, cm = split_cols(xbc, (SSD_INNER, SSD_GROUPS * SSD_STATE, SSD_GROUPS * SSD_STATE))
    rep = SSD_HEADS // SSD_GROUPS
    bm = jnp.repeat(bm.reshape(bsz, seq, SSD_GROUPS, SSD_STATE), rep, axis=2)
    cm = jnp.repeat(cm.reshape(bsz, seq, SSD_GROUPS, SSD_STATE), rep, axis=2)
    return x.reshape(bsz, seq, SSD_HEADS, SSD_HEAD_DIM), bm, cm


def _flip(t, rev):
    return t[:, ::-1] if rev else t


def ssd_gated_norm(y, z, g):
    bsz, seq = y.shape[:2]
    gy = (y.reshape(bsz, seq, SSD_INNER) * jax.nn.silu(z)).reshape(bsz, seq, SSD_GROUPS, SSD_INNER // SSD_GROUPS)
    return rms_norm(gy).reshape(bsz, seq, SSD_INNER) * g


def ssd_mixer(z, xbc, dtr, z_c, xbc_c, dtr_c, need_ctx, conv_w, conv_b, dt_bias, a_log, d_skip, g_norm):
    x, bm, cm = ssd_streams(xbc, conv_w, conv_b)
    x_c, bm_c, cm_c = ssd_streams(xbc_c, conv_w, conv_b)
    y = x * d_skip[:, None]
    y_c = x_c * d_skip[:, None]
    h0 = jnp.zeros((x_c.shape[0], SSD_HEADS, SSD_HEAD_DIM, SSD_STATE), x.dtype)
    for di in range(2):
        rev = di == 1
        a = -jnp.exp(a_log[di])
        hs = slice(di * SSD_HEADS, (di + 1) * SSD_HEADS)
        dt = jax.nn.softplus(dtr[..., hs] + dt_bias[di])
        dt_c = jax.nn.softplus(dtr_c[..., hs] + dt_bias[di])
        yd_c, h_c = ssd_chunked(_flip(x_c, rev), _flip(dt_c, rev), a, _flip(bm_c, rev), _flip(cm_c, rev),
                                h0, need_ctx)
        yd, _ = ssd_chunked(_flip(x, rev), _flip(dt, rev), a, _flip(bm, rev), _flip(cm, rev), h_c, True)
        y = y + _flip(yd, rev)
        if need_ctx:
            y_c = y_c + _flip(yd_c, rev)
    out = ssd_gated_norm(y, z, g_norm)
    out_c = ssd_gated_norm(y_c, z_c, g_norm) if need_ctx else None
    return out, out_c


def diff_qk(t, g, pos):
    t = rms_norm(t.reshape(t.shape[0], t.shape[1], DIFF_HEADS, 2, DIFF_QK), g)
    return t if pos is None else rope_2d(t, pos)


def diff_values(t):
    return t.reshape(t.shape[0], t.shape[1], DIFF_HEADS, DIFF_V)


def diff_heads_out(o, g_o, lam_init):
    return (rms_norm(o, g_o) * (1.0 - lam_init)).reshape(o.shape[0], o.shape[1], GROUP_WIDTH)


def odd_mixer(h, h_c, pos, need_ctx, layer_idx, w_in, w_out, g_q, g_k, lq1, lk1, lq2, lk2, g_o,
              conv_w, conv_b, dt_bias, a_log, d_skip, g_ssd):
    q, k, v, z, xbc, dtr = split_cols(h @ w_in, ODD_SPLIT)
    q_c, k_c, v_c, z_c, xbc_c, dtr_c = split_cols(h_c @ w_in, ODD_SPLIT)
    lam_init = 0.8 - 0.6 * math.exp(-0.3 * layer_idx)
    lam = (jnp.exp(jnp.sum(lq1 * lk1).astype(jnp.float32))
           - jnp.exp(jnp.sum(lq2 * lk2).astype(jnp.float32)) + lam_init)
    map_w = jnp.stack([jnp.ones_like(lam), -lam])
    kk_c = diff_qk(k_c, g_k, None)
    vv_c = diff_values(v_c)
    att = dense_attention(diff_qk(q, g_q, pos),
                          jnp.concatenate([kk_c, diff_qk(k, g_k, pos)], axis=1),
                          jnp.concatenate([vv_c, diff_values(v)], axis=1), map_w)
    att = diff_heads_out(att, g_o, lam_init)
    ssm, ssm_c = ssd_mixer(z, xbc, dtr, z_c, xbc_c, dtr_c, need_ctx, conv_w, conv_b, dt_bias, a_log,
                           d_skip, g_ssd)
    y = jnp.concatenate([att, ssm], axis=-1) @ w_out
    if not need_ctx:
        return y, None
    att_c = diff_heads_out(dense_attention(diff_qk(q_c, g_q, None), kk_c, vv_c, map_w), g_o, lam_init)
    return y, jnp.concatenate([att_c, ssm_c], axis=-1) @ w_out


def setup_inputs(seed: int = 0) -> dict:
    key = jax.random.key(seed)
    keys = iter(jax.random.split(key, 64))
    f32 = jnp.float32
    d = D_MODEL
    ne = (DEPTH + 1) // 2
    no = DEPTH // 2
    mix = 2 * GROUP_WIDTH

    def normal(shape, scale):
        return jax.random.normal(next(keys), shape, f32) * scale

    def gain(shape):
        return 1.0 + normal(shape, 0.02)

    def log_uniform(shape, lo, hi):
        return jax.random.uniform(next(keys), shape, f32, math.log(lo), math.log(hi))

    dt_ssd = jnp.exp(log_uniform((no, 2, SSD_HEADS), 1e-3, 1e-1))
    s5_shape = (ne, 2, S5_GROUPS, S5_STATE)
    return {
        'x': normal((BATCH, SEQ, d), 1.0),
        'c': normal((BATCH, d), 1.0),
        'ctx': normal((BATCH, CTX_LEN, d), 1.0),
        'c_ctx': normal((d,), 1.0),
        'ada_w': normal((DEPTH, d, 6 * d), 0.5 * d ** -0.5),
        'ada_b': normal((DEPTH, 6 * d), 0.02),
        'ffn_w1': normal((DEPTH, d, D_FF), d ** -0.5),
        'ffn_w3': normal((DEPTH, d, D_FF), d ** -0.5),
        'ffn_w2': normal((DEPTH, D_FF, d), D_FF ** -0.5),
        'e_w_in': normal((ne, d, sum(EVEN_SPLIT)), d ** -0.5),
        'e_w_out': normal((ne, mix, d), mix ** -0.5),
        'mla_g_qa': gain((ne, MLA_Q_RANK)),
        'mla_w_qb': normal((ne, MLA_Q_RANK, MLA_HEADS * (MLA_NOPE + MLA_ROPE)), MLA_Q_RANK ** -0.5),
        'mla_g_kva': gain((ne, MLA_KV_RANK)),
        'mla_w_kvb': normal((ne, MLA_KV_RANK, MLA_HEADS * (MLA_NOPE + MLA_V)), MLA_KV_RANK ** -0.5),
        'mla_g_qn': gain((ne, MLA_NOPE + MLA_ROPE)),
        'mla_g_kn': gain((ne, MLA_NOPE + MLA_ROPE)),
        's5_lam_re': -0.5 + normal(s5_shape, 0.01),
        's5_lam_im': math.pi * jnp.arange(S5_STATE, dtype=f32) + normal(s5_shape, 0.01),
        's5_log_dt': log_uniform((ne, 2, S5_GROUPS), 1e-3, 1e-1),
        's5_b_re': normal((ne, 2, S5_GROUPS, S5_STATE, S5_CH), (2 * S5_CH) ** -0.5),
        's5_b_im': normal((ne, 2, S5_GROUPS, S5_STATE, S5_CH), (2 * S5_CH) ** -0.5),
        's5_c_re': normal((ne, 2, S5_GROUPS, S5_CH, S5_STATE), S5_STATE ** -0.5),
        's5_c_im': normal((ne, 2, S5_GROUPS, S5_CH, S5_STATE), S5_STATE ** -0.5),
        's5_d': normal((ne, S5_GROUPS, S5_CH), 1.0),
        's5_w_glu': normal((ne, GROUP_WIDTH, GROUP_WIDTH), GROUP_WIDTH ** -0.5),
        's5_b_glu': normal((ne, GROUP_WIDTH), 0.02),
        'o_w_in': normal((no, d, sum(ODD_SPLIT)), d ** -0.5),
        'o_w_out': normal((no, mix, d), mix ** -0.5),
        'diff_g_q': gain((no, DIFF_QK)),
        'diff_g_k': gain((no, DIFF_QK)),
        'diff_lq1': normal((no, DIFF_QK), 0.1),
        'diff_lk1': normal((no, DIFF_QK), 0.1),
        'diff_lq2': normal((no, DIFF_QK), 0.1),
        'diff_lk2': normal((no, DIFF_QK), 0.1),
        'diff_g_o': gain((no, DIFF_V)),
        'ssd_conv_w': normal((no, SSD_CONV, SSD_XBC), SSD_CONV ** -0.5),
        'ssd_conv_b': normal((no, SSD_XBC), 0.02),
        'ssd_dt_bias': dt_ssd + jnp.log(-jnp.expm1(-dt_ssd)),
        'ssd_a_log': jnp.log(jax.random.uniform(next(keys), (no, 2, SSD_HEADS), f32, 1.0, 16.0)),
        'ssd_d': gain((no, SSD_HEADS)),
        'ssd_g': gain((no, SSD_INNER)),
    }


def reference(x, c, ctx, c_ctx, ada_w, ada_b, ffn_w1, ffn_w3, ffn_w2, e_w_in, e_w_out,
              mla_g_qa, mla_w_qb, mla_g_kva, mla_w_kvb, mla_g_qn, mla_g_kn,
              s5_lam_re, s5_lam_im, s5_log_dt, s5_b_re, s5_b_im, s5_c_re, s5_c_im, s5_d,
              s5_w_glu, s5_b_glu, o_w_in, o_w_out, diff_g_q, diff_g_k, diff_lq1, diff_lk1,
              diff_lq2, diff_lk2, diff_g_o, ssd_conv_w, ssd_conv_b, ssd_dt_bias, ssd_a_log,
              ssd_d, ssd_g):
    seq = x.shape[1]
    rows = seq // GRID_W
    pos = (jnp.repeat(jnp.arange(rows), GRID_W), jnp.tile(jnp.arange(GRID_W), rows))
    silu_c = jax.nn.silu(c)
    silu_cc = jax.nn.silu(c_ctx)
    for i in range(DEPTH):
        need_ctx = i < DEPTH - 1
        j = i // 2
        mod = (silu_c @ ada_w[i] + ada_b[i])[:, None, :]
        mod_c = silu_cc @ ada_w[i] + ada_b[i]
        sh_a, sc_a, g_a, sh_f, sc_f, g_f = jnp.split(mod, 6, axis=-1)
        shc_a, scc_a, gc_a, shc_f, scc_f, gc_f = jnp.split(mod_c, 6, axis=-1)
        h = modulate(x, sh_a, sc_a)
        h_c = modulate(ctx, shc_a, scc_a)
        if i % 2 == 0:
            y, y_c = even_mixer(h, h_c, pos, need_ctx, e_w_in[j], e_w_out[j], mla_g_qa[j], mla_w_qb[j],
                                mla_g_kva[j], mla_w_kvb[j], mla_g_qn[j], mla_g_kn[j], s5_lam_re[j],
                                s5_lam_im[j], s5_log_dt[j], s5_b_re[j], s5_b_im[j], s5_c_re[j], s5_c_im[j],
                                s5_d[j], s5_w_glu[j], s5_b_glu[j])
        else:
            y, y_c = odd_mixer(h, h_c, pos, need_ctx, i, o_w_in[j], o_w_out[j], diff_g_q[j], diff_g_k[j],
                               diff_lq1[j], diff_lk1[j], diff_lq2[j], diff_lk2[j], diff_g_o[j],
                               ssd_conv_w[j], ssd_conv_b[j], ssd_dt_bias[j], ssd_a_log[j], ssd_d[j], ssd_g[j])
        x = x + g_a * y
        x = x + g_f * swiglu(modulate(x, sh_f, sc_f), ffn_w1[i], ffn_w3[i], ffn_w2[i])
        if need_ctx:
            ctx = ctx + gc_a * y_c
            ctx = ctx + gc_f * swiglu(modulate(ctx, shc_f, scc_f), ffn_w1[i], ffn_w3[i], ffn_w2[i])
    return x
```

```python
import functools
import math

import numpy as np
import jax
import jax.numpy as jnp
from jax import lax
from jax.experimental import pallas as pl
from jax.experimental.pallas import tpu as pltpu

F32 = jnp.float32
BF16 = jnp.bfloat16
HIGHEST = lax.Precision.HIGHEST

D_MODEL = 1024
CTX_LEN = 256
GRID_W = 64
GROUP_WIDTH = 512
D_FF = 2816
EPS = 1e-6
ROPE_THETA = 10000.0
LOG2E = math.log2(math.e)

MLA_HEADS, MLA_NOPE, MLA_ROPE, MLA_V = 8, 64, 32, 64
MLA_QK = MLA_NOPE + MLA_ROPE
MLA_Q_RANK, MLA_KV_RANK = 384, 256
S5_CH, S5_GROUPS, S5_STATE = 16, 32, 64
S5_T = 16
S5_GPB = 4
DIFF_HEADS, DIFF_QK, DIFF_V = 4, 64, 128
SSD_HEADS, SSD_HEAD_DIM, SSD_GROUPS, SSD_STATE, SSD_CONV, SSD_CHUNK = 8, 64, 2, 128, 5, 128
SSD_INNER = 512
SSD_XBC = SSD_INNER + 2 * SSD_GROUPS * SSD_STATE

LANES = 128
ROW_TILE = 256
FFN_TILE = 640
FFN_CHUNK = 256
VMEM_LIMIT = 56 * 1024 * 1024


def _cparams(sem):
    return pltpu.CompilerParams(dimension_semantics=sem, vmem_limit_bytes=VMEM_LIMIT)


def _dot(a, b):
    return jnp.dot(a, b, preferred_element_type=F32)


def _rms(x):
    return x * lax.rsqrt(jnp.mean(x * x, axis=-1, keepdims=True) + EPS)


def _sigmoid(x):
    return 1.0 / (1.0 + jnp.exp(-x))


def _silu(x):
    return x * _sigmoid(x)


def _gelu_tanh(x):
    return 0.5 * x * (1.0 + jnp.tanh(math.sqrt(2.0 / math.pi) * (x + 0.044715 * (x * x * x))))


def _softplus(x):
    return jnp.maximum(x, 0.0) + jnp.log(1.0 + jnp.exp(-jnp.abs(x)))


def _const_spec(shape):
    nd = len(shape)
    return pl.BlockSpec(shape, lambda *_: (0,) * nd)


def _mod_kernel(c_ref, w_ref, b_ref, o_ref):
    s = _silu(c_ref[...]).astype(BF16)
    o_ref[0] = _dot(s, w_ref[0].astype(BF16)) + b_ref[0]


def _mod_vectors(c, c_ctx, ada_w, ada_b):
    depth = ada_w.shape[0]
    cc = jnp.zeros((8, D_MODEL), F32).at[0].set(c[0]).at[1].set(c_ctx)
    nblk = 6
    out = pl.pallas_call(
        _mod_kernel,
        grid=(depth, nblk),
        in_specs=[
            pl.BlockSpec((8, D_MODEL), lambda i, j: (0, 0)),
            pl.BlockSpec((1, D_MODEL, D_MODEL), lambda i, j: (i, 0, j)),
            pl.BlockSpec((1, 1, D_MODEL), lambda i, j: (i, 0, j)),
        ],
        out_specs=pl.BlockSpec((1, 8, D_MODEL), lambda i, j: (i, 0, j)),
        out_shape=jax.ShapeDtypeStruct((depth, 8, 6 * D_MODEL), F32),
        compiler_params=_cparams(("arbitrary", "arbitrary")),
    )(cc, ada_w, ada_b.reshape(depth, 1, 6 * D_MODEL))
    mods = out[:, :2].reshape(depth, 2, 6, D_MODEL)
    return jnp.pad(mods, ((0, 0), (0, 0), (0, 2), (0, 0)))


def _mod_rows(modx_ref, modc_ref, k, row0, nrows):
    rows = row0 + lax.broadcasted_iota(jnp.int32, (nrows, 1), 0)
    return jnp.where(rows < CTX_LEN, modc_ref[k:k + 1, :], modx_ref[k:k + 1, :])


def _rope_partner(n_half):
    return np.array([i + n_half if i < n_half else i - n_half for i in range(2 * n_half)])


def _even_in_kernel(x_ref, modx_ref, modc_ref, win_ref, gqa_ref, wq_ref, wqs_ref, gkva_ref, wk_ref,
                    wv_ref, gq_ref, gqs_ref, gk_ref, gks_ref, cos_ref, sin_ref,
                    q_ref, k_ref, v_ref, u_ref):
    i = pl.program_id(0)
    is_ctx = i == 0
    sh = jnp.where(is_ctx, modc_ref[0:1, :], modx_ref[0:1, :])
    sc = jnp.where(is_ctx, modc_ref[1:2, :], modx_ref[1:2, :])
    h = (_rms(x_ref[...]) * (1.0 + sc) + sh).astype(BF16)
    proj = _dot(h, win_ref[...])
    cq = proj[:, 0:384]
    ckv = proj[:, 384:640]
    u_ref[...] = proj[:, 640:1152]
    krb = proj[:, 1152:1280]
    krs = proj[:, 1280:1408]
    cqn = (_rms(cq) * gqa_ref[...]).astype(BF16)
    qf = _dot(cqn, wq_ref[...])
    qs = _dot(cqn, wqs_ref[...])
    ckvn = (_rms(ckv) * gkva_ref[...]).astype(BF16)
    kf = _dot(ckvn, wk_ref[...])
    v_ref[...] = _dot(ckvn, wv_ref[...]).astype(BF16)
    cos = cos_ref[...]
    sin = sin_ref[...]
    qscale = MLA_QK ** -0.5 * LOG2E
    inv_n = 1.0 / MLA_QK
    for hd in range(MLA_HEADS):
        blk = slice(hd * LANES, (hd + 1) * LANES)
        qh = qf[:, blk]
        rq = lax.rsqrt(jnp.sum(qh * qh, axis=-1, keepdims=True) * inv_n + EPS) * qscale
        qo = (qh * (gq_ref[...] * cos) + qs[:, blk] * (gqs_ref[...] * sin)) * rq
        q_ref[:, blk] = qo.astype(BF16)
        kh = kf[:, blk] + krb
        rk = lax.rsqrt(jnp.sum(kh * kh, axis=-1, keepdims=True) * inv_n + EPS)
        ko = (kh * (gk_ref[...] * cos) + krs * (gks_ref[...] * sin)) * rk
        k_ref[:, blk] = ko.astype(BF16)


def _even_in(xs, modx, modc, w_in, g_qa, w_qb, g_kva, w_kvb, g_qn, g_kn, cos0, sin0):
    lt = xs.shape[0]
    perm = _rope_partner(MLA_ROPE // 4)
    perm = np.concatenate([perm, perm + MLA_ROPE // 2])
    cq_w, ckv_w = w_in[:, :384], w_in[:, 384:640]
    kr_w, u_w = w_in[:, 640:672], w_in[:, 672:]
    zblk = jnp.zeros((D_MODEL, LANES), F32)
    krblk = zblk.at[:, MLA_NOPE:MLA_QK].set(kr_w)
    krsblk = zblk.at[:, MLA_NOPE:MLA_QK].set(kr_w[:, perm])
    win = jnp.concatenate([cq_w, ckv_w, u_w, krblk, krsblk], axis=1).astype(BF16)
    wq3 = w_qb.reshape(MLA_Q_RANK, MLA_HEADS, MLA_QK)
    wq = jnp.zeros((MLA_Q_RANK, MLA_HEADS, LANES), F32).at[:, :, :MLA_QK].set(wq3)
    wqs = jnp.zeros((MLA_Q_RANK, MLA_HEADS, LANES), F32).at[:, :, MLA_NOPE:MLA_QK].set(
        wq3[:, :, MLA_NOPE:][:, :, perm])
    wkv3 = w_kvb.reshape(MLA_KV_RANK, MLA_HEADS, MLA_NOPE + MLA_V)
    wk = jnp.zeros((MLA_KV_RANK, MLA_HEADS, LANES), F32).at[:, :, :MLA_NOPE].set(wkv3[:, :, :MLA_NOPE])
    wv = wkv3[:, :, MLA_NOPE:].reshape(MLA_KV_RANK, MLA_HEADS * MLA_V)

    def pad_gain(g):
        gp = jnp.zeros((1, LANES), F32).at[0, :MLA_QK].set(g)
        gs = jnp.zeros((1, LANES), F32).at[0, MLA_NOPE:MLA_QK].set(g[MLA_NOPE:][perm])
        return gp, gs

    gq, gqs = pad_gain(g_qn)
    gk, gks = pad_gain(g_kn)
    hw = MLA_HEADS * LANES
    args = (xs, modx, modc, win, g_qa.reshape(1, -1), wq.reshape(MLA_Q_RANK, hw).astype(BF16),
            wqs.reshape(MLA_Q_RANK, hw).astype(BF16), g_kva.reshape(1, -1),
            wk.reshape(MLA_KV_RANK, hw).astype(BF16), wv.astype(BF16), gq, gqs, gk, gks, cos0, sin0)
    row = lambda w: pl.BlockSpec((ROW_TILE, w), lambda i: (i, 0))
    in_specs = [row(D_MODEL)] + [_const_spec(a.shape) for a in args[1:14]] + [row(LANES), row(LANES)]
    return pl.pallas_call(
        _even_in_kernel,
        grid=(lt // ROW_TILE,),
        in_specs=in_specs,
        out_specs=[row(hw), row(hw), row(MLA_HEADS * MLA_V), row(GROUP_WIDTH)],
        out_shape=[jax.ShapeDtypeStruct((lt, hw), BF16), jax.ShapeDtypeStruct((lt, hw), BF16),
                   jax.ShapeDtypeStruct((lt, MLA_HEADS * MLA_V), BF16),
                   jax.ShapeDtypeStruct((lt, GROUP_WIDTH), F32)],
        compiler_params=_cparams(("arbitrary",)),
    )(*args)


def _kv_chunk(lt):
    for tk in (1280, 640, 256):
        if lt % tk == 0:
            return tk
    raise ValueError(lt)


def _online_update(q, k_ref, v_ref, start, size, carry):
    m, l, acc = carry
    kc = k_ref[pl.ds(start, size), :]
    vc = v_ref[pl.ds(start, size), :]
    s = lax.dot_general(q, kc, (((1,), (1,)), ((), ())), preferred_element_type=F32)
    mn = jnp.maximum(m, jnp.max(s, axis=-1, keepdims=True))
    alpha = jnp.exp2(m - mn)
    p = jnp.exp2(s - mn)
    l = alpha * l + jnp.sum(p, axis=-1, keepdims=True)
    acc = alpha * acc + _dot(p.astype(BF16), vc)
    return mn, l, acc


def _attend(streams, is_ctx, lt, tk):
    tq = streams[0][0].shape[0]
    init = tuple((jnp.full((tq, 1), -jnp.inf, F32), jnp.zeros((tq, 1), F32),
                  jnp.zeros((tq, v_ref.shape[-1]), F32)) for _, _, v_ref in streams)

    def body(j, carries):
        start = pl.multiple_of(j * tk, tk)
        return tuple(_online_update(q, k_ref, v_ref, start, tk, c)
                     for (q, k_ref, v_ref), c in zip(streams, carries))

    def ctx_branch():
        return tuple(_online_update(q, k_ref, v_ref, 0, CTX_LEN, c)
                     for (q, k_ref, v_ref), c in zip(streams, init))

    def full_branch():
        return lax.fori_loop(0, lt // tk, body, init)

    out = lax.cond(is_ctx, ctx_branch, full_branch)
    return [(l, acc) for (_, l, acc) in out]


def _mla_attn_kernel(q_ref, k_ref, v_ref, o_ref, *, lt, tk):
    is_ctx = pl.program_id(1) == 0
    lane = lax.broadcasted_iota(jnp.int32, (1, LANES), 1)
    blks = [slice(hh * LANES, (hh + 1) * LANES) for hh in range(2)]
    (la, acc_a), (lb, acc_b) = _attend([(q_ref[:, b], k_ref.at[:, b], v_ref) for b in blks], is_ctx, lt, tk)
    o_ref[...] = jnp.where(lane < MLA_V, acc_a / la, acc_b / lb)


def _mla_attention(q, k, v):
    lt = q.shape[0]
    tk = _kv_chunk(lt)
    npair = MLA_HEADS // 2
    return pl.pallas_call(
        functools.partial(_mla_attn_kernel, lt=lt, tk=tk),
        grid=(npair, lt // ROW_TILE),
        in_specs=[
            pl.BlockSpec((ROW_TILE, 2 * LANES), lambda p, i: (i, p)),
            pl.BlockSpec((lt, 2 * LANES), lambda p, i: (0, p), pipeline_mode=pl.Buffered(1)),
            pl.BlockSpec((lt, LANES), lambda p, i: (0, p), pipeline_mode=pl.Buffered(1)),
        ],
        out_specs=pl.BlockSpec((ROW_TILE, LANES), lambda p, i: (i, p)),
        out_shape=jax.ShapeDtypeStruct((lt, GROUP_WIDTH), F32),
        compiler_params=_cparams(("arbitrary", "arbitrary")),
    )(q, k, v)


def _diff_attn_kernel(q_ref, k_ref, v_ref, lam_ref, go_ref, o_ref, *, lt, tk, out_scale):
    is_ctx = pl.program_id(1) == 0
    lane = lax.broadcasted_iota(jnp.int32, (1, LANES), 1)
    q = q_ref[...]
    zero = jnp.zeros_like(q)
    q1 = jnp.where(lane < DIFF_QK, q, zero)
    q2 = jnp.where(lane < DIFF_QK, zero, q)
    (l1, a1), (l2, a2) = _attend([(q1, k_ref, v_ref), (q2, k_ref, v_ref)], is_ctx, lt, tk)
    o = a1 / l1 - lam_ref[...] * (a2 / l2)
    o_ref[...] = _rms(o) * go_ref[...] * out_scale


def _diff_attention(q, k, v, lam, g_o, out_scale):
    lt = q.shape[0]
    tk = _kv_chunk(lt)
    return pl.pallas_call(
        functools.partial(_diff_attn_kernel, lt=lt, tk=tk, out_scale=out_scale),
        grid=(DIFF_HEADS, lt // ROW_TILE),
        in_specs=[
            pl.BlockSpec((ROW_TILE, LANES), lambda h, i: (i, h)),
            pl.BlockSpec((lt, LANES), lambda h, i: (0, h), pipeline_mode=pl.Buffered(1)),
            pl.BlockSpec((lt, LANES), lambda h, i: (0, h), pipeline_mode=pl.Buffered(1)),
            _const_spec((1, LANES)),
            _const_spec((1, LANES)),
        ],
        out_specs=pl.BlockSpec((ROW_TILE, LANES), lambda h, i: (i, h)),
        out_shape=jax.ShapeDtypeStruct((lt, GROUP_WIDTH), F32),
        compiler_params=_cparams(("arbitrary", "arbitrary")),
    )(q, k, v, lam, g_o)


def _s5_matrices(lam_re, lam_im, log_dt, b_re, b_im, c_re, c_im, d_skip):
    t = S5_T
    g, n, ch = S5_GROUPS, S5_STATE, S5_CH
    dt = jnp.exp(log_dt)[:, :, None, None]
    tau = jnp.arange(t + 1, dtype=F32)
    mag = jnp.exp(lam_re[..., None] * dt * tau)
    ang = lam_im[..., None] * dt * tau
    p_re, p_im = mag * jnp.cos(ang), mag * jnp.sin(ang)
    a_re, a_im = p_re[..., 1], p_im[..., 1]
    den = lam_re * lam_re + lam_im * lam_im
    k_re = ((a_re - 1.0) * lam_re + a_im * lam_im) / den
    k_im = (a_im * lam_re - (a_re - 1.0) * lam_im) / den
    bb_re = k_re[..., None] * b_re - k_im[..., None] * b_im
    bb_im = k_re[..., None] * b_im + k_im[..., None] * b_re
    ca_re = c_re[..., None] * p_re[:, :, None] - c_im[..., None] * p_im[:, :, None]
    ca_im = c_re[..., None] * p_im[:, :, None] + c_im[..., None] * p_re[:, :, None]
    kk = (jnp.einsum('dgcnt,dgnk->dgtck', ca_re, bb_re, precision=HIGHEST)
          - jnp.einsum('dgcnt,dgnk->dgtck', ca_im, bb_im, precision=HIGHEST))
    ii = np.arange(t)
    lag = ii[None, :] - ii[:, None]
    mf = jnp.where((lag >= 0)[None, :, :, None, None], kk[0][:, np.clip(lag, 0, t)], 0.0)
    mr = jnp.where((lag <= 0)[None, :, :, None, None], kk[1][:, np.clip(-lag, 0, t)], 0.0)
    m = (mf + mr).transpose(0, 1, 4, 2, 3).reshape(g, t * ch, t * ch)

    def w_of(d, expo):
        pr = p_re[d][:, :, expo].transpose(0, 2, 1)[:, :, None, :]
        pi = p_im[d][:, :, expo].transpose(0, 2, 1)[:, :, None, :]
        br = bb_re[d].transpose(0, 2, 1)[:, None]
        bi = bb_im[d].transpose(0, 2, 1)[:, None]
        return pr * br - pi * bi, pr * bi + pi * br

    wf_re, wf_im = w_of(0, t - 1 - ii)
    wr_re, wr_im = w_of(1, ii)
    w4 = jnp.stack([wf_re, wf_im, wr_re, wr_im], axis=3)
    w4 = w4.reshape(g // S5_GPB, S5_GPB, t * ch, 4, n)

    def v_of(d, expo):
        gr = ca_re[d][:, :, :, expo].transpose(0, 2, 3, 1)
        gi = ca_im[d][:, :, :, expo].transpose(0, 2, 3, 1)
        return gr, -gi

    vf_re, vf_im = v_of(0, ii + 1)
    vr_re, vr_im = v_of(1, t - ii)
    v4 = jnp.stack([vf_re, vf_im, vr_re, vr_im], axis=0)
    v4 = v4.reshape(4, g // S5_GPB, S5_GPB, n, t * ch).transpose(1, 0, 2, 3, 4)
    eye = jnp.eye(S5_GPB, dtype=F32)
    nb = g // S5_GPB
    wz = jnp.einsum('rgkmn,gh->rgkmhn', w4, eye).reshape(nb, S5_GPB * t * ch, 4 * S5_GPB * n)
    vz = jnp.einsum('rmgnq,gh->rmgnhq', v4, eye).reshape(nb, 4 * S5_GPB * n, S5_GPB * t * ch)
    a16 = jnp.stack([p_re[0][..., t], p_im[0][..., t], p_re[1][..., t], p_im[1][..., t]], axis=0)
    a16 = a16.reshape(4, nb, S5_GPB * n).transpose(1, 0, 2)
    a16 = jnp.pad(a16, ((0, 0), (0, 4), (0, 0)))
    dtile = jnp.broadcast_to(d_skip[:, None, :], (g, t, ch)).reshape(nb, 1, S5_GPB * t * ch)
    return (m.reshape(nb, S5_GPB, t * ch, t * ch).astype(BF16), wz.astype(BF16), vz.astype(BF16), a16, dtile)


def _s5_kernel(u_ref, m_ref, wz_ref, vz_ref, a_ref, d_ref, y_ref, z_scr, s_scr, *, nblk, nctx):
    w = S5_GPB * S5_STATE
    u = u_ref[0]
    ub = u.astype(BF16)
    z_scr[...] = _dot(ub, wz_ref[0])
    a = a_ref[0]
    afr, afi, arr, ari = a[0:1], a[1:2], a[2:3], a[3:4]

    def step(cf, cr, carry):
        fr, fi, rr, ri = carry
        s_scr[pl.ds(cf, 1), 0:w] = fr
        s_scr[pl.ds(cf, 1), w:2 * w] = fi
        s_scr[pl.ds(cr, 1), 2 * w:3 * w] = rr
        s_scr[pl.ds(cr, 1), 3 * w:4 * w] = ri
        zfr = z_scr[pl.ds(cf, 1), 0:w]
        zfi = z_scr[pl.ds(cf, 1), w:2 * w]
        zrr = z_scr[pl.ds(cr, 1), 2 * w:3 * w]
        zri = z_scr[pl.ds(cr, 1), 3 * w:4 * w]
        return (afr * fr - afi * fi + zfr, afr * fi + afi * fr + zfi,
                arr * rr - ari * ri + zrr, arr * ri + ari * rr + zri)

    zero = jnp.zeros((1, w), F32)
    carry = lax.fori_loop(0, nctx, lambda k, c: step(k, nctx - 1 - k, c), (zero,) * 4)
    lax.fori_loop(0, nblk - nctx, lambda k, c: step(nctx + k, nblk - 1 - k, c), carry)
    y = u * d_ref[0] + _dot(s_scr[...].astype(BF16), vz_ref[0])
    bw = S5_T * S5_CH
    for gi in range(S5_GPB):
        blk = slice(gi * bw, (gi + 1) * bw)
        y_ref[0, :, blk] = y[:, blk] + _dot(ub[:, blk], m_ref[0, gi])


def _s5_mixer(u, mats):
    lt = u.shape[0]
    nblk = lt // S5_T
    nb = S5_GROUPS // S5_GPB
    wide = S5_GPB * S5_T * S5_CH
    m, wz, vz, a16, dtile = mats
    u4 = u.reshape(nblk, S5_T, nb, S5_GPB, S5_CH).transpose(2, 0, 3, 1, 4).reshape(nb, nblk, wide)
    y4 = pl.pallas_call(
        functools.partial(_s5_kernel, nblk=nblk, nctx=CTX_LEN // S5_T),
        grid=(nb,),
        in_specs=[
            pl.BlockSpec((1, nblk, wide), lambda r: (r, 0, 0)),
            pl.BlockSpec((1,) + m.shape[1:], lambda r: (r, 0, 0, 0)),
            pl.BlockSpec((1,) + wz.shape[1:], lambda r: (r, 0, 0)),
            pl.BlockSpec((1,) + vz.shape[1:], lambda r: (r, 0, 0)),
            pl.BlockSpec((1, 8, a16.shape[2]), lambda r: (r, 0, 0)),
            pl.BlockSpec((1, 1, wide), lambda r: (r, 0, 0)),
        ],
        out_specs=pl.BlockSpec((1, nblk, wide), lambda r: (r, 0, 0)),
        out_shape=jax.ShapeDtypeStruct((nb, nblk, wide), F32),
        scratch_shapes=[pltpu.VMEM((nblk, 4 * S5_GPB * S5_STATE), F32),
                        pltpu.VMEM((nblk, 4 * S5_GPB * S5_STATE), F32)],
        compiler_params=_cparams(("arbitrary",)),
    )(u4, m, wz, vz, a16, dtile)
    return y4.reshape(nb, nblk, S5_GPB, S5_T, S5_CH).transpose(1, 3, 0, 2, 4).reshape(lt, GROUP_WIDTH)


def _even_out_kernel(x_ref, att_ref, y_ref, modx_ref, modc_ref, wglu_ref, bglu_ref, wout_ref, o_ref):
    is_ctx = pl.program_id(0) == 0
    gate_a = jnp.where(is_ctx, modc_ref[2:3, :], modx_ref[2:3, :])
    g = _gelu_tanh(y_ref[...])
    ssm = g * _sigmoid(_dot(g.astype(BF16), wglu_ref[...]) + bglu_ref[...])
    mix = (_dot(att_ref[...].astype(BF16), wout_ref[0:GROUP_WIDTH, :])
           + _dot(ssm.astype(BF16), wout_ref[GROUP_WIDTH:, :]))
    o_ref[...] = x_ref[...] + gate_a * mix


def _even_out(xs, att, y, modx, modc, w_glu, b_glu, w_out):
    lt = xs.shape[0]
    row = lambda w: pl.BlockSpec((ROW_TILE, w), lambda i: (i, 0))
    return pl.pallas_call(
        _even_out_kernel,
        grid=(lt // ROW_TILE,),
        in_specs=[row(D_MODEL), row(GROUP_WIDTH), row(GROUP_WIDTH), _const_spec(modx.shape),
                  _const_spec(modc.shape), _const_spec(w_glu.shape), _const_spec((1, GROUP_WIDTH)),
                  _const_spec(w_out.shape)],
        out_specs=row(D_MODEL),
        out_shape=jax.ShapeDtypeStruct((lt, D_MODEL), F32),
        compiler_params=_cparams(("arbitrary",)),
    )(xs, att, y, modx, modc, w_glu.astype(BF16), b_glu.reshape(1, -1), w_out.astype(BF16))


def _odd_out_kernel(x_ref, att_ref, yf_ref, yr_ref, z_ref, modx_ref, modc_ref, gn_ref, wout_ref, o_ref):
    is_ctx = pl.program_id(0) == 0
    gate_a = jnp.where(is_ctx, modc_ref[2:3, :], modx_ref[2:3, :])
    gy = (yf_ref[...] + yr_ref[...]) * _silu(z_ref[...])
    gw = SSD_INNER // SSD_GROUPS
    parts = [_rms(gy[:, j * gw:(j + 1) * gw]) for j in range(SSD_GROUPS)]
    ssm = jnp.concatenate(parts, axis=-1) * gn_ref[...]
    mix = (_dot(att_ref[...].astype(BF16), wout_ref[0:GROUP_WIDTH, :])
           + _dot(ssm.astype(BF16), wout_ref[GROUP_WIDTH:, :]))
    o_ref[...] = x_ref[...] + gate_a * mix


def _odd_out(xs, att, yf, yr, z, modx, modc, g_norm, w_out):
    lt = xs.shape[0]
    row = lambda w: pl.BlockSpec((ROW_TILE, w), lambda i: (i, 0))
    return pl.pallas_call(
        _odd_out_kernel,
        grid=(lt // ROW_TILE,),
        in_specs=[row(D_MODEL), row(GROUP_WIDTH), row(SSD_INNER), row(SSD_INNER), row(SSD_INNER),
                  _const_spec(modx.shape), _const_spec(modc.shape), _const_spec((1, SSD_INNER)),
                  _const_spec(w_out.shape)],
        out_specs=row(D_MODEL),
        out_shape=jax.ShapeDtypeStruct((lt, D_MODEL), F32),
        compiler_params=_cparams(("arbitrary",)),
    )(xs, att, yf, yr, z, modx, modc, g_norm.reshape(1, -1), w_out.astype(BF16))


def _ffn_kernel(x_ref, modx_ref, modc_ref, w1_ref, w3_ref, w2_ref, o_ref):
    tm = x_ref.shape[0]
    row0 = pl.program_id(0) * tm
    x = x_ref[...]
    sh = _mod_rows(modx_ref, modc_ref, 3, row0, tm)
    sc = _mod_rows(modx_ref, modc_ref, 4, row0, tm)
    gate = _mod_rows(modx_ref, modc_ref, 5, row0, tm)
    h = (_rms(x) * (1.0 + sc) + sh).astype(BF16)
    acc = jnp.zeros((tm, D_MODEL), F32)
    for c in range(D_FF // FFN_CHUNK):
        blk = slice(c * FFN_CHUNK, (c + 1) * FFN_CHUNK)
        a = _dot(h, w1_ref[:, blk])
        b = _dot(h, w3_ref[:, blk])
        acc = acc + _dot((_silu(a) * b).astype(BF16), w2_ref[blk, :])
    o_ref[...] = x + gate * acc


def _ffn(xs, modx, modc, w1, w3, w2):
    lt = xs.shape[0]
    tm = FFN_TILE if lt % FFN_TILE == 0 else ROW_TILE
    row = pl.BlockSpec((tm, D_MODEL), lambda i: (i, 0))
    single = lambda shape: pl.BlockSpec(shape, lambda i: (0, 0), pipeline_mode=pl.Buffered(1))
    return pl.pallas_call(
        _ffn_kernel,
        grid=(lt // tm,),
        in_specs=[row, _const_spec(modx.shape), _const_spec(modc.shape), single(w1.shape),
                  single(w3.shape), single(w2.shape)],
        out_specs=row,
        out_shape=jax.ShapeDtypeStruct((lt, D_MODEL), F32),
        compiler_params=_cparams(("arbitrary",)),
    )(xs, modx, modc, w1.astype(BF16), w3.astype(BF16), w2.astype(BF16))


ODD_COLS = 2048 + SSD_XBC + LANES


def _odd_in_kernel(x_ref, modx_ref, modc_ref, win_ref, gq_ref, gk_ref, e_ref, et_ref, cos_ref, sina_ref,
                   sinb_ref, q_ref, k_ref, v_ref, z_ref, xbc_ref, dt_ref):
    i = pl.program_id(0)
    is_ctx = i == 0
    sh = jnp.where(is_ctx, modc_ref[0:1, :], modx_ref[0:1, :])
    sc = jnp.where(is_ctx, modc_ref[1:2, :], modx_ref[1:2, :])
    h = (_rms(x_ref[...]) * (1.0 + sc) + sh).astype(BF16)
    proj = _dot(h, win_ref[...])
    v_ref[...] = proj[:, 1024:1536].astype(BF16)
    z_ref[...] = proj[:, 1536:2048]
    xbc_ref[...] = proj[:, 2048:2048 + SSD_XBC]
    dt_ref[...] = proj[:, 2048 + SSD_XBC:2048 + SSD_XBC + 2 * SSD_HEADS]
    nrep = GROUP_WIDTH // LANES
    cos = jnp.concatenate([cos_ref[...]] * nrep, axis=-1)
    sina = jnp.concatenate([sina_ref[...]] * nrep, axis=-1)
    sinb = jnp.concatenate([sinb_ref[...]] * nrep, axis=-1)
    half = DIFF_QK // 4
    qscale = DIFF_QK ** -0.5 * LOG2E

    def prep(t, g_ref, scale):
        ss = jnp.dot(t * t, e_ref[...], precision=HIGHEST, preferred_element_type=F32)
        r = lax.rsqrt(ss * (1.0 / DIFF_QK) + EPS)
        rb = jnp.dot(r, et_ref[...], precision=HIGHEST, preferred_element_type=F32)
        tn = t * rb * g_ref[...]
        up = pltpu.roll(tn, GROUP_WIDTH - half, axis=1)
        dn = pltpu.roll(tn, half, axis=1)
        return ((tn * cos + up * sina + dn * sinb) * scale).astype(BF16)

    q_ref[...] = prep(proj[:, 0:512], gq_ref, qscale)
    k_ref[...] = prep(proj[:, 512:1024], gk_ref, 1.0)


def _odd_in(xs, modx, modc, w_in, g_q, g_k, cos1, sina1, sinb1):
    lt = xs.shape[0]
    win = jnp.pad(w_in, ((0, 0), (0, ODD_COLS - w_in.shape[1]))).astype(BF16)
    nblk = GROUP_WIDTH // DIFF_QK
    e = jnp.zeros((GROUP_WIDTH, LANES), F32).at[np.arange(GROUP_WIDTH), np.arange(GROUP_WIDTH) // DIFF_QK].set(1.0)
    gq = jnp.tile(g_q, nblk).reshape(1, -1)
    gk = jnp.tile(g_k, nblk).reshape(1, -1)
    args = (xs, modx, modc, win, gq, gk, e, e.T, cos1, sina1, sinb1)
    row = lambda w: pl.BlockSpec((ROW_TILE, w), lambda i: (i, 0))
    in_specs = [row(D_MODEL)] + [_const_spec(a.shape) for a in args[1:8]] + [row(LANES)] * 3
    ndt = 2 * SSD_HEADS
    return pl.pallas_call(
        _odd_in_kernel,
        grid=(lt // ROW_TILE,),
        in_specs=in_specs,
        out_specs=[row(512), row(512), row(512), row(512), row(SSD_XBC), row(ndt)],
        out_shape=[jax.ShapeDtypeStruct((lt, 512), BF16), jax.ShapeDtypeStruct((lt, 512), BF16),
                   jax.ShapeDtypeStruct((lt, 512), BF16), jax.ShapeDtypeStruct((lt, 512), F32),
                   jax.ShapeDtypeStruct((lt, SSD_XBC), F32), jax.ShapeDtypeStruct((lt, ndt), F32)],
        compiler_params=_cparams(("arbitrary",)),
    )(*args)


def _conv_kernel(prev_ref, cur_ref, next_ref, w_ref, b_ref, o_ref, ext_scr, *, ntiles):
    i = pl.program_id(0)
    pad = SSD_CONV // 2
    has_prev = i >= 2
    has_next = jnp.logical_and(i >= 1, i < ntiles - 1)
    ext_scr[0:8, :] = jnp.where(has_prev, prev_ref[...], 0.0)
    ext_scr[8:8 + ROW_TILE, :] = cur_ref[...]
    ext_scr[8 + ROW_TILE:16 + ROW_TILE, :] = jnp.where(has_next, next_ref[...], 0.0)
    acc = jnp.zeros((ROW_TILE, SSD_XBC), F32) + b_ref[...]
    for k in range(SSD_CONV):
        acc = acc + ext_scr[pl.ds(8 - pad + k, ROW_TILE), :] * w_ref[k:k + 1, :]
    o_ref[...] = _silu(acc)


def _ssd_conv(xbc, conv_w, conv_b):
    lt = xbc.shape[0]
    ntiles = lt // ROW_TILE
    per = ROW_TILE // 8
    last8 = lt // 8 - 1
    w = jnp.pad(conv_w, ((0, 8 - SSD_CONV), (0, 0)))
    return pl.pallas_call(
        functools.partial(_conv_kernel, ntiles=ntiles),
        grid=(ntiles,),
        in_specs=[
            pl.BlockSpec((8, SSD_XBC), lambda i: (jnp.maximum(i * per - 1, 0), 0)),
            pl.BlockSpec((ROW_TILE, SSD_XBC), lambda i: (i, 0)),
            pl.BlockSpec((8, SSD_XBC), lambda i: (jnp.minimum((i + 1) * per, last8), 0)),
            _const_spec((8, SSD_XBC)),
            _const_spec((1, SSD_XBC)),
        ],
        out_specs=pl.BlockSpec((ROW_TILE, SSD_XBC), lambda i: (i, 0)),
        out_shape=jax.ShapeDtypeStruct((lt, SSD_XBC), F32),
        scratch_shapes=[pltpu.VMEM((ROW_TILE + 16, SSD_XBC), F32)],
        compiler_params=_cparams(("arbitrary",)),
    )(xbc, xbc, xbc, w, conv_b.reshape(1, -1))


def _ssd_direction(xc, dtr, dtr_t, bias, bias_t, alog, alog_t, dskip, state_ref, rev):
    q = SSD_CHUNK
    hp = SSD_HEAD_DIM
    dt = _softplus(dtr + bias)
    dt_t = _softplus(dtr_t + bias_t)
    a = -jnp.exp(alog)
    a_t = -jnp.exp(alog_t)
    da = dt * a
    da_t = dt_t * a_t
    r_idx = lax.broadcasted_iota(jnp.int32, (q, q), 0)
    c_idx = lax.broadcasted_iota(jnp.int32, (q, q), 1)
    if rev:
        tri = (c_idx >= r_idx).astype(F32)
        keep = c_idx >= r_idx
    else:
        tri = (c_idx <= r_idx).astype(F32)
        keep = c_idx <= r_idx
    cs = jnp.dot(tri, da, precision=HIGHEST, preferred_element_type=F32)
    cs_t = lax.dot_general(da_t, tri, (((1,), (1,)), ((), ())), precision=HIGHEST,
                           preferred_element_type=F32)
    total = cs[0:1, :] if rev else cs[q - 1:q, :]
    x = xc[:, 0:SSD_INNER]
    gw = SSD_GROUPS * SSD_STATE
    bm = xc[:, SSD_INNER:SSD_INNER + gw].astype(BF16)
    cm = xc[:, SSD_INNER + gw:SSD_INNER + 2 * gw].astype(BF16)
    hpg = SSD_HEADS // SSD_GROUPS
    outs = []
    for g in range(SSD_GROUPS):
        bg = bm[:, g * SSD_STATE:(g + 1) * SSD_STATE]
        cg = cm[:, g * SSD_STATE:(g + 1) * SSD_STATE]
        scores = lax.dot_general(cg, bg, (((1,), (1,)), ((), ())), preferred_element_type=F32)
        cols = slice(g * hpg * hp, (g + 1) * hpg * hp)
        s_in = state_ref[:, cols]
        y_off = _dot(cg, s_in.astype(BF16))
        xdt_parts, xdec_parts, ydiag_parts, scale_parts = [], [], [], []
        for j in range(hpg):
            hd = g * hpg + j
            xh = x[:, hd * hp:(hd + 1) * hp]
            xdt = xh * dt[:, hd:hd + 1]
            lmat = jnp.exp(jnp.where(keep, cs[:, hd:hd + 1] - cs_t[hd:hd + 1, :], -jnp.inf))
            ydiag_parts.append(_dot((scores * lmat).astype(BF16), xdt.astype(BF16)))
            xdec_parts.append(xdt * jnp.exp(total[:, hd:hd + 1] - cs[:, hd:hd + 1]))
            scale_parts.append(jnp.broadcast_to(jnp.exp(cs[:, hd:hd + 1]), (q, hp)))
            xdt_parts.append(xh * dskip[:, hd:hd + 1] if dskip is not None else None)
        y = jnp.concatenate(ydiag_parts, axis=-1) + y_off * jnp.concatenate(scale_parts, axis=-1)
        if dskip is not None:
            y = y + jnp.concatenate(xdt_parts, axis=-1)
        outs.append(y)
        xdec = jnp.concatenate(xdec_parts, axis=-1).astype(BF16)
        new = lax.dot_general(bg, xdec, (((0,), (0,)), ((), ())), preferred_element_type=F32)
        chunk_decay = jnp.concatenate(
            [jnp.broadcast_to(jnp.exp(total[:, g * hpg + j:g * hpg + j + 1]), (1, hp)) for j in range(hpg)],
            axis=-1)
        state_ref[:, cols] = s_in * chunk_decay + new
    return jnp.concatenate(outs, axis=-1)


def _ssd_kernel(xf_ref, dtf_ref, dtft_ref, xr_ref, dtr_ref, dtrt_ref, bias_ref, biast_ref, alog_ref,
                alogt_ref, dskip_ref, yf_ref, yr_ref, sf_scr, sr_scr):
    @pl.when(pl.program_id(0) == 0)
    def _():
        sf_scr[...] = jnp.zeros_like(sf_scr)
        sr_scr[...] = jnp.zeros_like(sr_scr)

    h = SSD_HEADS
    yf_ref[...] = _ssd_direction(xf_ref[...], dtf_ref[:, 0:h], dtft_ref[0, 0:h, :], bias_ref[0:1, :],
                                 biast_ref[:, 0:1], alog_ref[0:1, :], alogt_ref[:, 0:1], dskip_ref[...],
                                 sf_scr, False)
    yr_ref[...] = _ssd_direction(xr_ref[...], dtr_ref[:, h:2 * h], dtrt_ref[0, h:2 * h, :], bias_ref[1:2, :],
                                 biast_ref[:, 1:2], alog_ref[1:2, :], alogt_ref[:, 1:2], None, sr_scr, True)


def _ssd_scan(xc, dtr, dt_bias, a_log, d_skip):
    lt = xc.shape[0]
    nc = lt // SSD_CHUNK
    nctx = CTX_LEN // SSD_CHUNK
    ndt = 2 * SSD_HEADS
    dtr_t = dtr.reshape(nc, SSD_CHUNK, ndt).transpose(0, 2, 1)

    def rev_chunk(k):
        return jnp.where(k < nctx, nctx - 1 - k, nc - 1 + nctx - k)

    fwd = lambda w: pl.BlockSpec((SSD_CHUNK, w), lambda k: (k, 0))
    bwd = lambda w: pl.BlockSpec((SSD_CHUNK, w), lambda k: (rev_chunk(k), 0))
    small = (dt_bias, dt_bias.T, a_log, a_log.T, d_skip.reshape(1, -1))
    return pl.pallas_call(
        _ssd_kernel,
        grid=(nc,),
        in_specs=[fwd(SSD_XBC), fwd(ndt), pl.BlockSpec((1, ndt, SSD_CHUNK), lambda k: (k, 0, 0)),
                  bwd(SSD_XBC), bwd(ndt), pl.BlockSpec((1, ndt, SSD_CHUNK), lambda k: (rev_chunk(k), 0, 0))]
                 + [_const_spec(a.shape) for a in small],
        out_specs=[fwd(SSD_INNER), bwd(SSD_INNER)],
        out_shape=[jax.ShapeDtypeStruct((lt, SSD_INNER), F32)] * 2,
        scratch_shapes=[pltpu.VMEM((SSD_STATE, SSD_INNER), F32)] * 2,
        compiler_params=_cparams(("arbitrary",)),
    )(xc, dtr, dtr_t, xc, dtr, dtr_t, *small)


def _rope_tables(seq):
    t = np.arange(seq)
    row, col = (t // GRID_W).astype(np.float32), (t % GRID_W).astype(np.float32)

    def angles(n):
        inv = ROPE_THETA ** (-jnp.arange(n, dtype=F32) / n)
        return jnp.asarray(row)[:, None] * inv[None, :], jnp.asarray(col)[:, None] * inv[None, :]

    def with_ctx(tab, fill):
        return jnp.concatenate([jnp.full((CTX_LEN, tab.shape[1]), fill, F32), tab], axis=0)

    ar, ac = angles(MLA_ROPE // 4)
    ones = jnp.ones((seq, MLA_NOPE), F32)
    zeros = jnp.zeros((seq, MLA_NOPE), F32)
    tail1 = jnp.ones((seq, LANES - MLA_QK), F32)
    tail0 = jnp.zeros((seq, LANES - MLA_QK), F32)
    cos0 = jnp.concatenate([ones, jnp.cos(ar), jnp.cos(ar), jnp.cos(ac), jnp.cos(ac), tail1], axis=1)
    sin0 = jnp.concatenate([zeros, -jnp.sin(ar), jnp.sin(ar), -jnp.sin(ac), jnp.sin(ac), tail0], axis=1)
    br, bc = angles(DIFF_QK // 4)
    z16 = jnp.zeros_like(br)
    cos1 = jnp.concatenate([jnp.cos(br), jnp.cos(br), jnp.cos(bc), jnp.cos(bc)] * 2, axis=1)
    sina1 = jnp.concatenate([-jnp.sin(br), z16, -jnp.sin(bc), z16] * 2, axis=1)
    sinb1 = jnp.concatenate([z16, jnp.sin(br), z16, jnp.sin(bc)] * 2, axis=1)
    return (with_ctx(cos0, 1.0), with_ctx(sin0, 0.0), with_ctx(cos1, 1.0), with_ctx(sina1, 0.0),
            with_ctx(sinb1, 0.0))


def kernel(x, c, ctx, c_ctx, ada_w, ada_b, ffn_w1, ffn_w3, ffn_w2, e_w_in, e_w_out, mla_g_qa, mla_w_qb, mla_g_kva, mla_w_kvb, mla_g_qn, mla_g_kn, s5_lam_re, s5_lam_im, s5_log_dt, s5_b_re, s5_b_im, s5_c_re, s5_c_im, s5_d, s5_w_glu, s5_b_glu, o_w_in, o_w_out, diff_g_q, diff_g_k, diff_lq1, diff_lk1, diff_lq2, diff_lk2, diff_g_o, ssd_conv_w, ssd_conv_b, ssd_dt_bias, ssd_a_log, ssd_d, ssd_g):
    depth = ada_w.shape[0]
    seq = x.shape[1]
    assert x.shape[0] == 1 and ctx.shape[1] == CTX_LEN and seq % ROW_TILE == 0
    xs = jnp.concatenate([ctx[0], x[0]], axis=0)
    mods = _mod_vectors(c, c_ctx, ada_w, ada_b)
    cos0, sin0, cos1, sina1, sinb1 = _rope_tables(seq)
    for i in range(depth):
        j = i // 2
        modx, modc = mods[i, 0], mods[i, 1]
        if i % 2 == 0:
            q, k, v, u = _even_in(xs, modx, modc, e_w_in[j], mla_g_qa[j], mla_w_qb[j], mla_g_kva[j],
                                  mla_w_kvb[j], mla_g_qn[j], mla_g_kn[j], cos0, sin0)
            att = _mla_attention(q, k, v)
            mats = _s5_matrices(s5_lam_re[j], s5_lam_im[j], s5_log_dt[j], s5_b_re[j], s5_b_im[j],
                                s5_c_re[j], s5_c_im[j], s5_d[j])
            y = _s5_mixer(u, mats)
            xs = _even_out(xs, att, y, modx, modc, s5_w_glu[j], s5_b_glu[j], e_w_out[j])
        else:
            q, k, v, z, xbc, dtr = _odd_in(xs, modx, modc, o_w_in[j], diff_g_q[j], diff_g_k[j],
                                           cos1, sina1, sinb1)
            lam_init = 0.8 - 0.6 * math.exp(-0.3 * i)
            lam = (jnp.exp(jnp.sum(diff_lq1[j] * diff_lk1[j])) - jnp.exp(jnp.sum(diff_lq2[j] * diff_lk2[j]))
                   + lam_init)
            att = _diff_attention(q, k, v, jnp.full((1, LANES), lam, F32), diff_g_o[j].reshape(1, -1),
                                  1.0 - lam_init)
            xc = _ssd_conv(xbc, ssd_conv_w[j], ssd_conv_b[j])
            yf, yr = _ssd_scan(xc, dtr, ssd_dt_bias[j], ssd_a_log[j], ssd_d[j])
            xs = _odd_out(xs, att, yf, yr, z, modx, modc, ssd_g[j], o_w_out[j])
        xs = _ffn(xs, modx, modc, ffn_w1[i], ffn_w3[i], ffn_w2[i])
    return xs[CTX_LEN:][None]
```

```python
import functools
import math

import numpy as np
import jax
import jax.numpy as jnp
from jax import lax
from jax.experimental import pallas as pl
from jax.experimental.pallas import tpu as pltpu

F32 = jnp.float32
BF16 = jnp.bfloat16
HIGHEST = lax.Precision.HIGHEST

D_MODEL = 1024
CTX_LEN = 256
GRID_W = 64
GROUP_WIDTH = 512
D_FF = 2816
EPS = 1e-6
ROPE_THETA = 10000.0
LOG2E = math.log2(math.e)

MLA_HEADS, MLA_NOPE, MLA_ROPE, MLA_V = 8, 64, 32, 64
MLA_QK = MLA_NOPE + MLA_ROPE
MLA_Q_RANK, MLA_KV_RANK = 384, 256
S5_CH, S5_GROUPS, S5_STATE = 16, 32, 64
S5_T = 16
S5_GPB = 4
DIFF_HEADS, DIFF_QK, DIFF_V = 4, 64, 128
SSD_HEADS, SSD_HEAD_DIM, SSD_GROUPS, SSD_STATE, SSD_CONV, SSD_CHUNK = 8, 64, 2, 128, 5, 128
SSD_INNER = 512
SSD_XBC = SSD_INNER + 2 * SSD_GROUPS * SSD_STATE

LANES = 128
ROW_TILE = 256
FFN_TILE = 640
FFN_CHUNK = 256
KV_UNROLL = 4
VMEM_LIMIT = 56 * 1024 * 1024


def _cparams(sem):
    return pltpu.CompilerParams(dimension_semantics=sem, vmem_limit_bytes=VMEM_LIMIT)


def _dot(a, b):
    return jnp.dot(a, b, preferred_element_type=F32)


def _rms(x):
    return x * lax.rsqrt(jnp.mean(x * x, axis=-1, keepdims=True) + EPS)


def _sigmoid(x):
    return 1.0 / (1.0 + jnp.exp(-x))


def _silu(x):
    return x * _sigmoid(x)


def _gelu_tanh(x):
    return 0.5 * x * (1.0 + jnp.tanh(math.sqrt(2.0 / math.pi) * (x + 0.044715 * (x * x * x))))


def _softplus(x):
    return jnp.maximum(x, 0.0) + jnp.log(1.0 + jnp.exp(-jnp.abs(x)))


def _const_spec(shape):
    nd = len(shape)
    return pl.BlockSpec(shape, lambda *_: (0,) * nd)


def _mod_kernel(c_ref, w_ref, b_ref, o_ref):
    s = _silu(c_ref[...]).astype(BF16)
    o_ref[0] = _dot(s, w_ref[0].astype(BF16)) + b_ref[0]


def _mod_vectors(c, c_ctx, ada_w, ada_b):
    depth = ada_w.shape[0]
    cc = jnp.zeros((8, D_MODEL), F32).at[0].set(c[0]).at[1].set(c_ctx)
    nblk = 6
    out = pl.pallas_call(
        _mod_kernel,
        grid=(depth, nblk),
        in_specs=[
            pl.BlockSpec((8, D_MODEL), lambda i, j: (0, 0)),
            pl.BlockSpec((1, D_MODEL, D_MODEL), lambda i, j: (i, 0, j)),
            pl.BlockSpec((1, 1, D_MODEL), lambda i, j: (i, 0, j)),
        ],
        out_specs=pl.BlockSpec((1, 8, D_MODEL), lambda i, j: (i, 0, j)),
        out_shape=jax.ShapeDtypeStruct((depth, 8, 6 * D_MODEL), F32),
        compiler_params=_cparams(("arbitrary", "arbitrary")),
    )(cc, ada_w, ada_b.reshape(depth, 1, 6 * D_MODEL))
    mods = out[:, :2].reshape(depth, 2, 6, D_MODEL)
    return jnp.pad(mods, ((0, 0), (0, 0), (0, 2), (0, 0)))


def _mod_rows(modx_ref, modc_ref, k, row0, nrows):
    rows = row0 + lax.broadcasted_iota(jnp.int32, (nrows, 1), 0)
    return jnp.where(rows < CTX_LEN, modc_ref[k:k + 1, :], modx_ref[k:k + 1, :])


def _rope_partner(n_half):
    return np.array([i + n_half if i < n_half else i - n_half for i in range(2 * n_half)])


def _even_in_kernel(x_ref, modx_ref, modc_ref, win_ref, gqa_ref, wq_ref, wqs_ref, gkva_ref, wk_ref,
                    wv_ref, gq_ref, gqs_ref, gk_ref, gks_ref, cos_ref, sin_ref,
                    q_ref, k_ref, v_ref, u_ref):
    i = pl.program_id(0)
    is_ctx = i == 0
    sh = jnp.where(is_ctx, modc_ref[0:1, :], modx_ref[0:1, :])
    sc = jnp.where(is_ctx, modc_ref[1:2, :], modx_ref[1:2, :])
    h = (_rms(x_ref[...]) * (1.0 + sc) + sh).astype(BF16)
    proj = _dot(h, win_ref[...])
    cq = proj[:, 0:384]
    ckv = proj[:, 384:640]
    u_ref[...] = proj[:, 640:1152]
    krb = proj[:, 1152:1280]
    krs = proj[:, 1280:1408]
    cqn = (_rms(cq) * gqa_ref[...]).astype(BF16)
    qf = _dot(cqn, wq_ref[...])
    qs = _dot(cqn, wqs_ref[...])
    ckvn = (_rms(ckv) * gkva_ref[...]).astype(BF16)
    kf = _dot(ckvn, wk_ref[...])
    vf = _dot(ckvn, wv_ref[...])
    lane = lax.broadcasted_iota(jnp.int32, (1, LANES), 1)
    for pr in range(MLA_HEADS // 2):
        vp = vf[:, pr * LANES:(pr + 1) * LANES]
        v_ref[:, (2 * pr) * LANES:(2 * pr + 1) * LANES] = jnp.where(lane < MLA_V, vp, 1.0).astype(BF16)
        v_ref[:, (2 * pr + 1) * LANES:(2 * pr + 2) * LANES] = jnp.where(lane < MLA_V, 1.0, vp).astype(BF16)
    cos = cos_ref[...]
    sin = sin_ref[...]
    qscale = MLA_QK ** -0.5 * LOG2E
    inv_n = 1.0 / MLA_QK
    for hd in range(MLA_HEADS):
        blk = slice(hd * LANES, (hd + 1) * LANES)
        qh = qf[:, blk]
        rq = lax.rsqrt(jnp.sum(qh * qh, axis=-1, keepdims=True) * inv_n + EPS) * qscale
        qo = (qh * (gq_ref[...] * cos) + qs[:, blk] * (gqs_ref[...] * sin)) * rq
        q_ref[:, blk] = qo.astype(BF16)
        kh = kf[:, blk] + krb
        rk = lax.rsqrt(jnp.sum(kh * kh, axis=-1, keepdims=True) * inv_n + EPS)
        ko = (kh * (gk_ref[...] * cos) + krs * (gks_ref[...] * sin)) * rk
        k_ref[:, blk] = ko.astype(BF16)


def _even_in(xs, modx, modc, w_in, g_qa, w_qb, g_kva, w_kvb, g_qn, g_kn, cos0, sin0):
    lt = xs.shape[0]
    perm = _rope_partner(MLA_ROPE // 4)
    perm = np.concatenate([perm, perm + MLA_ROPE // 2])
    cq_w, ckv_w = w_in[:, :384], w_in[:, 384:640]
    kr_w, u_w = w_in[:, 640:672], w_in[:, 672:]
    zblk = jnp.zeros((D_MODEL, LANES), F32)
    krblk = zblk.at[:, MLA_NOPE:MLA_QK].set(kr_w)
    krsblk = zblk.at[:, MLA_NOPE:MLA_QK].set(kr_w[:, perm])
    win = jnp.concatenate([cq_w, ckv_w, u_w, krblk, krsblk], axis=1).astype(BF16)
    wq3 = w_qb.reshape(MLA_Q_RANK, MLA_HEADS, MLA_QK)
    wq = jnp.zeros((MLA_Q_RANK, MLA_HEADS, LANES), F32).at[:, :, :MLA_QK].set(wq3)
    wqs = jnp.zeros((MLA_Q_RANK, MLA_HEADS, LANES), F32).at[:, :, MLA_NOPE:MLA_QK].set(
        wq3[:, :, MLA_NOPE:][:, :, perm])
    wkv3 = w_kvb.reshape(MLA_KV_RANK, MLA_HEADS, MLA_NOPE + MLA_V)
    wk = jnp.zeros((MLA_KV_RANK, MLA_HEADS, LANES), F32).at[:, :, :MLA_NOPE].set(wkv3[:, :, :MLA_NOPE])
    wv = wkv3[:, :, MLA_NOPE:].reshape(MLA_KV_RANK, MLA_HEADS * MLA_V)

    def pad_gain(g):
        gp = jnp.zeros((1, LANES), F32).at[0, :MLA_QK].set(g)
        gs = jnp.zeros((1, LANES), F32).at[0, MLA_NOPE:MLA_QK].set(g[MLA_NOPE:][perm])
        return gp, gs

    gq, gqs = pad_gain(g_qn)
    gk, gks = pad_gain(g_kn)
    hw = MLA_HEADS * LANES
    args = (xs, modx, modc, win, g_qa.reshape(1, -1), wq.reshape(MLA_Q_RANK, hw).astype(BF16),
            wqs.reshape(MLA_Q_RANK, hw).astype(BF16), g_kva.reshape(1, -1),
            wk.reshape(MLA_KV_RANK, hw).astype(BF16), wv.astype(BF16), gq, gqs, gk, gks, cos0, sin0)
    row = lambda w: pl.BlockSpec((ROW_TILE, w), lambda i: (i, 0))
    in_specs = [row(D_MODEL)] + [_const_spec(a.shape) for a in args[1:14]] + [row(LANES), row(LANES)]
    return pl.pallas_call(
        _even_in_kernel,
        grid=(lt // ROW_TILE,),
        in_specs=in_specs,
        out_specs=[row(hw), row(hw), row(hw), row(GROUP_WIDTH)],
        out_shape=[jax.ShapeDtypeStruct((lt, hw), BF16), jax.ShapeDtypeStruct((lt, hw), BF16),
                   jax.ShapeDtypeStruct((lt, hw), BF16),
                   jax.ShapeDtypeStruct((lt, GROUP_WIDTH), F32)],
        compiler_params=_cparams(("arbitrary",)),
    )(*args)


def _kv_chunk(lt):
    for tk in (1280, 640, 256):
        if lt % tk == 0:
            return tk
    raise ValueError(lt)


def _scores(q, k_ref, start, size):
    kc = k_ref[pl.ds(start, size), :]
    return lax.dot_general(q, kc, (((1,), (1,)), ((), ())), preferred_element_type=F32)


def _softmax_update(s_ref, v_ref, start, size, carry):
    m, acc = carry
    mn = jnp.maximum(m, jnp.max(s_ref[...], axis=-1, keepdims=True))
    alpha = jnp.exp2(m - mn)
    p = jnp.exp2(s_ref[...] - mn)
    acc = alpha * acc + _dot(p.astype(BF16), v_ref[pl.ds(start, size), :])
    return mn, acc


def _attend(streams, s_scr, is_ctx, lt, tk):
    tq = streams[0][0].shape[0]
    n = lt // tk
    init = tuple((jnp.full((tq, 1), -jnp.inf, F32), jnp.zeros((tq, v_ref.shape[-1]), F32))
                 for _, _, v_ref in streams)

    def consume(chunk, slot, carries, prefetch):
        start = chunk * tk
        if not isinstance(start, int):
            start = pl.multiple_of(start, tk)
        out = []
        for t, ((q, k_ref, v_ref), c) in enumerate(zip(streams, carries)):
            if prefetch:
                s_scr[2 * t + 1 - slot] = _scores(q, k_ref, start + tk, tk)
            out.append(_softmax_update(s_scr.at[2 * t + slot], v_ref, start, tk, c))
        return tuple(out)

    def group(i, carries):
        for u in range(KV_UNROLL):
            carries = consume(KV_UNROLL * i + u, u % 2, carries, True)
        return carries

    def ctx_branch():
        out = []
        for t, ((q, k_ref, v_ref), c) in enumerate(zip(streams, init)):
            s_scr[2 * t, :, 0:CTX_LEN] = _scores(q, k_ref, 0, CTX_LEN)
            out.append(_softmax_update(s_scr.at[2 * t, :, 0:CTX_LEN], v_ref, 0, CTX_LEN, c))
        return tuple(out)

    def full_branch():
        for t, (q, k_ref, _) in enumerate(streams):
            s_scr[2 * t] = _scores(q, k_ref, 0, tk)
        ngroups = (n - 1) // KV_UNROLL
        carries = lax.fori_loop(0, ngroups, group, init) if ngroups else init
        for chunk in range(KV_UNROLL * ngroups, n):
            carries = consume(chunk, chunk % 2, carries, chunk < n - 1)
        return carries

    out = lax.cond(is_ctx, ctx_branch, full_branch)
    return [acc for (_, acc) in out]


def _mla_attn_kernel(q_ref, k_ref, v_ref, o_ref, s_scr, *, lt, tk):
    is_ctx = pl.program_id(1) == 0
    lane = lax.broadcasted_iota(jnp.int32, (1, LANES), 1)
    blks = [slice(hh * LANES, (hh + 1) * LANES) for hh in range(2)]
    acc_a, acc_b = _attend([(q_ref[:, b], k_ref.at[:, b], v_ref.at[:, b]) for b in blks], s_scr, is_ctx, lt, tk)
    o_ref[...] = jnp.where(lane < MLA_V, acc_a / acc_a[:, MLA_V:MLA_V + 1], acc_b / acc_b[:, 0:1])


def _mla_attention(q, k, v):
    lt = q.shape[0]
    tk = _kv_chunk(lt)
    npair = MLA_HEADS // 2
    return pl.pallas_call(
        functools.partial(_mla_attn_kernel, lt=lt, tk=tk),
        grid=(npair, lt // ROW_TILE),
        in_specs=[
            pl.BlockSpec((ROW_TILE, 2 * LANES), lambda p, i: (i, p)),
            pl.BlockSpec((lt, 2 * LANES), lambda p, i: (0, p), pipeline_mode=pl.Buffered(1)),
            pl.BlockSpec((lt, 2 * LANES), lambda p, i: (0, p), pipeline_mode=pl.Buffered(1)),
        ],
        out_specs=pl.BlockSpec((ROW_TILE, LANES), lambda p, i: (i, p)),
        out_shape=jax.ShapeDtypeStruct((lt, GROUP_WIDTH), F32),
        scratch_shapes=[pltpu.VMEM((4, ROW_TILE, tk), F32)],
        compiler_params=_cparams(("arbitrary", "arbitrary")),
    )(q, k, v)


def _diff_attn_kernel(q_ref, k_ref, v_ref, lam_ref, go_ref, o_ref, s_scr, *, lt, tk, out_scale):
    is_ctx = pl.program_id(1) == 0
    lane = lax.broadcasted_iota(jnp.int32, (1, LANES), 1)
    q = q_ref[...]
    zero = jnp.zeros_like(q)
    q1 = jnp.where(lane < DIFF_QK, q, zero)
    q2 = jnp.where(lane < DIFF_QK, zero, q)
    a1, a2 = _attend([(q1, k_ref, v_ref), (q2, k_ref, v_ref)], s_scr, is_ctx, lt, tk)
    o = (a1[:, 0:DIFF_V] / a1[:, DIFF_V:DIFF_V + 1]
         - lam_ref[...] * (a2[:, 0:DIFF_V] / a2[:, DIFF_V:DIFF_V + 1]))
    o_ref[...] = _rms(o) * go_ref[...] * out_scale


def _diff_attention(q, k, v, lam, g_o, out_scale):
    lt = q.shape[0]
    tk = _kv_chunk(lt)
    return pl.pallas_call(
        functools.partial(_diff_attn_kernel, lt=lt, tk=tk, out_scale=out_scale),
        grid=(DIFF_HEADS, lt // ROW_TILE),
        in_specs=[
            pl.BlockSpec((ROW_TILE, LANES), lambda h, i: (i, h)),
            pl.BlockSpec((lt, LANES), lambda h, i: (0, h), pipeline_mode=pl.Buffered(1)),
            pl.BlockSpec((lt, 2 * LANES), lambda h, i: (0, h), pipeline_mode=pl.Buffered(1)),
            _const_spec((1, LANES)),
            _const_spec((1, LANES)),
        ],
        out_specs=pl.BlockSpec((ROW_TILE, LANES), lambda h, i: (i, h)),
        out_shape=jax.ShapeDtypeStruct((lt, GROUP_WIDTH), F32),
        scratch_shapes=[pltpu.VMEM((4, ROW_TILE, tk), F32)],
        compiler_params=_cparams(("arbitrary", "arbitrary")),
    )(q, k, v, lam, g_o)


def _s5_matrices(lam_re, lam_im, log_dt, b_re, b_im, c_re, c_im, d_skip):
    t = S5_T
    g, n, ch = S5_GROUPS, S5_STATE, S5_CH
    dt = jnp.exp(log_dt)[:, :, None, None]
    tau = jnp.arange(t + 1, dtype=F32)
    mag = jnp.exp(lam_re[..., None] * dt * tau)
    ang = lam_im[..., None] * dt * tau
    p_re, p_im = mag * jnp.cos(ang), mag * jnp.sin(ang)
    a_re, a_im = p_re[..., 1], p_im[..., 1]
    den = lam_re * lam_re + lam_im * lam_im
    k_re = ((a_re - 1.0) * lam_re + a_im * lam_im) / den
    k_im = (a_im * lam_re - (a_re - 1.0) * lam_im) / den
    bb_re = k_re[..., None] * b_re - k_im[..., None] * b_im
    bb_im = k_re[..., None] * b_im + k_im[..., None] * b_re
    ca_re = c_re[..., None] * p_re[:, :, None] - c_im[..., None] * p_im[:, :, None]
    ca_im = c_re[..., None] * p_im[:, :, None] + c_im[..., None] * p_re[:, :, None]
    kk = (jnp.einsum('dgcnt,dgnk->dgtck', ca_re, bb_re, precision=HIGHEST)
          - jnp.einsum('dgcnt,dgnk->dgtck', ca_im, bb_im, precision=HIGHEST))
    ii = np.arange(t)
    lag = ii[None, :] - ii[:, None]
    mf = jnp.where((lag >= 0)[None, :, :, None, None], kk[0][:, np.clip(lag, 0, t)], 0.0)
    mr = jnp.where((lag <= 0)[None, :, :, None, None], kk[1][:, np.clip(-lag, 0, t)], 0.0)
    m = (mf + mr).transpose(0, 1, 4, 2, 3).reshape(g, t * ch, t * ch)

    def w_of(d, expo):
        pr = p_re[d][:, :, expo].transpose(0, 2, 1)[:, :, None, :]
        pi = p_im[d][:, :, expo].transpose(0, 2, 1)[:, :, None, :]
        br = bb_re[d].transpose(0, 2, 1)[:, None]
        bi = bb_im[d].transpose(0, 2, 1)[:, None]
        return pr * br - pi * bi, pr * bi + pi * br

    wf_re, wf_im = w_of(0, t - 1 - ii)
    wr_re, wr_im = w_of(1, ii)
    w4 = jnp.stack([wf_re, wf_im, wr_re, wr_im], axis=3)
    w4 = w4.reshape(g // S5_GPB, S5_GPB, t * ch, 4, n)

    def v_of(d, expo):
        gr = ca_re[d][:, :, :, expo].transpose(0, 2, 3, 1)
        gi = ca_im[d][:, :, :, expo].transpose(0, 2, 3, 1)
        return gr, -gi

    vf_re, vf_im = v_of(0, ii + 1)
    vr_re, vr_im = v_of(1, t - ii)
    v4 = jnp.stack([vf_re, vf_im, vr_re, vr_im], axis=0)
    v4 = v4.reshape(4, g // S5_GPB, S5_GPB, n, t * ch).transpose(1, 0, 2, 3, 4)
    eye = jnp.eye(S5_GPB, dtype=F32)
    nb = g // S5_GPB
    wz = jnp.einsum('rgkmn,gh->rgkmhn', w4, eye).reshape(nb, S5_GPB * t * ch, 4 * S5_GPB * n)
    vz = jnp.einsum('rmgnq,gh->rmgnhq', v4, eye).reshape(nb, 4 * S5_GPB * n, S5_GPB * t * ch)
    a16 = jnp.stack([p_re[0][..., t], p_im[0][..., t], p_re[1][..., t], p_im[1][..., t]], axis=0)
    a16 = a16.reshape(4, nb, S5_GPB * n).transpose(1, 0, 2)
    a16 = jnp.pad(a16, ((0, 0), (0, 4), (0, 0)))
    dtile = jnp.broadcast_to(d_skip[:, None, :], (g, t, ch)).reshape(nb, 1, S5_GPB * t * ch)
    return (m.reshape(nb, S5_GPB, t * ch, t * ch).astype(BF16), wz.astype(BF16), vz.astype(BF16), a16, dtile)


def _s5_kernel(u_ref, m_ref, wz_ref, vz_ref, a_ref, d_ref, y_ref, z_scr, s_scr, *, nblk, nctx):
    w = S5_GPB * S5_STATE
    u = u_ref[0]
    ub = u.astype(BF16)
    z_scr[...] = _dot(ub, wz_ref[0])
    a = a_ref[0]
    afr, afi, arr, ari = a[0:1], a[1:2], a[2:3], a[3:4]

    def step(cf, cr, carry):
        fr, fi, rr, ri = carry
        s_scr[pl.ds(cf, 1), 0:w] = fr
        s_scr[pl.ds(cf, 1), w:2 * w] = fi
        s_scr[pl.ds(cr, 1), 2 * w:3 * w] = rr
        s_scr[pl.ds(cr, 1), 3 * w:4 * w] = ri
        zfr = z_scr[pl.ds(cf, 1), 0:w]
        zfi = z_scr[pl.ds(cf, 1), w:2 * w]
        zrr = z_scr[pl.ds(cr, 1), 2 * w:3 * w]
        zri = z_scr[pl.ds(cr, 1), 3 * w:4 * w]
        return (afr * fr - afi * fi + zfr, afr * fi + afi * fr + zfi,
                arr * rr - ari * ri + zrr, arr * ri + ari * rr + zri)

    zero = jnp.zeros((1, w), F32)
    carry = lax.fori_loop(0, nctx, lambda k, c: step(k, nctx - 1 - k, c), (zero,) * 4)
    lax.fori_loop(0, nblk - nctx, lambda k, c: step(nctx + k, nblk - 1 - k, c), carry)
    y = u * d_ref[0] + _dot(s_scr[...].astype(BF16), vz_ref[0])
    bw = S5_T * S5_CH
    for gi in range(S5_GPB):
        blk = slice(gi * bw, (gi + 1) * bw)
        y_ref[0, :, blk] = y[:, blk] + _dot(ub[:, blk], m_ref[0, gi])


def _s5_mixer(u, mats):
    lt = u.shape[0]
    nblk = lt // S5_T
    nb = S5_GROUPS // S5_GPB
    wide = S5_GPB * S5_T * S5_CH
    m, wz, vz, a16, dtile = mats
    u4 = u.reshape(nblk, S5_T, nb, S5_GPB, S5_CH).transpose(2, 0, 3, 1, 4).reshape(nb, nblk, wide)
    y4 = pl.pallas_call(
        functools.partial(_s5_kernel, nblk=nblk, nctx=CTX_LEN // S5_T),
        grid=(nb,),
        in_specs=[
            pl.BlockSpec((1, nblk, wide), lambda r: (r, 0, 0)),
            pl.BlockSpec((1,) + m.shape[1:], lambda r: (r, 0, 0, 0)),
            pl.BlockSpec((1,) + wz.shape[1:], lambda r: (r, 0, 0)),
            pl.BlockSpec((1,) + vz.shape[1:], lambda r: (r, 0, 0)),
            pl.BlockSpec((1, 8, a16.shape[2]), lambda r: (r, 0, 0)),
            pl.BlockSpec((1, 1, wide), lambda r: (r, 0, 0)),
        ],
        out_specs=pl.BlockSpec((1, nblk, wide), lambda r: (r, 0, 0)),
        out_shape=jax.ShapeDtypeStruct((nb, nblk, wide), F32),
        scratch_shapes=[pltpu.VMEM((nblk, 4 * S5_GPB * S5_STATE), F32),
                        pltpu.VMEM((nblk, 4 * S5_GPB * S5_STATE), F32)],
        compiler_params=_cparams(("arbitrary",)),
    )(u4, m, wz, vz, a16, dtile)
    return y4.reshape(nb, nblk, S5_GPB, S5_T, S5_CH).transpose(1, 3, 0, 2, 4).reshape(lt, GROUP_WIDTH)


def _even_out_kernel(x_ref, att_ref, y_ref, modx_ref, modc_ref, wglu_ref, bglu_ref, wout_ref, o_ref):
    is_ctx = pl.program_id(0) == 0
    gate_a = jnp.where(is_ctx, modc_ref[2:3, :], modx_ref[2:3, :])
    g = _gelu_tanh(y_ref[...])
    ssm = g * _sigmoid(_dot(g.astype(BF16), wglu_ref[...]) + bglu_ref[...])
    mix = (_dot(att_ref[...].astype(BF16), wout_ref[0:GROUP_WIDTH, :])
           + _dot(ssm.astype(BF16), wout_ref[GROUP_WIDTH:, :]))
    o_ref[...] = x_ref[...] + gate_a * mix


def _even_out(xs, att, y, modx, modc, w_glu, b_glu, w_out):
    lt = xs.shape[0]
    row = lambda w: pl.BlockSpec((ROW_TILE, w), lambda i: (i, 0))
    return pl.pallas_call(
        _even_out_kernel,
        grid=(lt // ROW_TILE,),
        in_specs=[row(D_MODEL), row(GROUP_WIDTH), row(GROUP_WIDTH), _const_spec(modx.shape),
                  _const_spec(modc.shape), _const_spec(w_glu.shape), _const_spec((1, GROUP_WIDTH)),
                  _const_spec(w_out.shape)],
        out_specs=row(D_MODEL),
        out_shape=jax.ShapeDtypeStruct((lt, D_MODEL), F32),
        compiler_params=_cparams(("arbitrary",)),
    )(xs, att, y, modx, modc, w_glu.astype(BF16), b_glu.reshape(1, -1), w_out.astype(BF16))


def _odd_out_kernel(x_ref, att_ref, yf_ref, yr_ref, z_ref, modx_ref, modc_ref, gn_ref, wout_ref, o_ref):
    is_ctx = pl.program_id(0) == 0
    gate_a = jnp.where(is_ctx, modc_ref[2:3, :], modx_ref[2:3, :])
    gy = (yf_ref[...] + yr_ref[...]) * _silu(z_ref[...])
    gw = SSD_INNER // SSD_GROUPS
    parts = [_rms(gy[:, j * gw:(j + 1) * gw]) for j in range(SSD_GROUPS)]
    ssm = jnp.concatenate(parts, axis=-1) * gn_ref[...]
    mix = (_dot(att_ref[...].astype(BF16), wout_ref[0:GROUP_WIDTH, :])
           + _dot(ssm.astype(BF16), wout_ref[GROUP_WIDTH:, :]))
    o_ref[...] = x_ref[...] + gate_a * mix


def _odd_out(xs, att, yf, yr, z, modx, modc, g_norm, w_out):
    lt = xs.shape[0]
    row = lambda w: pl.BlockSpec((ROW_TILE, w), lambda i: (i, 0))
    return pl.pallas_call(
        _odd_out_kernel,
        grid=(lt // ROW_TILE,),
        in_specs=[row(D_MODEL), row(GROUP_WIDTH), row(SSD_INNER), row(SSD_INNER), row(SSD_INNER),
                  _const_spec(modx.shape), _const_spec(modc.shape), _const_spec((1, SSD_INNER)),
                  _const_spec(w_out.shape)],
        out_specs=row(D_MODEL),
        out_shape=jax.ShapeDtypeStruct((lt, D_MODEL), F32),
        compiler_params=_cparams(("arbitrary",)),
    )(xs, att, yf, yr, z, modx, modc, g_norm.reshape(1, -1), w_out.astype(BF16))


def _ffn_kernel(x_ref, modx_ref, modc_ref, w1_ref, w3_ref, w2_ref, o_ref):
    tm = x_ref.shape[0]
    row0 = pl.program_id(0) * tm
    x = x_ref[...]
    sh = _mod_rows(modx_ref, modc_ref, 3, row0, tm)
    sc = _mod_rows(modx_ref, modc_ref, 4, row0, tm)
    gate = _mod_rows(modx_ref, modc_ref, 5, row0, tm)
    h = (_rms(x) * (1.0 + sc) + sh).astype(BF16)
    acc = jnp.zeros((tm, D_MODEL), F32)
    for c in range(D_FF // FFN_CHUNK):
        blk = slice(c * FFN_CHUNK, (c + 1) * FFN_CHUNK)
        a = _dot(h, w1_ref[:, blk])
        b = _dot(h, w3_ref[:, blk])
        acc = acc + _dot((_silu(a) * b).astype(BF16), w2_ref[blk, :])
    o_ref[...] = x + gate * acc


def _ffn(xs, modx, modc, w1, w3, w2):
    lt = xs.shape[0]
    tm = FFN_TILE if lt % FFN_TILE == 0 else ROW_TILE
    row = pl.BlockSpec((tm, D_MODEL), lambda i: (i, 0))
    single = lambda shape: pl.BlockSpec(shape, lambda i: (0, 0), pipeline_mode=pl.Buffered(1))
    return pl.pallas_call(
        _ffn_kernel,
        grid=(lt // tm,),
        in_specs=[row, _const_spec(modx.shape), _const_spec(modc.shape), single(w1.shape),
                  single(w3.shape), single(w2.shape)],
        out_specs=row,
        out_shape=jax.ShapeDtypeStruct((lt, D_MODEL), F32),
        compiler_params=_cparams(("arbitrary",)),
    )(xs, modx, modc, w1.astype(BF16), w3.astype(BF16), w2.astype(BF16))


ODD_COLS = 2048 + SSD_XBC + LANES


def _odd_in_kernel(x_ref, modx_ref, modc_ref, win_ref, gq_ref, gk_ref, e_ref, et_ref, cos_ref, sina_ref,
                   sinb_ref, q_ref, k_ref, v_ref, z_ref, xbc_ref, dt_ref):
    i = pl.program_id(0)
    is_ctx = i == 0
    sh = jnp.where(is_ctx, modc_ref[0:1, :], modx_ref[0:1, :])
    sc = jnp.where(is_ctx, modc_ref[1:2, :], modx_ref[1:2, :])
    h = (_rms(x_ref[...]) * (1.0 + sc) + sh).astype(BF16)
    proj = _dot(h, win_ref[...])
    ones = jnp.ones((ROW_TILE, DIFF_V), BF16)
    for hd in range(DIFF_HEADS):
        v_ref[:, 2 * hd * DIFF_V:(2 * hd + 1) * DIFF_V] = proj[:, 1024 + hd * DIFF_V:1024 + (hd + 1) * DIFF_V].astype(BF16)
        v_ref[:, (2 * hd + 1) * DIFF_V:(2 * hd + 2) * DIFF_V] = ones
    z_ref[...] = proj[:, 1536:2048]
    xbc_ref[...] = proj[:, 2048:2048 + SSD_XBC]
    dt_ref[...] = proj[:, 2048 + SSD_XBC:2048 + SSD_XBC + 2 * SSD_HEADS]
    nrep = GROUP_WIDTH // LANES
    cos = jnp.concatenate([cos_ref[...]] * nrep, axis=-1)
    sina = jnp.concatenate([sina_ref[...]] * nrep, axis=-1)
    sinb = jnp.concatenate([sinb_ref[...]] * nrep, axis=-1)
    half = DIFF_QK // 4
    qscale = DIFF_QK ** -0.5 * LOG2E

    def prep(t, g_ref, scale):
        ss = jnp.dot(t * t, e_ref[...], precision=HIGHEST, preferred_element_type=F32)
        r = lax.rsqrt(ss * (1.0 / DIFF_QK) + EPS)
        rb = jnp.dot(r, et_ref[...], precision=HIGHEST, preferred_element_type=F32)
        tn = t * rb * g_ref[...]
        up = pltpu.roll(tn, GROUP_WIDTH - half, axis=1)
        dn = pltpu.roll(tn, half, axis=1)
        return ((tn * cos + up * sina + dn * sinb) * scale).astype(BF16)

    q_ref[...] = prep(proj[:, 0:512], gq_ref, qscale)
    k_ref[...] = prep(proj[:, 512:1024], gk_ref, 1.0)


def _odd_in(xs, modx, modc, w_in, g_q, g_k, cos1, sina1, sinb1):
    lt = xs.shape[0]
    win = jnp.pad(w_in, ((0, 0), (0, ODD_COLS - w_in.shape[1]))).astype(BF16)
    nblk = GROUP_WIDTH // DIFF_QK
    e = jnp.zeros((GROUP_WIDTH, LANES), F32).at[np.arange(GROUP_WIDTH), np.arange(GROUP_WIDTH) // DIFF_QK].set(1.0)
    gq = jnp.tile(g_q, nblk).reshape(1, -1)
    gk = jnp.tile(g_k, nblk).reshape(1, -1)
    args = (xs, modx, modc, win, gq, gk, e, e.T, cos1, sina1, sinb1)
    row = lambda w: pl.BlockSpec((ROW_TILE, w), lambda i: (i, 0))
    in_specs = [row(D_MODEL)] + [_const_spec(a.shape) for a in args[1:8]] + [row(LANES)] * 3
    ndt = 2 * SSD_HEADS
    return pl.pallas_call(
        _odd_in_kernel,
        grid=(lt // ROW_TILE,),
        in_specs=in_specs,
        out_specs=[row(512), row(512), row(1024), row(512), row(SSD_XBC), row(ndt)],
        out_shape=[jax.ShapeDtypeStruct((lt, 512), BF16), jax.ShapeDtypeStruct((lt, 512), BF16),
                   jax.ShapeDtypeStruct((lt, 1024), BF16), jax.ShapeDtypeStruct((lt, 512), F32),
                   jax.ShapeDtypeStruct((lt, SSD_XBC), F32), jax.ShapeDtypeStruct((lt, ndt), F32)],
        compiler_params=_cparams(("arbitrary",)),
    )(*args)


def _conv_kernel(prev_ref, cur_ref, next_ref, w_ref, b_ref, o_ref, ext_scr, *, ntiles):
    i = pl.program_id(0)
    pad = SSD_CONV // 2
    has_prev = i >= 2
    has_next = jnp.logical_and(i >= 1, i < ntiles - 1)
    ext_scr[0:8, :] = jnp.where(has_prev, prev_ref[...], 0.0)
    ext_scr[8:8 + ROW_TILE, :] = cur_ref[...]
    ext_scr[8 + ROW_TILE:16 + ROW_TILE, :] = jnp.where(has_next, next_ref[...], 0.0)
    acc = jnp.zeros((ROW_TILE, SSD_XBC), F32) + b_ref[...]
    for k in range(SSD_CONV):
        acc = acc + ext_scr[pl.ds(8 - pad + k, ROW_TILE), :] * w_ref[k:k + 1, :]
    o_ref[...] = _silu(acc)


def _ssd_conv(xbc, conv_w, conv_b):
    lt = xbc.shape[0]
    ntiles = lt // ROW_TILE
    per = ROW_TILE // 8
    last8 = lt // 8 - 1
    w = jnp.pad(conv_w, ((0, 8 - SSD_CONV), (0, 0)))
    return pl.pallas_call(
        functools.partial(_conv_kernel, ntiles=ntiles),
        grid=(ntiles,),
        in_specs=[
            pl.BlockSpec((8, SSD_XBC), lambda i: (jnp.maximum(i * per - 1, 0), 0)),
            pl.BlockSpec((ROW_TILE, SSD_XBC), lambda i: (i, 0)),
            pl.BlockSpec((8, SSD_XBC), lambda i: (jnp.minimum((i + 1) * per, last8), 0)),
            _const_spec((8, SSD_XBC)),
            _const_spec((1, SSD_XBC)),
        ],
        out_specs=pl.BlockSpec((ROW_TILE, SSD_XBC), lambda i: (i, 0)),
        out_shape=jax.ShapeDtypeStruct((lt, SSD_XBC), F32),
        scratch_shapes=[pltpu.VMEM((ROW_TILE + 16, SSD_XBC), F32)],
        compiler_params=_cparams(("arbitrary",)),
    )(xbc, xbc, xbc, w, conv_b.reshape(1, -1))


def _ssd_direction(xc, dtr, dtr_t, bias, bias_t, alog, alog_t, dskip, state_ref, rev):
    q = SSD_CHUNK
    hp = SSD_HEAD_DIM
    dt = _softplus(dtr + bias)
    dt_t = _softplus(dtr_t + bias_t)
    a = -jnp.exp(alog)
    a_t = -jnp.exp(alog_t)
    da = dt * a
    da_t = dt_t * a_t
    r_idx = lax.broadcasted_iota(jnp.int32, (q, q), 0)
    c_idx = lax.broadcasted_iota(jnp.int32, (q, q), 1)
    if rev:
        tri = (c_idx >= r_idx).astype(F32)
        keep = c_idx >= r_idx
    else:
        tri = (c_idx <= r_idx).astype(F32)
        keep = c_idx <= r_idx
    cs = jnp.dot(tri, da, precision=HIGHEST, preferred_element_type=F32)
    cs_t = lax.dot_general(da_t, tri, (((1,), (1,)), ((), ())), precision=HIGHEST,
                           preferred_element_type=F32)
    total = cs[0:1, :] if rev else cs[q - 1:q, :]
    x = xc[:, 0:SSD_INNER]
    gw = SSD_GROUPS * SSD_STATE
    bm = xc[:, SSD_INNER:SSD_INNER + gw].astype(BF16)
    cm = xc[:, SSD_INNER + gw:SSD_INNER + 2 * gw].astype(BF16)
    hpg = SSD_HEADS // SSD_GROUPS
    outs = []
    for g in range(SSD_GROUPS):
        bg = bm[:, g * SSD_STATE:(g + 1) * SSD_STATE]
        cg = cm[:, g * SSD_STATE:(g + 1) * SSD_STATE]
        scores = lax.dot_general(cg, bg, (((1,), (1,)), ((), ())), preferred_element_type=F32)
        cols = slice(g * hpg * hp, (g + 1) * hpg * hp)
        s_in = state_ref[:, cols]
        y_off = _dot(cg, s_in.astype(BF16))
        xdt_parts, xdec_parts, ydiag_parts, scale_parts = [], [], [], []
        for j in range(hpg):
            hd = g * hpg + j
            xh = x[:, hd * hp:(hd + 1) * hp]
            xdt = xh * dt[:, hd:hd + 1]
            lmat = jnp.exp(jnp.where(keep, cs[:, hd:hd + 1] - cs_t[hd:hd + 1, :], -jnp.inf))
            ydiag_parts.append(_dot((scores * lmat).astype(BF16), xdt.astype(BF16)))
            xdec_parts.append(xdt * jnp.exp(total[:, hd:hd + 1] - cs[:, hd:hd + 1]))
            scale_parts.append(jnp.broadcast_to(jnp.exp(cs[:, hd:hd + 1]), (q, hp)))
            xdt_parts.append(xh * dskip[:, hd:hd + 1] if dskip is not None else None)
        y = jnp.concatenate(ydiag_parts, axis=-1) + y_off * jnp.concatenate(scale_parts, axis=-1)
        if dskip is not None:
            y = y + jnp.concatenate(xdt_parts, axis=-1)
        outs.append(y)
        xdec = jnp.concatenate(xdec_parts, axis=-1).astype(BF16)
        new = lax.dot_general(bg, xdec, (((0,), (0,)), ((), ())), preferred_element_type=F32)
        chunk_decay = jnp.concatenate(
            [jnp.broadcast_to(jnp.exp(total[:, g * hpg + j:g * hpg + j + 1]), (1, hp)) for j in range(hpg)],
            axis=-1)
        state_ref[:, cols] = s_in * chunk_decay + new
    return jnp.concatenate(outs, axis=-1)


def _ssd_kernel(xf_ref, dtf_ref, dtft_ref, xr_ref, dtr_ref, dtrt_ref, bias_ref, biast_ref, alog_ref,
                alogt_ref, dskip_ref, yf_ref, yr_ref, sf_scr, sr_scr):
    @pl.when(pl.program_id(0) == 0)
    def _():
        sf_scr[...] = jnp.zeros_like(sf_scr)
        sr_scr[...] = jnp.zeros_like(sr_scr)

    h = SSD_HEADS
    yf_ref[...] = _ssd_direction(xf_ref[...], dtf_ref[:, 0:h], dtft_ref[0, 0:h, :], bias_ref[0:1, :],
                                 biast_ref[:, 0:1], alog_ref[0:1, :], alogt_ref[:, 0:1], dskip_ref[...],
                                 sf_scr, False)
    yr_ref[...] = _ssd_direction(xr_ref[...], dtr_ref[:, h:2 * h], dtrt_ref[0, h:2 * h, :], bias_ref[1:2, :],
                                 biast_ref[:, 1:2], alog_ref[1:2, :], alogt_ref[:, 1:2], None, sr_scr, True)


def _ssd_scan(xc, dtr, dt_bias, a_log, d_skip):
    lt = xc.shape[0]
    nc = lt // SSD_CHUNK
    nctx = CTX_LEN // SSD_CHUNK
    ndt = 2 * SSD_HEADS
    dtr_t = dtr.reshape(nc, SSD_CHUNK, ndt).transpose(0, 2, 1)

    def rev_chunk(k):
        return jnp.where(k < nctx, nctx - 1 - k, nc - 1 + nctx - k)

    fwd = lambda w: pl.BlockSpec((SSD_CHUNK, w), lambda k: (k, 0))
    bwd = lambda w: pl.BlockSpec((SSD_CHUNK, w), lambda k: (rev_chunk(k), 0))
    small = (dt_bias, dt_bias.T, a_log, a_log.T, d_skip.reshape(1, -1))
    return pl.pallas_call(
        _ssd_kernel,
        grid=(nc,),
        in_specs=[fwd(SSD_XBC), fwd(ndt), pl.BlockSpec((1, ndt, SSD_CHUNK), lambda k: (k, 0, 0)),
                  bwd(SSD_XBC), bwd(ndt), pl.BlockSpec((1, ndt, SSD_CHUNK), lambda k: (rev_chunk(k), 0, 0))]
                 + [_const_spec(a.shape) for a in small],
        out_specs=[fwd(SSD_INNER), bwd(SSD_INNER)],
        out_shape=[jax.ShapeDtypeStruct((lt, SSD_INNER), F32)] * 2,
        scratch_shapes=[pltpu.VMEM((SSD_STATE, SSD_INNER), F32)] * 2,
        compiler_params=_cparams(("arbitrary",)),
    )(xc, dtr, dtr_t, xc, dtr, dtr_t, *small)


def _rope_tables(seq):
    t = np.arange(seq)
    row, col = (t // GRID_W).astype(np.float32), (t % GRID_W).astype(np.float32)

    def angles(n):
        inv = ROPE_THETA ** (-jnp.arange(n, dtype=F32) / n)
        return jnp.asarray(row)[:, None] * inv[None, :], jnp.asarray(col)[:, None] * inv[None, :]

    def with_ctx(tab, fill):
        return jnp.concatenate([jnp.full((CTX_LEN, tab.shape[1]), fill, F32), tab], axis=0)

    ar, ac = angles(MLA_ROPE // 4)
    ones = jnp.ones((seq, MLA_NOPE), F32)
    zeros = jnp.zeros((seq, MLA_NOPE), F32)
    tail1 = jnp.ones((seq, LANES - MLA_QK), F32)
    tail0 = jnp.zeros((seq, LANES - MLA_QK), F32)
    cos0 = jnp.concatenate([ones, jnp.cos(ar), jnp.cos(ar), jnp.cos(ac), jnp.cos(ac), tail1], axis=1)
    sin0 = jnp.concatenate([zeros, -jnp.sin(ar), jnp.sin(ar), -jnp.sin(ac), jnp.sin(ac), tail0], axis=1)
    br, bc = angles(DIFF_QK // 4)
    z16 = jnp.zeros_like(br)
    cos1 = jnp.concatenate([jnp.cos(br), jnp.cos(br), jnp.cos(bc), jnp.cos(bc)] * 2, axis=1)
    sina1 = jnp.concatenate([-jnp.sin(br), z16, -jnp.sin(bc), z16] * 2, axis=1)
    sinb1 = jnp.concatenate([z16, jnp.sin(br), z16, jnp.sin(bc)] * 2, axis=1)
    return (with_ctx(cos0, 1.0), with_ctx(sin0, 0.0), with_ctx(cos1, 1.0), with_ctx(sina1, 0.0),
            with_ctx(sinb1, 0.0))


def kernel(x, c, ctx, c_ctx, ada_w, ada_b, ffn_w1, ffn_w3, ffn_w2, e_w_in, e_w_out, mla_g_qa, mla_w_qb, mla_g_kva, mla_w_kvb, mla_g_qn, mla_g_kn, s5_lam_re, s5_lam_im, s5_log_dt, s5_b_re, s5_b_im, s5_c_re, s5_c_im, s5_d, s5_w_glu, s5_b_glu, o_w_in, o_w_out, diff_g_q, diff_g_k, diff_lq1, diff_lk1, diff_lq2, diff_lk2, diff_g_o, ssd_conv_w, ssd_conv_b, ssd_dt_bias, ssd_a_log, ssd_d, ssd_g):
    depth = ada_w.shape[0]
    seq = x.shape[1]
    assert x.shape[0] == 1 and ctx.shape[1] == CTX_LEN and seq % ROW_TILE == 0
    xs = jnp.concatenate([ctx[0], x[0]], axis=0)
    mods = _mod_vectors(c, c_ctx, ada_w, ada_b)
    cos0, sin0, cos1, sina1, sinb1 = _rope_tables(seq)
    for i in range(depth):
        j = i // 2
        modx, modc = mods[i, 0], mods[i, 1]
        if i % 2 == 0:
            q, k, v, u = _even_in(xs, modx, modc, e_w_in[j], mla_g_qa[j], mla_w_qb[j], mla_g_kva[j],
                                  mla_w_kvb[j], mla_g_qn[j], mla_g_kn[j], cos0, sin0)
            att = _mla_attention(q, k, v)
            mats = _s5_matrices(s5_lam_re[j], s5_lam_im[j], s5_log_dt[j], s5_b_re[j], s5_b_im[j],
                                s5_c_re[j], s5_c_im[j], s5_d[j])
            y = _s5_mixer(u, mats)
            xs = _even_out(xs, att, y, modx, modc, s5_w_glu[j], s5_b_glu[j], e_w_out[j])
        else:
            q, k, v, z, xbc, dtr = _odd_in(xs, modx, modc, o_w_in[j], diff_g_q[j], diff_g_k[j],
                                           cos1, sina1, sinb1)
            lam_init = 0.8 - 0.6 * math.exp(-0.3 * i)
            lam = (jnp.exp(jnp.sum(diff_lq1[j] * diff_lk1[j])) - jnp.exp(jnp.sum(diff_lq2[j] * diff_lk2[j]))
                   + lam_init)
            att = _diff_attention(q, k, v, jnp.full((1, LANES), lam, F32), diff_g_o[j].reshape(1, -1),
                                  1.0 - lam_init)
            xc = _ssd_conv(xbc, ssd_conv_w[j], ssd_conv_b[j])
            yf, yr = _ssd_scan(xc, dtr, ssd_dt_bias[j], ssd_a_log[j], ssd_d[j])
            xs = _odd_out(xs, att, yf, yr, z, modx, modc, ssd_g[j], o_w_out[j])
        xs = _ffn(xs, modx, modc, ffn_w1[i], ffn_w3[i], ffn_w2[i])
    return xs[CTX_LEN:][None]
```

```python
import functools
import math

import numpy as np
import jax
import jax.numpy as jnp
from jax import lax
from jax.experimental import pallas as pl
from jax.experimental.pallas import tpu as pltpu

F32 = jnp.float32
BF16 = jnp.bfloat16
HIGHEST = lax.Precision.HIGHEST

D_MODEL = 1024
CTX_LEN = 256
GRID_W = 64
GROUP_WIDTH = 512
D_FF = 2816
EPS = 1e-6
ROPE_THETA = 10000.0
LOG2E = math.log2(math.e)

MLA_HEADS, MLA_NOPE, MLA_ROPE, MLA_V = 8, 64, 32, 64
MLA_QK = MLA_NOPE + MLA_ROPE
MLA_Q_RANK, MLA_KV_RANK = 384, 256
S5_CH, S5_GROUPS, S5_STATE = 16, 32, 64
S5_T = 16
S5_GPB = 8
DIFF_HEADS, DIFF_QK, DIFF_V = 4, 64, 128
SSD_HEADS, SSD_HEAD_DIM, SSD_GROUPS, SSD_STATE, SSD_CONV, SSD_CHUNK = 8, 64, 2, 128, 5, 128
SSD_INNER = 512
SSD_XBC = SSD_INNER + 2 * SSD_GROUPS * SSD_STATE

LANES = 128
ROW_TILE = 256
FFN_TILE = 640
FFN_CHUNK = 256
KV_UNROLL = 4
VMEM_LIMIT = 56 * 1024 * 1024


def _cparams(sem):
    return pltpu.CompilerParams(dimension_semantics=sem, vmem_limit_bytes=VMEM_LIMIT)


def _dot(a, b):
    return jnp.dot(a, b, preferred_element_type=F32)


def _split_dot(x, w, parts):
    acc = None
    for _ in range(parts):
        piece = x.astype(BF16)
        term = _dot(piece, w)
        acc = term if acc is None else acc + term
        x = x - piece.astype(F32)
    return acc


def _rms(x):
    return x * lax.rsqrt(jnp.mean(x * x, axis=-1, keepdims=True) + EPS)


def _sigmoid(x):
    return 1.0 / (1.0 + jnp.exp(-x))


def _silu(x):
    return x * _sigmoid(x)


def _gelu_tanh(x):
    return 0.5 * x * (1.0 + jnp.tanh(math.sqrt(2.0 / math.pi) * (x + 0.044715 * (x * x * x))))


def _softplus(x):
    return jnp.maximum(x, 0.0) + jnp.log(1.0 + jnp.exp(-jnp.abs(x)))


def _const_spec(shape):
    nd = len(shape)
    return pl.BlockSpec(shape, lambda *_: (0,) * nd)


def _mod_kernel(c_ref, w_ref, b_ref, o_ref):
    s = _silu(c_ref[...]).astype(BF16)
    o_ref[0] = _dot(s, w_ref[0].astype(BF16)) + b_ref[0]


def _mod_vectors(c, c_ctx, ada_w, ada_b):
    depth = ada_w.shape[0]
    cc = jnp.zeros((8, D_MODEL), F32).at[0].set(c[0]).at[1].set(c_ctx)
    nblk = 6
    out = pl.pallas_call(
        _mod_kernel,
        grid=(depth, nblk),
        in_specs=[
            pl.BlockSpec((8, D_MODEL), lambda i, j: (0, 0)),
            pl.BlockSpec((1, D_MODEL, D_MODEL), lambda i, j: (i, 0, j)),
            pl.BlockSpec((1, 1, D_MODEL), lambda i, j: (i, 0, j)),
        ],
        out_specs=pl.BlockSpec((1, 8, D_MODEL), lambda i, j: (i, 0, j)),
        out_shape=jax.ShapeDtypeStruct((depth, 8, 6 * D_MODEL), F32),
        compiler_params=_cparams(("arbitrary", "arbitrary")),
    )(cc, ada_w, ada_b.reshape(depth, 1, 6 * D_MODEL))
    mods = out[:, :2].reshape(depth, 2, 6, D_MODEL)
    return jnp.pad(mods, ((0, 0), (0, 0), (0, 2), (0, 0)))


def _mod_rows(modx_ref, modc_ref, k, row0, nrows):
    rows = row0 + lax.broadcasted_iota(jnp.int32, (nrows, 1), 0)
    return jnp.where(rows < CTX_LEN, modc_ref[k:k + 1, :], modx_ref[k:k + 1, :])


def _rope_partner(n_half):
    return np.array([i + n_half if i < n_half else i - n_half for i in range(2 * n_half)])


def _even_in_kernel(x_ref, modx_ref, modc_ref, win_ref, gqa_ref, wq_ref, wqs_ref, gkva_ref, wk_ref,
                    wv_ref, gq_ref, gqs_ref, gk_ref, gks_ref, rrow_ref, rcol_ref,
                    q_ref, k_ref, v_ref, u_ref):
    i = pl.program_id(0)
    is_ctx = i == 0
    sh = jnp.where(is_ctx, modc_ref[0:1, :], modx_ref[0:1, :])
    sc = jnp.where(is_ctx, modc_ref[1:2, :], modx_ref[1:2, :])
    h = (_rms(x_ref[...]) * (1.0 + sc) + sh).astype(BF16)
    proj = _dot(h, win_ref[...])
    cq = proj[:, 0:384]
    ckv = proj[:, 384:640]
    u_ref[...] = proj[:, 640:1152]
    krb = proj[:, 1152:1280]
    krs = proj[:, 1280:1408]
    cqn = (_rms(cq) * gqa_ref[...]).astype(BF16)
    qf = _dot(cqn, wq_ref[...])
    qs = _dot(cqn, wqs_ref[...])
    ckvn = (_rms(ckv) * gkva_ref[...]).astype(BF16)
    kf = _dot(ckvn, wk_ref[...])
    vf = _dot(ckvn, wv_ref[...])
    lane = lax.broadcasted_iota(jnp.int32, (1, LANES), 1)
    for pr in range(MLA_HEADS // 2):
        vp = vf[:, pr * LANES:(pr + 1) * LANES]
        v_ref[:, (2 * pr) * LANES:(2 * pr + 1) * LANES] = jnp.where(lane < MLA_V, vp, 1.0).astype(BF16)
        v_ref[:, (2 * pr + 1) * LANES:(2 * pr + 2) * LANES] = jnp.where(lane < MLA_V, 1.0, vp).astype(BF16)
    cos = _rope_tile(rrow_ref, rcol_ref, 0, is_ctx, True)
    sin = _rope_tile(rrow_ref, rcol_ref, 1, is_ctx, False)
    qscale = MLA_QK ** -0.5 * LOG2E
    inv_n = 1.0 / MLA_QK
    for hd in range(MLA_HEADS):
        blk = slice(hd * LANES, (hd + 1) * LANES)
        qh = qf[:, blk]
        rq = lax.rsqrt(jnp.sum(qh * qh, axis=-1, keepdims=True) * inv_n + EPS) * qscale
        qo = (qh * (gq_ref[...] * cos) + qs[:, blk] * (gqs_ref[...] * sin)) * rq
        q_ref[:, blk] = qo.astype(BF16)
        kh = kf[:, blk] + krb
        rk = lax.rsqrt(jnp.sum(kh * kh, axis=-1, keepdims=True) * inv_n + EPS)
        ko = (kh * (gk_ref[...] * cos) + krs * (gks_ref[...] * sin)) * rk
        k_ref[:, blk] = ko.astype(BF16)


def _even_in(xs, modx, modc, w_in, g_qa, w_qb, g_kva, w_kvb, g_qn, g_kn, rope_row, rope_col):
    lt = xs.shape[0]
    perm = _rope_partner(MLA_ROPE // 4)
    perm = np.concatenate([perm, perm + MLA_ROPE // 2])
    cq_w, ckv_w = w_in[:, :384], w_in[:, 384:640]
    kr_w, u_w = w_in[:, 640:672], w_in[:, 672:]
    zblk = jnp.zeros((D_MODEL, LANES), F32)
    krblk = zblk.at[:, MLA_NOPE:MLA_QK].set(kr_w)
    krsblk = zblk.at[:, MLA_NOPE:MLA_QK].set(kr_w[:, perm])
    win = jnp.concatenate([cq_w, ckv_w, u_w, krblk, krsblk], axis=1).astype(BF16)
    wq3 = w_qb.reshape(MLA_Q_RANK, MLA_HEADS, MLA_QK)
    wq = jnp.zeros((MLA_Q_RANK, MLA_HEADS, LANES), F32).at[:, :, :MLA_QK].set(wq3)
    wqs = jnp.zeros((MLA_Q_RANK, MLA_HEADS, LANES), F32).at[:, :, MLA_NOPE:MLA_QK].set(
        wq3[:, :, MLA_NOPE:][:, :, perm])
    wkv3 = w_kvb.reshape(MLA_KV_RANK, MLA_HEADS, MLA_NOPE + MLA_V)
    wk = jnp.zeros((MLA_KV_RANK, MLA_HEADS, LANES), F32).at[:, :, :MLA_NOPE].set(wkv3[:, :, :MLA_NOPE])
    wv = wkv3[:, :, MLA_NOPE:].reshape(MLA_KV_RANK, MLA_HEADS * MLA_V)

    def pad_gain(g):
        gp = jnp.zeros((1, LANES), F32).at[0, :MLA_QK].set(g)
        gs = jnp.zeros((1, LANES), F32).at[0, MLA_NOPE:MLA_QK].set(g[MLA_NOPE:][perm])
        return gp, gs

    gq, gqs = pad_gain(g_qn)
    gk, gks = pad_gain(g_kn)
    hw = MLA_HEADS * LANES
    args = (xs, modx, modc, win, g_qa.reshape(1, -1), wq.reshape(MLA_Q_RANK, hw).astype(BF16),
            wqs.reshape(MLA_Q_RANK, hw).astype(BF16), g_kva.reshape(1, -1),
            wk.reshape(MLA_KV_RANK, hw).astype(BF16), wv.astype(BF16), gq, gqs, gk, gks, rope_row, rope_col)
    row = lambda w: pl.BlockSpec((ROW_TILE, w), lambda i: (i, 0))
    in_specs = ([row(D_MODEL)] + [_const_spec(a.shape) for a in args[1:14]]
                + [pl.BlockSpec((1,) + rope_row.shape[1:], lambda i: (i, 0, 0)), _const_spec(rope_col.shape)])
    return pl.pallas_call(
        _even_in_kernel,
        grid=(lt // ROW_TILE,),
        in_specs=in_specs,
        out_specs=[row(hw), row(hw), row(hw), row(GROUP_WIDTH)],
        out_shape=[jax.ShapeDtypeStruct((lt, hw), BF16), jax.ShapeDtypeStruct((lt, hw), BF16),
                   jax.ShapeDtypeStruct((lt, hw), BF16),
                   jax.ShapeDtypeStruct((lt, GROUP_WIDTH), F32)],
        compiler_params=_cparams(("arbitrary",)),
    )(*args)


def _kv_chunk(lt):
    for tk in (1280, 640, 256):
        if lt % tk == 0:
            return tk
    raise ValueError(lt)


def _scores(q, k_ref, start, size):
    kc = k_ref[pl.ds(start, size), :]
    return lax.dot_general(q, kc, (((1,), (1,)), ((), ())), preferred_element_type=F32)


def _softmax_update(s_ref, v_ref, start, size, carry):
    m, acc = carry
    mn = jnp.maximum(m, jnp.max(s_ref[...], axis=-1, keepdims=True))
    alpha = jnp.exp2(m - mn)
    p = jnp.exp2(s_ref[...] - mn)
    acc = alpha * acc + _dot(p.astype(BF16), v_ref[pl.ds(start, size), :])
    return mn, acc


def _attend(streams, s_scr, is_ctx, lt, tk):
    tq = streams[0][0].shape[0]
    n = lt // tk
    init = tuple((jnp.full((tq, 1), -jnp.inf, F32), jnp.zeros((tq, v_ref.shape[-1]), F32))
                 for _, _, v_ref in streams)

    def consume(chunk, slot, carries, prefetch):
        start = chunk * tk
        if not isinstance(start, int):
            start = pl.multiple_of(start, tk)
        out = []
        for t, ((q, k_ref, v_ref), c) in enumerate(zip(streams, carries)):
            if prefetch:
                s_scr[2 * t + 1 - slot] = _scores(q, k_ref, start + tk, tk)
            out.append(_softmax_update(s_scr.at[2 * t + slot], v_ref, start, tk, c))
        return tuple(out)

    def group(i, carries):
        for u in range(KV_UNROLL):
            carries = consume(KV_UNROLL * i + u, u % 2, carries, True)
        return carries

    def ctx_branch():
        out = []
        for t, ((q, k_ref, v_ref), c) in enumerate(zip(streams, init)):
            s_scr[2 * t, :, 0:CTX_LEN] = _scores(q, k_ref, 0, CTX_LEN)
            out.append(_softmax_update(s_scr.at[2 * t, :, 0:CTX_LEN], v_ref, 0, CTX_LEN, c))
        return tuple(out)

    def full_branch():
        for t, (q, k_ref, _) in enumerate(streams):
            s_scr[2 * t] = _scores(q, k_ref, 0, tk)
        ngroups = (n - 1) // KV_UNROLL
        carries = lax.fori_loop(0, ngroups, group, init) if ngroups else init
        for chunk in range(KV_UNROLL * ngroups, n):
            carries = consume(chunk, chunk % 2, carries, chunk < n - 1)
        return carries

    out = lax.cond(is_ctx, ctx_branch, full_branch)
    return [acc for (_, acc) in out]


def _mla_attn_kernel(q_ref, k_ref, v_ref, o_ref, s_scr, *, lt, tk):
    is_ctx = pl.program_id(1) == 0
    lane = lax.broadcasted_iota(jnp.int32, (1, LANES), 1)
    blks = [slice(hh * LANES, (hh + 1) * LANES) for hh in range(2)]
    acc_a, acc_b = _attend([(q_ref[:, b], k_ref.at[:, b], v_ref.at[:, b]) for b in blks], s_scr, is_ctx, lt, tk)
    o_ref[...] = jnp.where(lane < MLA_V, acc_a / acc_a[:, MLA_V:MLA_V + 1], acc_b / acc_b[:, 0:1])


def _mla_attention(q, k, v):
    lt = q.shape[0]
    tk = _kv_chunk(lt)
    npair = MLA_HEADS // 2
    return pl.pallas_call(
        functools.partial(_mla_attn_kernel, lt=lt, tk=tk),
        grid=(npair, lt // ROW_TILE),
        in_specs=[
            pl.BlockSpec((ROW_TILE, 2 * LANES), lambda p, i: (i, p)),
            pl.BlockSpec((lt, 2 * LANES), lambda p, i: (0, p), pipeline_mode=pl.Buffered(1)),
            pl.BlockSpec((lt, 2 * LANES), lambda p, i: (0, p), pipeline_mode=pl.Buffered(1)),
        ],
        out_specs=pl.BlockSpec((ROW_TILE, LANES), lambda p, i: (i, p)),
        out_shape=jax.ShapeDtypeStruct((lt, GROUP_WIDTH), F32),
        scratch_shapes=[pltpu.VMEM((4, ROW_TILE, tk), F32)],
        compiler_params=_cparams(("arbitrary", "arbitrary")),
    )(q, k, v)


def _diff_attn_kernel(q_ref, k_ref, v_ref, lam_ref, go_ref, o_ref, s_scr, *, lt, tk, out_scale):
    is_ctx = pl.program_id(1) == 0
    lane = lax.broadcasted_iota(jnp.int32, (1, LANES), 1)
    q = q_ref[...]
    zero = jnp.zeros_like(q)
    q1 = jnp.where(lane < DIFF_QK, q, zero)
    q2 = jnp.where(lane < DIFF_QK, zero, q)
    a1, a2 = _attend([(q1, k_ref, v_ref), (q2, k_ref, v_ref)], s_scr, is_ctx, lt, tk)
    o = (a1[:, 0:DIFF_V] / a1[:, DIFF_V:DIFF_V + 1]
         - lam_ref[...] * (a2[:, 0:DIFF_V] / a2[:, DIFF_V:DIFF_V + 1]))
    o_ref[...] = _rms(o) * go_ref[...] * out_scale


def _diff_attention(q, k, v, lam, g_o, out_scale):
    lt = q.shape[0]
    tk = _kv_chunk(lt)
    return pl.pallas_call(
        functools.partial(_diff_attn_kernel, lt=lt, tk=tk, out_scale=out_scale),
        grid=(DIFF_HEADS, lt // ROW_TILE),
        in_specs=[
            pl.BlockSpec((ROW_TILE, LANES), lambda h, i: (i, h)),
            pl.BlockSpec((lt, LANES), lambda h, i: (0, h), pipeline_mode=pl.Buffered(1)),
            pl.BlockSpec((lt, 2 * LANES), lambda h, i: (0, h), pipeline_mode=pl.Buffered(1)),
            _const_spec((1, LANES)),
            _const_spec((1, LANES)),
        ],
        out_specs=pl.BlockSpec((ROW_TILE, LANES), lambda h, i: (i, h)),
        out_shape=jax.ShapeDtypeStruct((lt, GROUP_WIDTH), F32),
        scratch_shapes=[pltpu.VMEM((4, ROW_TILE, tk), F32)],
        compiler_params=_cparams(("arbitrary", "arbitrary")),
    )(q, k, v, lam, g_o)


def _s5_matrices(lam_re, lam_im, log_dt, b_re, b_im, c_re, c_im, d_skip):
    t = S5_T
    g, n, ch = S5_GROUPS, S5_STATE, S5_CH
    dt = jnp.exp(log_dt)[:, :, None, None]
    tau = jnp.arange(t + 1, dtype=F32)
    mag = jnp.exp(lam_re[..., None] * dt * tau)
    ang = lam_im[..., None] * dt * tau
    p_re, p_im = mag * jnp.cos(ang), mag * jnp.sin(ang)
    a_re, a_im = p_re[..., 1], p_im[..., 1]
    den = lam_re * lam_re + lam_im * lam_im
    k_re = ((a_re - 1.0) * lam_re + a_im * lam_im) / den
    k_im = (a_im * lam_re - (a_re - 1.0) * lam_im) / den
    bb_re = k_re[..., None] * b_re - k_im[..., None] * b_im
    bb_im = k_re[..., None] * b_im + k_im[..., None] * b_re
    ca_re = c_re[..., None] * p_re[:, :, None] - c_im[..., None] * p_im[:, :, None]
    ca_im = c_re[..., None] * p_im[:, :, None] + c_im[..., None] * p_re[:, :, None]
    kk = (jnp.einsum('dgcnt,dgnk->dgtck', ca_re, bb_re, precision=HIGHEST)
          - jnp.einsum('dgcnt,dgnk->dgtck', ca_im, bb_im, precision=HIGHEST))
    ii = np.arange(t)
    lag = ii[None, :] - ii[:, None]
    mf = jnp.where((lag >= 0)[None, :, :, None, None], kk[0][:, np.clip(lag, 0, t)], 0.0)
    mr = jnp.where((lag <= 0)[None, :, :, None, None], kk[1][:, np.clip(-lag, 0, t)], 0.0)
    m = (mf + mr).transpose(0, 1, 4, 2, 3).reshape(g, t * ch, t * ch)

    def w_of(d, expo):
        pr = p_re[d][:, :, expo].transpose(0, 2, 1)[:, :, None, :]
        pi = p_im[d][:, :, expo].transpose(0, 2, 1)[:, :, None, :]
        br = bb_re[d].transpose(0, 2, 1)[:, None]
        bi = bb_im[d].transpose(0, 2, 1)[:, None]
        return pr * br - pi * bi, pr * bi + pi * br

    wf_re, wf_im = w_of(0, t - 1 - ii)
    wr_re, wr_im = w_of(1, ii)
    w4 = jnp.stack([wf_re, wf_im, wr_re, wr_im], axis=3)

    def v_of(d, expo):
        gr = ca_re[d][:, :, :, expo].transpose(0, 2, 3, 1)
        gi = ca_im[d][:, :, :, expo].transpose(0, 2, 3, 1)
        return gr, -gi

    vf_re, vf_im = v_of(0, ii + 1)
    vr_re, vr_im = v_of(1, t - ii)
    v4 = jnp.stack([vf_re, vf_im, vr_re, vr_im], axis=0)
    nb, gp = g // S5_GPB, S5_GPB
    eye = np.eye(gp, dtype=bool)

    def spread(x, g_axis, h_axis):
        x = jnp.expand_dims(x.astype(BF16), h_axis)
        shape = [1] * x.ndim
        shape[g_axis], shape[h_axis] = gp, gp
        return jnp.where(eye.reshape(shape), x, jnp.zeros((), BF16))

    m7 = spread(m.reshape(nb, gp, t, ch, t, ch).transpose(0, 2, 1, 3, 4, 5), 2, 5)
    w7 = spread(w4.reshape(nb, gp, t, ch, 4, n).transpose(0, 2, 1, 3, 4, 5), 2, 5)
    v7 = spread(v4.reshape(4, nb, gp, n, t, ch).transpose(1, 0, 2, 3, 4, 5), 2, 5)
    wide = gp * t * ch
    mv = jnp.concatenate([m7.reshape(nb, wide, wide), v7.reshape(nb, 4 * gp * n, wide)], axis=1)
    wz = w7.reshape(nb, wide, 4 * gp * n)
    a16 = jnp.stack([p_re[0][..., t], p_im[0][..., t], p_re[1][..., t], p_im[1][..., t]], axis=0)
    a16 = a16.reshape(4, nb, gp * n).transpose(1, 0, 2)
    a16 = jnp.pad(a16, ((0, 0), (0, 4), (0, 0)))
    return mv, wz, a16, d_skip.reshape(nb, 1, gp * ch)


def _s5_gather_kernel(u_ref, wz_ref, ub_ref, z_ref):
    nrow = ub_ref.shape[1]
    for j in range(S5_T):
        ub_ref[0, :, j * LANES:(j + 1) * LANES] = u_ref[pl.ds(j, nrow, stride=S5_T), :].astype(BF16)
    z_ref[0] = _dot(ub_ref[0], wz_ref[0])


def _s5_scan_kernel(z_ref, a_ref, s_ref, *, nblk, nctx):
    w = S5_GPB * S5_STATE
    a = a_ref[0]
    afr, afi, arr, ari = a[0:1], a[1:2], a[2:3], a[3:4]

    def step(cf, cr, carry):
        fr, fi, rr, ri = carry
        s_ref[0, pl.ds(cf, 1), 0:w] = fr
        s_ref[0, pl.ds(cf, 1), w:2 * w] = fi
        s_ref[0, pl.ds(cr, 1), 2 * w:3 * w] = rr
        s_ref[0, pl.ds(cr, 1), 3 * w:4 * w] = ri
        zfr = z_ref[0, pl.ds(cf, 1), 0:w]
        zfi = z_ref[0, pl.ds(cf, 1), w:2 * w]
        zrr = z_ref[0, pl.ds(cr, 1), 2 * w:3 * w]
        zri = z_ref[0, pl.ds(cr, 1), 3 * w:4 * w]
        return (afr * fr - afi * fi + zfr, afr * fi + afi * fr + zfi,
                arr * rr - ari * ri + zrr, arr * ri + ari * rr + zri)

    zero = jnp.zeros((1, w), F32)
    carry = lax.fori_loop(0, nctx, lambda k, c: step(k, nctx - 1 - k, c), (zero,) * 4)
    lax.fori_loop(0, nblk - nctx, lambda k, c: step(nctx + k, nblk - 1 - k, c), carry)


def _s5_out_kernel(ub_ref, s_ref, u_ref, mv_ref, d_ref, y_ref):
    nrow = ub_ref.shape[1]
    lhs = jnp.concatenate([ub_ref[0], s_ref[0].astype(BF16)], axis=-1)
    yb = _dot(lhs, mv_ref[0])
    for i in range(S5_T):
        rows = pl.ds(i, nrow, stride=S5_T)
        y_ref[rows, :] = yb[:, i * LANES:(i + 1) * LANES] + u_ref[rows, :] * d_ref[0]


def _s5_mixer(u, mats):
    lt = u.shape[0]
    nblk = lt // S5_T
    nb = S5_GROUPS // S5_GPB
    wide = S5_GPB * S5_T * S5_CH
    swide = 4 * S5_GPB * S5_STATE
    mv, wz, a16, dvec = mats
    tb = max(c for c in range(16, 257, 16) if nblk % c == 0)
    ntile = nblk // tb
    tok = pl.BlockSpec((tb * S5_T, LANES), lambda b, r: (r, b))
    blk = lambda w: pl.BlockSpec((1, tb, w), lambda b, r: (b, r, 0))
    per_b = lambda shape: pl.BlockSpec((1,) + shape, lambda b, r: (b, 0, 0))
    ub, z = pl.pallas_call(
        _s5_gather_kernel,
        grid=(nb, ntile),
        in_specs=[tok, per_b(wz.shape[1:])],
        out_specs=[blk(wide), blk(swide)],
        out_shape=[jax.ShapeDtypeStruct((nb, nblk, wide), BF16), jax.ShapeDtypeStruct((nb, nblk, swide), F32)],
        compiler_params=_cparams(("arbitrary", "arbitrary")),
    )(u, wz)
    s = pl.pallas_call(
        functools.partial(_s5_scan_kernel, nblk=nblk, nctx=CTX_LEN // S5_T),
        grid=(nb,),
        in_specs=[pl.BlockSpec((1, nblk, swide), lambda b: (b, 0, 0)),
                  pl.BlockSpec((1, 8, a16.shape[2]), lambda b: (b, 0, 0))],
        out_specs=pl.BlockSpec((1, nblk, swide), lambda b: (b, 0, 0)),
        out_shape=jax.ShapeDtypeStruct((nb, nblk, swide), F32),
        compiler_params=_cparams(("arbitrary",)),
    )(z, a16)
    return pl.pallas_call(
        _s5_out_kernel,
        grid=(nb, ntile),
        in_specs=[blk(wide), blk(swide), tok,
                  pl.BlockSpec((1,) + mv.shape[1:], lambda b, r: (b, 0, 0), pipeline_mode=pl.Buffered(1)),
                  per_b((1, LANES))],
        out_specs=tok,
        out_shape=jax.ShapeDtypeStruct((lt, GROUP_WIDTH), F32),
        compiler_params=_cparams(("arbitrary", "arbitrary")),
    )(ub, s, u, mv, dvec)


def _even_out_kernel(x_ref, att_ref, y_ref, modx_ref, modc_ref, wglu_ref, bglu_ref, wout_ref, o_ref):
    is_ctx = pl.program_id(0) == 0
    gate_a = jnp.where(is_ctx, modc_ref[2:3, :], modx_ref[2:3, :])
    g = _gelu_tanh(y_ref[...])
    ssm = g * _sigmoid(_dot(g.astype(BF16), wglu_ref[...]) + bglu_ref[...])
    mix = (_dot(att_ref[...].astype(BF16), wout_ref[0:GROUP_WIDTH, :])
           + _dot(ssm.astype(BF16), wout_ref[GROUP_WIDTH:, :]))
    o_ref[...] = x_ref[...] + gate_a * mix


def _even_out(xs, att, y, modx, modc, w_glu, b_glu, w_out):
    lt = xs.shape[0]
    row = lambda w: pl.BlockSpec((ROW_TILE, w), lambda i: (i, 0))
    return pl.pallas_call(
        _even_out_kernel,
        grid=(lt // ROW_TILE,),
        in_specs=[row(D_MODEL), row(GROUP_WIDTH), row(GROUP_WIDTH), _const_spec(modx.shape),
                  _const_spec(modc.shape), _const_spec(w_glu.shape), _const_spec((1, GROUP_WIDTH)),
                  _const_spec(w_out.shape)],
        out_specs=row(D_MODEL),
        out_shape=jax.ShapeDtypeStruct((lt, D_MODEL), F32),
        compiler_params=_cparams(("arbitrary",)),
    )(xs, att, y, modx, modc, w_glu.astype(BF16), b_glu.reshape(1, -1), w_out.astype(BF16))


def _odd_out_kernel(x_ref, att_ref, yf_ref, yr_ref, z_ref, modx_ref, modc_ref, gn_ref, wout_ref, o_ref):
    is_ctx = pl.program_id(0) == 0
    gate_a = jnp.where(is_ctx, modc_ref[2:3, :], modx_ref[2:3, :])
    gy = (yf_ref[...] + yr_ref[...]) * _silu(z_ref[...])
    gw = SSD_INNER // SSD_GROUPS
    parts = [_rms(gy[:, j * gw:(j + 1) * gw]) for j in range(SSD_GROUPS)]
    ssm = jnp.concatenate(parts, axis=-1) * gn_ref[...]
    mix = (_dot(att_ref[...].astype(BF16), wout_ref[0:GROUP_WIDTH, :])
           + _dot(ssm.astype(BF16), wout_ref[GROUP_WIDTH:, :]))
    o_ref[...] = x_ref[...] + gate_a * mix


def _odd_out(xs, att, yf, yr, z, modx, modc, g_norm, w_out):
    lt = xs.shape[0]
    row = lambda w: pl.BlockSpec((ROW_TILE, w), lambda i: (i, 0))
    return pl.pallas_call(
        _odd_out_kernel,
        grid=(lt // ROW_TILE,),
        in_specs=[row(D_MODEL), row(GROUP_WIDTH), row(SSD_INNER), row(SSD_INNER), row(SSD_INNER),
                  _const_spec(modx.shape), _const_spec(modc.shape), _const_spec((1, SSD_INNER)),
                  _const_spec(w_out.shape)],
        out_specs=row(D_MODEL),
        out_shape=jax.ShapeDtypeStruct((lt, D_MODEL), F32),
        compiler_params=_cparams(("arbitrary",)),
    )(xs, att, yf, yr, z, modx, modc, g_norm.reshape(1, -1), w_out.astype(BF16))


def _ffn_kernel(x_ref, modx_ref, modc_ref, w1_ref, w3_ref, w2_ref, o_ref):
    tm = x_ref.shape[0]
    row0 = pl.program_id(0) * tm
    x = x_ref[...]
    sh = _mod_rows(modx_ref, modc_ref, 3, row0, tm)
    sc = _mod_rows(modx_ref, modc_ref, 4, row0, tm)
    gate = _mod_rows(modx_ref, modc_ref, 5, row0, tm)
    h = (_rms(x) * (1.0 + sc) + sh).astype(BF16)
    acc = jnp.zeros((tm, D_MODEL), F32)
    for c in range(D_FF // FFN_CHUNK):
        blk = slice(c * FFN_CHUNK, (c + 1) * FFN_CHUNK)
        a = _dot(h, w1_ref[:, blk])
        b = _dot(h, w3_ref[:, blk])
        acc = acc + _dot((_silu(a) * b).astype(BF16), w2_ref[blk, :])
    o_ref[...] = x + gate * acc


def _ffn(xs, modx, modc, w1, w3, w2):
    lt = xs.shape[0]
    tm = FFN_TILE if lt % FFN_TILE == 0 else ROW_TILE
    row = pl.BlockSpec((tm, D_MODEL), lambda i: (i, 0))
    single = lambda shape: pl.BlockSpec(shape, lambda i: (0, 0), pipeline_mode=pl.Buffered(1))
    return pl.pallas_call(
        _ffn_kernel,
        grid=(lt // tm,),
        in_specs=[row, _const_spec(modx.shape), _const_spec(modc.shape), single(w1.shape),
                  single(w3.shape), single(w2.shape)],
        out_specs=row,
        out_shape=jax.ShapeDtypeStruct((lt, D_MODEL), F32),
        compiler_params=_cparams(("arbitrary",)),
    )(xs, modx, modc, w1.astype(BF16), w3.astype(BF16), w2.astype(BF16))


ODD_COLS = 2048 + SSD_XBC + LANES


def _odd_in_kernel(x_ref, modx_ref, modc_ref, win_ref, gq_ref, gk_ref, e_ref, et_ref, rrow_ref, rcol_ref,
                   q_ref, k_ref, v_ref, z_ref, xbc_ref, dt_ref):
    i = pl.program_id(0)
    is_ctx = i == 0
    sh = jnp.where(is_ctx, modc_ref[0:1, :], modx_ref[0:1, :])
    sc = jnp.where(is_ctx, modc_ref[1:2, :], modx_ref[1:2, :])
    h = (_rms(x_ref[...]) * (1.0 + sc) + sh).astype(BF16)
    proj = _dot(h, win_ref[...])
    ones = jnp.ones((ROW_TILE, DIFF_V), BF16)
    for hd in range(DIFF_HEADS):
        v_ref[:, 2 * hd * DIFF_V:(2 * hd + 1) * DIFF_V] = proj[:, 1024 + hd * DIFF_V:1024 + (hd + 1) * DIFF_V].astype(BF16)
        v_ref[:, (2 * hd + 1) * DIFF_V:(2 * hd + 2) * DIFF_V] = ones
    z_ref[...] = proj[:, 1536:2048]
    xbc_ref[...] = proj[:, 2048:2048 + SSD_XBC]
    dt_ref[...] = proj[:, 2048 + SSD_XBC:2048 + SSD_XBC + 2 * SSD_HEADS]
    nrep = GROUP_WIDTH // LANES
    cos = jnp.concatenate([_rope_tile(rrow_ref, rcol_ref, 0, is_ctx, True)] * nrep, axis=-1)
    sina = jnp.concatenate([_rope_tile(rrow_ref, rcol_ref, 1, is_ctx, False)] * nrep, axis=-1)
    sinb = jnp.concatenate([_rope_tile(rrow_ref, rcol_ref, 2, is_ctx, False)] * nrep, axis=-1)
    half = DIFF_QK // 4
    qscale = DIFF_QK ** -0.5 * LOG2E

    def prep(t, g_ref, scale):
        ss = _split_dot(t * t, e_ref[...], 3)
        r = lax.rsqrt(ss * (1.0 / DIFF_QK) + EPS)
        rb = _split_dot(r, et_ref[...], 3)
        tn = t * rb * g_ref[...]
        up = pltpu.roll(tn, GROUP_WIDTH - half, axis=1)
        dn = pltpu.roll(tn, half, axis=1)
        return ((tn * cos + up * sina + dn * sinb) * scale).astype(BF16)

    q_ref[...] = prep(proj[:, 0:512], gq_ref, qscale)
    k_ref[...] = prep(proj[:, 512:1024], gk_ref, 1.0)


def _odd_in(xs, modx, modc, w_in, g_q, g_k, rope_row, rope_col):
    lt = xs.shape[0]
    win = jnp.pad(w_in, ((0, 0), (0, ODD_COLS - w_in.shape[1]))).astype(BF16)
    nblk = GROUP_WIDTH // DIFF_QK
    e = jnp.zeros((GROUP_WIDTH, LANES), F32).at[np.arange(GROUP_WIDTH), np.arange(GROUP_WIDTH) // DIFF_QK].set(1.0)
    gq = jnp.tile(g_q, nblk).reshape(1, -1)
    gk = jnp.tile(g_k, nblk).reshape(1, -1)
    args = (xs, modx, modc, win, gq, gk, e.astype(BF16), e.T.astype(BF16), rope_row, rope_col)
    row = lambda w: pl.BlockSpec((ROW_TILE, w), lambda i: (i, 0))
    in_specs = ([row(D_MODEL)] + [_const_spec(a.shape) for a in args[1:8]]
                + [pl.BlockSpec((1,) + rope_row.shape[1:], lambda i: (i, 0, 0)), _const_spec(rope_col.shape)])
    ndt = 2 * SSD_HEADS
    return pl.pallas_call(
        _odd_in_kernel,
        grid=(lt // ROW_TILE,),
        in_specs=in_specs,
        out_specs=[row(512), row(512), row(1024), row(512), row(SSD_XBC), row(ndt)],
        out_shape=[jax.ShapeDtypeStruct((lt, 512), BF16), jax.ShapeDtypeStruct((lt, 512), BF16),
                   jax.ShapeDtypeStruct((lt, 1024), BF16), jax.ShapeDtypeStruct((lt, 512), F32),
                   jax.ShapeDtypeStruct((lt, SSD_XBC), F32), jax.ShapeDtypeStruct((lt, ndt), F32)],
        compiler_params=_cparams(("arbitrary",)),
    )(*args)


def _conv_kernel(prev_ref, cur_ref, next_ref, w_ref, b_ref, o_ref, ext_scr, *, ntiles):
    i = pl.program_id(0)
    pad = SSD_CONV // 2
    has_prev = i >= 2
    has_next = jnp.logical_and(i >= 1, i < ntiles - 1)
    ext_scr[0:8, :] = jnp.where(has_prev, prev_ref[...], 0.0)
    ext_scr[8:8 + ROW_TILE, :] = cur_ref[...]
    ext_scr[8 + ROW_TILE:16 + ROW_TILE, :] = jnp.where(has_next, next_ref[...], 0.0)
    acc = jnp.zeros((ROW_TILE, SSD_XBC), F32) + b_ref[...]
    for k in range(SSD_CONV):
        acc = acc + ext_scr[pl.ds(8 - pad + k, ROW_TILE), :] * w_ref[k:k + 1, :]
    o_ref[...] = _silu(acc)


def _ssd_conv(xbc, conv_w, conv_b):
    lt = xbc.shape[0]
    ntiles = lt // ROW_TILE
    per = ROW_TILE // 8
    last8 = lt // 8 - 1
    w = jnp.pad(conv_w, ((0, 8 - SSD_CONV), (0, 0)))
    return pl.pallas_call(
        functools.partial(_conv_kernel, ntiles=ntiles),
        grid=(ntiles,),
        in_specs=[
            pl.BlockSpec((8, SSD_XBC), lambda i: (jnp.maximum(i * per - 1, 0), 0)),
            pl.BlockSpec((ROW_TILE, SSD_XBC), lambda i: (i, 0)),
            pl.BlockSpec((8, SSD_XBC), lambda i: (jnp.minimum((i + 1) * per, last8), 0)),
            _const_spec((8, SSD_XBC)),
            _const_spec((1, SSD_XBC)),
        ],
        out_specs=pl.BlockSpec((ROW_TILE, SSD_XBC), lambda i: (i, 0)),
        out_shape=jax.ShapeDtypeStruct((lt, SSD_XBC), F32),
        scratch_shapes=[pltpu.VMEM((ROW_TILE + 16, SSD_XBC), F32)],
        compiler_params=_cparams(("arbitrary",)),
    )(xbc, xbc, xbc, w, conv_b.reshape(1, -1))


def _ssd_direction(xc, dtr, dtr_t, bias, bias_t, alog, alog_t, dskip, state_ref, rev):
    q = SSD_CHUNK
    hp = SSD_HEAD_DIM
    dt = _softplus(dtr + bias)
    dt_t = _softplus(dtr_t + bias_t)
    a = -jnp.exp(alog)
    a_t = -jnp.exp(alog_t)
    da = dt * a
    da_t = dt_t * a_t
    r_idx = lax.broadcasted_iota(jnp.int32, (q, q), 0)
    c_idx = lax.broadcasted_iota(jnp.int32, (q, q), 1)
    if rev:
        tri = (c_idx >= r_idx).astype(F32)
        keep = c_idx >= r_idx
    else:
        tri = (c_idx <= r_idx).astype(F32)
        keep = c_idx <= r_idx
    cs = jnp.dot(tri, da, precision=HIGHEST, preferred_element_type=F32)
    cs_t = lax.dot_general(da_t, tri, (((1,), (1,)), ((), ())), precision=HIGHEST,
                           preferred_element_type=F32)
    total = cs[0:1, :] if rev else cs[q - 1:q, :]
    x = xc[:, 0:SSD_INNER]
    gw = SSD_GROUPS * SSD_STATE
    bm = xc[:, SSD_INNER:SSD_INNER + gw].astype(BF16)
    cm = xc[:, SSD_INNER + gw:SSD_INNER + 2 * gw].astype(BF16)
    hpg = SSD_HEADS // SSD_GROUPS
    outs = []
    for g in range(SSD_GROUPS):
        bg = bm[:, g * SSD_STATE:(g + 1) * SSD_STATE]
        cg = cm[:, g * SSD_STATE:(g + 1) * SSD_STATE]
        scores = lax.dot_general(cg, bg, (((1,), (1,)), ((), ())), preferred_element_type=F32)
        cols = slice(g * hpg * hp, (g + 1) * hpg * hp)
        s_in = state_ref[:, cols]
        y_off = _dot(cg, s_in.astype(BF16))
        xdt_parts, xdec_parts, ydiag_parts, scale_parts = [], [], [], []
        for j in range(hpg):
            hd = g * hpg + j
            xh = x[:, hd * hp:(hd + 1) * hp]
            xdt = xh * dt[:, hd:hd + 1]
            lmat = jnp.exp(jnp.where(keep, cs[:, hd:hd + 1] - cs_t[hd:hd + 1, :], -jnp.inf))
            ydiag_parts.append(_dot((scores * lmat).astype(BF16), xdt.astype(BF16)))
            xdec_parts.append(xdt * jnp.exp(total[:, hd:hd + 1] - cs[:, hd:hd + 1]))
            scale_parts.append(jnp.broadcast_to(jnp.exp(cs[:, hd:hd + 1]), (q, hp)))
            xdt_parts.append(xh * dskip[:, hd:hd + 1] if dskip is not None else None)
        y = jnp.concatenate(ydiag_parts, axis=-1) + y_off * jnp.concatenate(scale_parts, axis=-1)
        if dskip is not None:
            y = y + jnp.concatenate(xdt_parts, axis=-1)
        outs.append(y)
        xdec = jnp.concatenate(xdec_parts, axis=-1).astype(BF16)
        new = lax.dot_general(bg, xdec, (((0,), (0,)), ((), ())), preferred_element_type=F32)
        chunk_decay = jnp.concatenate(
            [jnp.broadcast_to(jnp.exp(total[:, g * hpg + j:g * hpg + j + 1]), (1, hp)) for j in range(hpg)],
            axis=-1)
        state_ref[:, cols] = s_in * chunk_decay + new
    return jnp.concatenate(outs, axis=-1)


def _ssd_kernel(xf_ref, dtf_ref, dtft_ref, xr_ref, dtr_ref, dtrt_ref, bias_ref, biast_ref, alog_ref,
                alogt_ref, dskip_ref, yf_ref, yr_ref, sf_scr, sr_scr):
    @pl.when(pl.program_id(0) == 0)
    def _():
        sf_scr[...] = jnp.zeros_like(sf_scr)
        sr_scr[...] = jnp.zeros_like(sr_scr)

    h = SSD_HEADS
    yf_ref[...] = _ssd_direction(xf_ref[...], dtf_ref[:, 0:h], dtft_ref[0, 0:h, :], bias_ref[0:1, :],
                                 biast_ref[:, 0:1], alog_ref[0:1, :], alogt_ref[:, 0:1], dskip_ref[...],
                                 sf_scr, False)
    yr_ref[...] = _ssd_direction(xr_ref[...], dtr_ref[:, h:2 * h], dtrt_ref[0, h:2 * h, :], bias_ref[1:2, :],
                                 biast_ref[:, 1:2], alog_ref[1:2, :], alogt_ref[:, 1:2], None, sr_scr, True)


def _ssd_scan(xc, dtr, dt_bias, a_log, d_skip):
    lt = xc.shape[0]
    nc = lt // SSD_CHUNK
    nctx = CTX_LEN // SSD_CHUNK
    ndt = 2 * SSD_HEADS
    dtr_t = dtr.reshape(nc, SSD_CHUNK, ndt).transpose(0, 2, 1)

    def rev_chunk(k):
        return jnp.where(k < nctx, nctx - 1 - k, nc - 1 + nctx - k)

    fwd = lambda w: pl.BlockSpec((SSD_CHUNK, w), lambda k: (k, 0))
    bwd = lambda w: pl.BlockSpec((SSD_CHUNK, w), lambda k: (rev_chunk(k), 0))
    small = (dt_bias, dt_bias.T, a_log, a_log.T, d_skip.reshape(1, -1))
    return pl.pallas_call(
        _ssd_kernel,
        grid=(nc,),
        in_specs=[fwd(SSD_XBC), fwd(ndt), pl.BlockSpec((1, ndt, SSD_CHUNK), lambda k: (k, 0, 0)),
                  bwd(SSD_XBC), bwd(ndt), pl.BlockSpec((1, ndt, SSD_CHUNK), lambda k: (rev_chunk(k), 0, 0))]
                 + [_const_spec(a.shape) for a in small],
        out_specs=[fwd(SSD_INNER), bwd(SSD_INNER)],
        out_shape=[jax.ShapeDtypeStruct((lt, SSD_INNER), F32)] * 2,
        scratch_shapes=[pltpu.VMEM((SSD_STATE, SSD_INNER), F32)] * 2,
        compiler_params=_cparams(("arbitrary",)),
    )(xc, dtr, dtr_t, xc, dtr, dtr_t, *small)


ROWS_PER_TILE = ROW_TILE // GRID_W


def _rope_tables(seq):
    ng = seq // GRID_W
    nctx = CTX_LEN // GRID_W

    def trig(n, count):
        inv = ROPE_THETA ** (-jnp.arange(n, dtype=F32) / n)
        ang = jnp.arange(count, dtype=F32)[:, None] * inv[None, :]
        return jnp.cos(ang), jnp.sin(ang)

    def lay(count, parts):
        return jnp.concatenate([p if hasattr(p, "shape") else jnp.full((count, p[0]), p[1], F32) for p in parts],
                               axis=1)

    def row_slabs(tabs, idents):
        full = [jnp.concatenate([jnp.full((nctx, LANES), ident, F32), t], axis=0) for t, ident in zip(tabs, idents)]
        packed = jnp.concatenate(full, axis=1).reshape(-1, ROWS_PER_TILE, len(tabs) * LANES)
        return jnp.pad(packed, ((0, 0), (0, 8 - ROWS_PER_TILE), (0, 0)))

    n0 = MLA_ROPE // 4
    cr, sr = trig(n0, ng)
    cc, sc = trig(n0, GRID_W)
    tail = LANES - MLA_QK
    row0 = row_slabs([lay(ng, [(MLA_NOPE, 1.0), cr, cr, (2 * n0, 1.0), (tail, 1.0)]),
                      lay(ng, [(MLA_NOPE, 0.0), -sr, sr, (2 * n0, 0.0), (tail, 0.0)])], (1.0, 0.0))
    col0 = jnp.concatenate([lay(GRID_W, [(MLA_NOPE, 1.0), (2 * n0, 1.0), cc, cc, (tail, 1.0)]),
                            lay(GRID_W, [(MLA_NOPE, 0.0), (2 * n0, 0.0), -sc, sc, (tail, 0.0)])], axis=1)
    n1 = DIFF_QK // 4
    cr, sr = trig(n1, ng)
    cc, sc = trig(n1, GRID_W)
    row1 = row_slabs([lay(ng, [cr, cr, (2 * n1, 1.0)] * 2),
                      lay(ng, [-sr, (n1, 0.0), (2 * n1, 0.0)] * 2),
                      lay(ng, [(n1, 0.0), sr, (2 * n1, 0.0)] * 2)], (1.0, 0.0, 0.0))
    col1 = jnp.concatenate([lay(GRID_W, [(2 * n1, 1.0), cc, cc] * 2),
                            lay(GRID_W, [(2 * n1, 0.0), -sc, (n1, 0.0)] * 2),
                            lay(GRID_W, [(2 * n1, 0.0), (n1, 0.0), sc] * 2)], axis=1)
    return row0, col0, row1, col1


def _rope_tile(row_ref, col_ref, k, is_ctx, product):
    r = row_ref[0, :, k * LANES:(k + 1) * LANES]
    c = jnp.where(is_ctx, 1.0 if product else 0.0, col_ref[:, k * LANES:(k + 1) * LANES])
    rb = jnp.concatenate([jnp.broadcast_to(r[j:j + 1], (GRID_W, LANES)) for j in range(ROWS_PER_TILE)], axis=0)
    cb = jnp.concatenate([c] * ROWS_PER_TILE, axis=0)
    return rb * cb if product else rb + cb


def kernel(x, c, ctx, c_ctx, ada_w, ada_b, ffn_w1, ffn_w3, ffn_w2, e_w_in, e_w_out, mla_g_qa, mla_w_qb, mla_g_kva, mla_w_kvb, mla_g_qn, mla_g_kn, s5_lam_re, s5_lam_im, s5_log_dt, s5_b_re, s5_b_im, s5_c_re, s5_c_im, s5_d, s5_w_glu, s5_b_glu, o_w_in, o_w_out, diff_g_q, diff_g_k, diff_lq1, diff_lk1, diff_lq2, diff_lk2, diff_g_o, ssd_conv_w, ssd_conv_b, ssd_dt_bias, ssd_a_log, ssd_d, ssd_g):
    depth = ada_w.shape[0]
    seq = x.shape[1]
    assert x.shape[0] == 1 and ctx.shape[1] == CTX_LEN and seq % ROW_TILE == 0
    xs = jnp.concatenate([ctx[0], x[0]], axis=0)
    mods = _mod_vectors(c, c_ctx, ada_w, ada_b)
    rope_row0, rope_col0, rope_row1, rope_col1 = _rope_tables(seq)
    for i in range(depth):
        j = i // 2
        modx, modc = mods[i, 0], mods[i, 1]
        if i % 2 == 0:
            q, k, v, u = _even_in(xs, modx, modc, e_w_in[j], mla_g_qa[j], mla_w_qb[j], mla_g_kva[j],
                                  mla_w_kvb[j], mla_g_qn[j], mla_g_kn[j], rope_row0, rope_col0)
            att = _mla_attention(q, k, v)
            mats = _s5_matrices(s5_lam_re[j], s5_lam_im[j], s5_log_dt[j], s5_b_re[j], s5_b_im[j],
                                s5_c_re[j], s5_c_im[j], s5_d[j])
            y = _s5_mixer(u, mats)
            xs = _even_out(xs, att, y, modx, modc, s5_w_glu[j], s5_b_glu[j], e_w_out[j])
        else:
            q, k, v, z, xbc, dtr = _odd_in(xs, modx, modc, o_w_in[j], diff_g_q[j], diff_g_k[j],
                                           rope_row1, rope_col1)
            lam_init = 0.8 - 0.6 * math.exp(-0.3 * i)
            lam = (jnp.exp(jnp.sum(diff_lq1[j] * diff_lk1[j])) - jnp.exp(jnp.sum(diff_lq2[j] * diff_lk2[j]))
                   + lam_init)
            att = _diff_attention(q, k, v, jnp.full((1, LANES), lam, F32), diff_g_o[j].reshape(1, -1),
                                  1.0 - lam_init)
            xc = _ssd_conv(xbc, ssd_conv_w[j], ssd_conv_b[j])
            yf, yr = _ssd_scan(xc, dtr, ssd_dt_bias[j], ssd_a_log[j], ssd_d[j])
            xs = _odd_out(xs, att, yf, yr, z, modx, modc, ssd_g[j], o_w_out[j])
        xs = _ffn(xs, modx, modc, ffn_w1[i], ffn_w3[i], ffn_w2[i])
    return xs[CTX_LEN:][None]
```

```python
import functools
import math

import numpy as np
import jax
import jax.numpy as jnp
from jax import lax
from jax.experimental import pallas as pl
from jax.experimental.pallas import tpu as pltpu

F32 = jnp.float32
BF16 = jnp.bfloat16
HIGHEST = lax.Precision.HIGHEST

D_MODEL = 1024
CTX_LEN = 256
GRID_W = 64
GROUP_WIDTH = 512
D_FF = 2816
EPS = 1e-6
ROPE_THETA = 10000.0
LOG2E = math.log2(math.e)

MLA_HEADS, MLA_NOPE, MLA_ROPE, MLA_V = 8, 64, 32, 64
MLA_QK = MLA_NOPE + MLA_ROPE
MLA_Q_RANK, MLA_KV_RANK = 384, 256
S5_CH, S5_GROUPS, S5_STATE = 16, 32, 64
S5_T = 16
S5_GPB = 8
DIFF_HEADS, DIFF_QK, DIFF_V = 4, 64, 128
SSD_HEADS, SSD_HEAD_DIM, SSD_GROUPS, SSD_STATE, SSD_CONV, SSD_CHUNK = 8, 64, 2, 128, 5, 128
SSD_INNER = 512
SSD_XBC = SSD_INNER + 2 * SSD_GROUPS * SSD_STATE

LANES = 128
ROW_TILE = 256
FFN_TILE = 640
FFN_TILE_LATENT = 512
FFN_CHUNK = 256
KV_UNROLL = 4
KV_UNROLL_FULL = 12
VMEM_LIMIT = 56 * 1024 * 1024


def _cparams(sem):
    return pltpu.CompilerParams(dimension_semantics=sem, vmem_limit_bytes=VMEM_LIMIT)


def _dot(a, b):
    return jnp.dot(a, b, preferred_element_type=F32)


def _split_dot(x, w, parts):
    acc = None
    for _ in range(parts):
        piece = x.astype(BF16)
        term = _dot(piece, w)
        acc = term if acc is None else acc + term
        x = x - piece.astype(F32)
    return acc


def _rms(x):
    return x * lax.rsqrt(jnp.mean(x * x, axis=-1, keepdims=True) + EPS)


def _sigmoid(x):
    return 1.0 / (1.0 + jnp.exp(-x))


def _silu(x):
    return x * _sigmoid(x)


def _gelu_tanh(x):
    return 0.5 * x * (1.0 + jnp.tanh(math.sqrt(2.0 / math.pi) * (x + 0.044715 * (x * x * x))))


def _softplus(x):
    return jnp.maximum(x, 0.0) + jnp.log(1.0 + jnp.exp(-jnp.abs(x)))


def _const_spec(shape):
    nd = len(shape)
    return pl.BlockSpec(shape, lambda *_: (0,) * nd)


def _mod_kernel(c_ref, w_ref, b_ref, o_ref):
    s = _silu(c_ref[...]).astype(BF16)
    o_ref[0] = _dot(s, w_ref[0].astype(BF16)) + b_ref[0]


def _mod_vectors(c, c_ctx, ada_w, ada_b):
    depth = ada_w.shape[0]
    cc = jnp.zeros((8, D_MODEL), F32).at[0].set(c[0]).at[1].set(c_ctx)
    nblk = 6
    out = pl.pallas_call(
        _mod_kernel,
        grid=(depth, nblk),
        in_specs=[
            pl.BlockSpec((8, D_MODEL), lambda i, j: (0, 0)),
            pl.BlockSpec((1, D_MODEL, D_MODEL), lambda i, j: (i, 0, j)),
            pl.BlockSpec((1, 1, D_MODEL), lambda i, j: (i, 0, j)),
        ],
        out_specs=pl.BlockSpec((1, 8, D_MODEL), lambda i, j: (i, 0, j)),
        out_shape=jax.ShapeDtypeStruct((depth, 8, 6 * D_MODEL), F32),
        compiler_params=_cparams(("arbitrary", "arbitrary")),
    )(cc, ada_w, ada_b.reshape(depth, 1, 6 * D_MODEL))
    mods = out[:, :2].reshape(depth, 2, 6, D_MODEL)
    return jnp.pad(mods, ((0, 0), (0, 0), (0, 2), (0, 0)))


def _mod_rows(modx_ref, modc_ref, k, row0, nrows):
    rows = row0 + lax.broadcasted_iota(jnp.int32, (nrows, 1), 0)
    return jnp.where(rows < CTX_LEN, modc_ref[k:k + 1, :], modx_ref[k:k + 1, :])


def _rope_partner(n_half):
    return np.array([i + n_half if i < n_half else i - n_half for i in range(2 * n_half)])


def _stream_specs(stream):
    lat, crows, first = stream
    specs = [pl.BlockSpec((ROW_TILE, D_MODEL), lambda i: (jnp.maximum(i - 1 + first, 0), 0)),
             pl.BlockSpec((ROW_TILE, D_MODEL), lambda i: (0, 0))]
    return [lat, crows], specs


def _even_in_kernel(x_ref, c_ref, modx_ref, modc_ref, win_ref, gqa_ref, wq_ref, wqs_ref, gkva_ref, wk_ref,
                    wv_ref, gq_ref, gqs_ref, gk_ref, gks_ref, rrow_ref, rcol_ref,
                    q_ref, k_ref, v_ref, u_ref):
    i = pl.program_id(0)
    is_ctx = i == 0
    sh = jnp.where(is_ctx, modc_ref[0:1, :], modx_ref[0:1, :])
    sc = jnp.where(is_ctx, modc_ref[1:2, :], modx_ref[1:2, :])
    xt = jnp.where(is_ctx, c_ref[...], x_ref[...])
    h = (_rms(xt) * (1.0 + sc) + sh).astype(BF16)
    proj = _dot(h, win_ref[...])
    cq = proj[:, 0:384]
    ckv = proj[:, 384:640]
    u_ref[...] = proj[:, 640:1152]
    krb = proj[:, 1152:1280]
    krs = proj[:, 1280:1408]
    cqn = (_rms(cq) * gqa_ref[...]).astype(BF16)
    qf = _dot(cqn, wq_ref[...])
    qs = _dot(cqn, wqs_ref[...])
    ckvn = (_rms(ckv) * gkva_ref[...]).astype(BF16)
    kf = _dot(ckvn, wk_ref[...])
    vf = _dot(ckvn, wv_ref[...])
    lane = lax.broadcasted_iota(jnp.int32, (1, LANES), 1)
    for pr in range(MLA_HEADS // 2):
        vp = vf[:, pr * LANES:(pr + 1) * LANES]
        v_ref[:, (2 * pr) * LANES:(2 * pr + 1) * LANES] = jnp.where(lane < MLA_V, vp, 1.0).astype(BF16)
        v_ref[:, (2 * pr + 1) * LANES:(2 * pr + 2) * LANES] = jnp.where(lane < MLA_V, 1.0, vp).astype(BF16)
    cos = _rope_tile(rrow_ref, rcol_ref, 0, is_ctx, True)
    sin = _rope_tile(rrow_ref, rcol_ref, 1, is_ctx, False)
    qscale = MLA_QK ** -0.5 * LOG2E
    inv_n = 1.0 / MLA_QK
    for hd in range(MLA_HEADS):
        blk = slice(hd * LANES, (hd + 1) * LANES)
        qh = qf[:, blk]
        rq = lax.rsqrt(jnp.sum(qh * qh, axis=-1, keepdims=True) * inv_n + EPS) * qscale
        qo = (qh * (gq_ref[...] * cos) + qs[:, blk] * (gqs_ref[...] * sin)) * rq
        q_ref[:, blk] = qo.astype(BF16)
        kh = kf[:, blk] + krb
        rk = lax.rsqrt(jnp.sum(kh * kh, axis=-1, keepdims=True) * inv_n + EPS)
        ko = (kh * (gk_ref[...] * cos) + krs * (gks_ref[...] * sin)) * rk
        k_ref[:, blk] = ko.astype(BF16)


def _even_in(stream, lt, modx, modc, w_in, g_qa, w_qb, g_kva, w_kvb, g_qn, g_kn, rope_row, rope_col):
    perm = _rope_partner(MLA_ROPE // 4)
    perm = np.concatenate([perm, perm + MLA_ROPE // 2])
    cq_w, ckv_w = w_in[:, :384], w_in[:, 384:640]
    kr_w, u_w = w_in[:, 640:672], w_in[:, 672:]
    zblk = jnp.zeros((D_MODEL, LANES), F32)
    krblk = zblk.at[:, MLA_NOPE:MLA_QK].set(kr_w)
    krsblk = zblk.at[:, MLA_NOPE:MLA_QK].set(kr_w[:, perm])
    win = jnp.concatenate([cq_w, ckv_w, u_w, krblk, krsblk], axis=1).astype(BF16)
    wq3 = w_qb.reshape(MLA_Q_RANK, MLA_HEADS, MLA_QK)
    wq = jnp.zeros((MLA_Q_RANK, MLA_HEADS, LANES), F32).at[:, :, :MLA_QK].set(wq3)
    wqs = jnp.zeros((MLA_Q_RANK, MLA_HEADS, LANES), F32).at[:, :, MLA_NOPE:MLA_QK].set(
        wq3[:, :, MLA_NOPE:][:, :, perm])
    wkv3 = w_kvb.reshape(MLA_KV_RANK, MLA_HEADS, MLA_NOPE + MLA_V)
    wk = jnp.zeros((MLA_KV_RANK, MLA_HEADS, LANES), F32).at[:, :, :MLA_NOPE].set(wkv3[:, :, :MLA_NOPE])
    wv = wkv3[:, :, MLA_NOPE:].reshape(MLA_KV_RANK, MLA_HEADS * MLA_V)

    def pad_gain(g):
        gp = jnp.zeros((1, LANES), F32).at[0, :MLA_QK].set(g)
        gs = jnp.zeros((1, LANES), F32).at[0, MLA_NOPE:MLA_QK].set(g[MLA_NOPE:][perm])
        return gp, gs

    gq, gqs = pad_gain(g_qn)
    gk, gks = pad_gain(g_kn)
    hw = MLA_HEADS * LANES
    consts = (modx, modc, win, g_qa.reshape(1, -1), wq.reshape(MLA_Q_RANK, hw).astype(BF16),
              wqs.reshape(MLA_Q_RANK, hw).astype(BF16), g_kva.reshape(1, -1),
              wk.reshape(MLA_KV_RANK, hw).astype(BF16), wv.astype(BF16), gq, gqs, gk, gks)
    row = lambda w: pl.BlockSpec((ROW_TILE, w), lambda i: (i, 0))
    rows, row_specs = _stream_specs(stream)
    args = (*rows, *consts, rope_row, rope_col)
    in_specs = (row_specs + [_const_spec(a.shape) for a in consts]
                + [pl.BlockSpec((1,) + rope_row.shape[1:], lambda i: (i, 0, 0)), _const_spec(rope_col.shape)])
    return pl.pallas_call(
        _even_in_kernel,
        grid=(lt // ROW_TILE,),
        in_specs=in_specs,
        out_specs=[row(hw), row(hw), row(hw), row(GROUP_WIDTH)],
        out_shape=[jax.ShapeDtypeStruct((lt, hw), BF16), jax.ShapeDtypeStruct((lt, hw), BF16),
                   jax.ShapeDtypeStruct((lt, hw), BF16),
                   jax.ShapeDtypeStruct((lt, GROUP_WIDTH), F32)],
        compiler_params=_cparams(("arbitrary",)),
    )(*args)


def _kv_chunk(lt):
    for tk in (1280, 640, 256):
        if lt % tk == 0:
            return tk
    raise ValueError(lt)


def _scores(q, k_ref, start, size):
    kc = k_ref[pl.ds(start, size), :]
    return lax.dot_general(q, kc, (((1,), (1,)), ((), ())), preferred_element_type=F32)


def _softmax_update(s_ref, v_ref, start, size, carry):
    m, acc = carry
    mn = jnp.maximum(m, jnp.max(s_ref[...], axis=-1, keepdims=True))
    alpha = jnp.exp2(m - mn)
    p = jnp.exp2(s_ref[...] - mn)
    acc = alpha * acc + _dot(p.astype(BF16), v_ref[pl.ds(start, size), :])
    return mn, acc


def _attend(streams, s_scr, is_ctx, lt, tk, unroll=KV_UNROLL):
    tq = streams[0][0].shape[0]
    n = lt // tk
    init = tuple((jnp.full((tq, 1), -jnp.inf, F32), jnp.zeros((tq, v_ref.shape[-1]), F32))
                 for _, _, v_ref in streams)

    def consume(chunk, slot, carries, prefetch):
        start = chunk * tk
        if not isinstance(start, int):
            start = pl.multiple_of(start, tk)
        out = []
        for t, ((q, k_ref, v_ref), c) in enumerate(zip(streams, carries)):
            if prefetch:
                s_scr[2 * t + 1 - slot] = _scores(q, k_ref, start + tk, tk)
            out.append(_softmax_update(s_scr.at[2 * t + slot], v_ref, start, tk, c))
        return tuple(out)

    def group(i, carries):
        for u in range(unroll):
            carries = consume(unroll * i + u, u % 2, carries, True)
        return carries

    def ctx_branch():
        out = []
        for t, ((q, k_ref, v_ref), c) in enumerate(zip(streams, init)):
            s_scr[2 * t, :, 0:CTX_LEN] = _scores(q, k_ref, 0, CTX_LEN)
            out.append(_softmax_update(s_scr.at[2 * t, :, 0:CTX_LEN], v_ref, 0, CTX_LEN, c))
        return tuple(out)

    def full_branch():
        for t, (q, k_ref, _) in enumerate(streams):
            s_scr[2 * t] = _scores(q, k_ref, 0, tk)
        ngroups = (n - 1) // unroll
        if ngroups == 1:
            carries = group(0, init)
        else:
            carries = lax.fori_loop(0, ngroups, group, init) if ngroups else init
        for chunk in range(unroll * ngroups, n):
            carries = consume(chunk, chunk % 2, carries, chunk < n - 1)
        return carries

    out = lax.cond(is_ctx, ctx_branch, full_branch)
    return [acc for (_, acc) in out]


def _mla_attn_kernel(q_ref, k_ref, v_ref, o_ref, s_scr, *, lt, tk):
    is_ctx = pl.program_id(1) == 0
    lane = lax.broadcasted_iota(jnp.int32, (1, LANES), 1)
    blks = [slice(hh * LANES, (hh + 1) * LANES) for hh in range(2)]
    acc_a, acc_b = _attend([(q_ref[:, b], k_ref.at[:, b], v_ref.at[:, b]) for b in blks], s_scr, is_ctx, lt, tk,
                           unroll=KV_UNROLL_FULL)
    o_ref[...] = jnp.where(lane < MLA_V, acc_a / acc_a[:, MLA_V:MLA_V + 1], acc_b / acc_b[:, 0:1])


def _mla_attention(q, k, v):
    lt = q.shape[0]
    tk = _kv_chunk(lt)
    npair = MLA_HEADS // 2
    return pl.pallas_call(
        functools.partial(_mla_attn_kernel, lt=lt, tk=tk),
        grid=(npair, lt // ROW_TILE),
        in_specs=[
            pl.BlockSpec((ROW_TILE, 2 * LANES), lambda p, i: (i, p)),
            pl.BlockSpec((lt, 2 * LANES), lambda p, i: (0, p), pipeline_mode=pl.Buffered(1)),
            pl.BlockSpec((lt, 2 * LANES), lambda p, i: (0, p), pipeline_mode=pl.Buffered(1)),
        ],
        out_specs=pl.BlockSpec((ROW_TILE, LANES), lambda p, i: (i, p)),
        out_shape=jax.ShapeDtypeStruct((lt, GROUP_WIDTH), F32),
        scratch_shapes=[pltpu.VMEM((4, ROW_TILE, tk), F32)],
        compiler_params=_cparams(("arbitrary", "arbitrary")),
    )(q, k, v)


def _diff_attn_kernel(q_ref, k_ref, v_ref, lam_ref, go_ref, o_ref, s_scr, *, lt, tk, out_scale):
    is_ctx = pl.program_id(1) == 0
    lane = lax.broadcasted_iota(jnp.int32, (1, LANES), 1)
    q = q_ref[...]
    zero = jnp.zeros_like(q)
    q1 = jnp.where(lane < DIFF_QK, q, zero)
    q2 = jnp.where(lane < DIFF_QK, zero, q)
    a1, a2 = _attend([(q1, k_ref, v_ref), (q2, k_ref, v_ref)], s_scr, is_ctx, lt, tk)
    o = (a1[:, 0:DIFF_V] / a1[:, DIFF_V:DIFF_V + 1]
         - lam_ref[...] * (a2[:, 0:DIFF_V] / a2[:, DIFF_V:DIFF_V + 1]))
    o_ref[...] = _rms(o) * go_ref[...] * out_scale


def _diff_attention(q, k, v, lam, g_o, out_scale):
    lt = q.shape[0]
    tk = _kv_chunk(lt)
    return pl.pallas_call(
        functools.partial(_diff_attn_kernel, lt=lt, tk=tk, out_scale=out_scale),
        grid=(DIFF_HEADS, lt // ROW_TILE),
        in_specs=[
            pl.BlockSpec((ROW_TILE, LANES), lambda h, i: (i, h)),
            pl.BlockSpec((lt, LANES), lambda h, i: (0, h), pipeline_mode=pl.Buffered(1)),
            pl.BlockSpec((lt, 2 * LANES), lambda h, i: (0, h), pipeline_mode=pl.Buffered(1)),
            _const_spec((1, LANES)),
            _const_spec((1, LANES)),
        ],
        out_specs=pl.BlockSpec((ROW_TILE, LANES), lambda h, i: (i, h)),
        out_shape=jax.ShapeDtypeStruct((lt, GROUP_WIDTH), F32),
        scratch_shapes=[pltpu.VMEM((4, ROW_TILE, tk), F32)],
        compiler_params=_cparams(("arbitrary", "arbitrary")),
    )(q, k, v, lam, g_o)


def _s5_matrices(lam_re, lam_im, log_dt, b_re, b_im, c_re, c_im, d_skip):
    t = S5_T
    g, n, ch = S5_GROUPS, S5_STATE, S5_CH
    dt = jnp.exp(log_dt)[:, :, None, None]
    tau = jnp.arange(t + 1, dtype=F32)
    mag = jnp.exp(lam_re[..., None] * dt * tau)
    ang = lam_im[..., None] * dt * tau
    p_re, p_im = mag * jnp.cos(ang), mag * jnp.sin(ang)
    a_re, a_im = p_re[..., 1], p_im[..., 1]
    den = lam_re * lam_re + lam_im * lam_im
    k_re = ((a_re - 1.0) * lam_re + a_im * lam_im) / den
    k_im = (a_im * lam_re - (a_re - 1.0) * lam_im) / den
    bb_re = k_re[..., None] * b_re - k_im[..., None] * b_im
    bb_im = k_re[..., None] * b_im + k_im[..., None] * b_re
    ca_re = c_re[..., None] * p_re[:, :, None] - c_im[..., None] * p_im[:, :, None]
    ca_im = c_re[..., None] * p_im[:, :, None] + c_im[..., None] * p_re[:, :, None]
    ca_t = lambda x: x.transpose(0, 1, 4, 2, 3)[:, :, :, :, None, :]
    bb_t = lambda x: x.transpose(0, 1, 3, 2)[:, :, None, None, :, :]
    kk = jnp.sum(ca_t(ca_re) * bb_t(bb_re) - ca_t(ca_im) * bb_t(bb_im), axis=-1)
    ii = np.arange(t)
    lag = ii[None, :] - ii[:, None]
    mf = jnp.where((lag >= 0)[None, :, :, None, None], kk[0][:, np.clip(lag, 0, t)], 0.0)
    mr = jnp.where((lag <= 0)[None, :, :, None, None], kk[1][:, np.clip(-lag, 0, t)], 0.0)
    m = (mf + mr).transpose(0, 1, 4, 2, 3).reshape(g, t * ch, t * ch)

    def w_of(d, expo):
        pr = p_re[d][:, :, expo].transpose(0, 2, 1)[:, :, None, :]
        pi = p_im[d][:, :, expo].transpose(0, 2, 1)[:, :, None, :]
        br = bb_re[d].transpose(0, 2, 1)[:, None]
        bi = bb_im[d].transpose(0, 2, 1)[:, None]
        return pr * br - pi * bi, pr * bi + pi * br

    wf_re, wf_im = w_of(0, t - 1 - ii)
    wr_re, wr_im = w_of(1, ii)
    w4 = jnp.stack([wf_re, wf_im, wr_re, wr_im], axis=3)

    def v_of(d, expo):
        gr = ca_re[d][:, :, :, expo].transpose(0, 2, 3, 1)
        gi = ca_im[d][:, :, :, expo].transpose(0, 2, 3, 1)
        return gr, -gi

    vf_re, vf_im = v_of(0, ii + 1)
    vr_re, vr_im = v_of(1, t - ii)
    v4 = jnp.stack([vf_re, vf_im, vr_re, vr_im], axis=0)
    nb, gp = g // S5_GPB, S5_GPB
    wide = gp * t * ch
    mc = m.reshape(nb, gp, t, ch, t * ch).transpose(0, 2, 1, 3, 4).reshape(nb, wide, t * ch)
    wc = w4.reshape(nb, gp, t, ch, 4 * n).transpose(0, 2, 1, 3, 4).reshape(nb, wide, 4 * n)
    vc = v4.reshape(4, nb, gp, n, t * ch).transpose(1, 0, 2, 3, 4).reshape(nb, 4 * gp * n, t * ch)
    m_big = _s5_expand(mc.astype(BF16), row_bits=4, col_bits=4)
    w_big = _s5_expand(wc.astype(BF16), row_bits=4, col_bits=6)
    v_big = _s5_expand(vc.astype(BF16), row_bits=6, col_bits=4)
    a16 = jnp.stack([p_re[0][..., t], p_im[0][..., t], p_re[1][..., t], p_im[1][..., t]], axis=0)
    a16 = a16.reshape(4, nb, gp * n).transpose(1, 0, 2)
    a16 = jnp.pad(a16, ((0, 0), (0, 4), (0, 0)))
    return m_big, v_big, w_big, a16, d_skip.reshape(nb, 1, gp * ch)


def _s5_expand_kernel(c_ref, o_ref, *, row_bits, col_bits):
    nrow, ncol = o_ref.shape[1], o_ref.shape[2]
    kc = c_ref.shape[2]
    gbits = S5_GPB.bit_length() - 1
    r = lax.broadcasted_iota(jnp.int32, (kc, ncol), 0)
    c = lax.broadcasted_iota(jnp.int32, (kc, ncol), 1)
    bmask = (1 << col_bits) - 1
    same = jnp.logical_and((r >> col_bits) == (c >> (col_bits + gbits)), (r & bmask) == (c & bmask))
    tiling = jnp.where(same, 1.0, 0.0).astype(BF16)
    slab = S5_GPB << row_bits
    rr = lax.broadcasted_iota(jnp.int32, (slab, ncol), 0)
    cc = lax.broadcasted_iota(jnp.int32, (slab, ncol), 1)
    keep = ((rr >> row_bits) & (S5_GPB - 1)) == ((cc >> col_bits) & (S5_GPB - 1))
    for k in range(nrow // slab):
        rows = slice(k * slab, (k + 1) * slab)
        o_ref[0, rows, :] = jnp.where(keep, _dot(c_ref[0, rows, :], tiling), 0.0).astype(BF16)


def _s5_expand(compact, row_bits, col_bits):
    nb, nrow, kc = compact.shape
    ncol = kc * S5_GPB
    return pl.pallas_call(
        functools.partial(_s5_expand_kernel, row_bits=row_bits, col_bits=col_bits),
        grid=(nb,),
        in_specs=[pl.BlockSpec((1, nrow, kc), lambda b: (b, 0, 0))],
        out_specs=pl.BlockSpec((1, nrow, ncol), lambda b: (b, 0, 0)),
        out_shape=jax.ShapeDtypeStruct((nb, nrow, ncol), BF16),
        compiler_params=_cparams(("arbitrary",)),
    )(compact)


def _s5_gather_kernel(u_ref, wz_ref, ub_ref, z_ref):
    nrow = ub_ref.shape[1]
    for j in range(S5_T):
        ub_ref[0, :, j * LANES:(j + 1) * LANES] = u_ref[pl.ds(j, nrow, stride=S5_T), :].astype(BF16)
    z_ref[0] = _dot(ub_ref[0], wz_ref[0])


def _s5_scan_kernel(z_ref, a_ref, s_ref, *, nblk, nctx):
    w = S5_GPB * S5_STATE
    a = a_ref[0]
    afr, afi, arr, ari = a[0:1], a[1:2], a[2:3], a[3:4]

    def step(cf, cr, carry):
        fr, fi, rr, ri = carry
        s_ref[0, pl.ds(cf, 1), 0:w] = fr
        s_ref[0, pl.ds(cf, 1), w:2 * w] = fi
        s_ref[0, pl.ds(cr, 1), 2 * w:3 * w] = rr
        s_ref[0, pl.ds(cr, 1), 3 * w:4 * w] = ri
        zfr = z_ref[0, pl.ds(cf, 1), 0:w]
        zfi = z_ref[0, pl.ds(cf, 1), w:2 * w]
        zrr = z_ref[0, pl.ds(cr, 1), 2 * w:3 * w]
        zri = z_ref[0, pl.ds(cr, 1), 3 * w:4 * w]
        return (afr * fr - afi * fi + zfr, afr * fi + afi * fr + zfi,
                arr * rr - ari * ri + zrr, arr * ri + ari * rr + zri)

    zero = jnp.zeros((1, w), F32)
    carry = lax.fori_loop(0, nctx, lambda k, c: step(k, nctx - 1 - k, c), (zero,) * 4)
    lax.fori_loop(0, nblk - nctx, lambda k, c: step(nctx + k, nblk - 1 - k, c), carry)


def _s5_out_kernel(ub_ref, s_ref, u_ref, m_ref, v_ref, d_ref, y_ref):
    nrow = ub_ref.shape[1]
    yb = _dot(ub_ref[0], m_ref[0]) + _dot(s_ref[0].astype(BF16), v_ref[0])
    for i in range(S5_T):
        rows = pl.ds(i, nrow, stride=S5_T)
        y_ref[rows, :] = yb[:, i * LANES:(i + 1) * LANES] + u_ref[rows, :] * d_ref[0]


def _s5_mixer(u, mats):
    lt = u.shape[0]
    nblk = lt // S5_T
    nb = S5_GROUPS // S5_GPB
    wide = S5_GPB * S5_T * S5_CH
    swide = 4 * S5_GPB * S5_STATE
    m_big, v_big, wz, a16, dvec = mats
    tb = max(c for c in range(16, 257, 16) if nblk % c == 0)
    ntile = nblk // tb
    tok = pl.BlockSpec((tb * S5_T, LANES), lambda b, r: (r, b))
    blk = lambda w: pl.BlockSpec((1, tb, w), lambda b, r: (b, r, 0))
    per_b = lambda shape: pl.BlockSpec((1,) + shape, lambda b, r: (b, 0, 0))
    ub, z = pl.pallas_call(
        _s5_gather_kernel,
        grid=(nb, ntile),
        in_specs=[tok, per_b(wz.shape[1:])],
        out_specs=[blk(wide), blk(swide)],
        out_shape=[jax.ShapeDtypeStruct((nb, nblk, wide), BF16), jax.ShapeDtypeStruct((nb, nblk, swide), F32)],
        compiler_params=_cparams(("arbitrary", "arbitrary")),
    )(u, wz)
    s = pl.pallas_call(
        functools.partial(_s5_scan_kernel, nblk=nblk, nctx=CTX_LEN // S5_T),
        grid=(nb,),
        in_specs=[pl.BlockSpec((1, nblk, swide), lambda b: (b, 0, 0)),
                  pl.BlockSpec((1, 8, a16.shape[2]), lambda b: (b, 0, 0))],
        out_specs=pl.BlockSpec((1, nblk, swide), lambda b: (b, 0, 0)),
        out_shape=jax.ShapeDtypeStruct((nb, nblk, swide), F32),
        compiler_params=_cparams(("arbitrary",)),
    )(z, a16)
    return pl.pallas_call(
        _s5_out_kernel,
        grid=(nb, ntile),
        in_specs=[blk(wide), blk(swide), tok,
                  pl.BlockSpec((1,) + m_big.shape[1:], lambda b, r: (b, 0, 0), pipeline_mode=pl.Buffered(1)),
                  pl.BlockSpec((1,) + v_big.shape[1:], lambda b, r: (b, 0, 0), pipeline_mode=pl.Buffered(1)),
                  per_b((1, LANES))],
        out_specs=tok,
        out_shape=jax.ShapeDtypeStruct((lt, GROUP_WIDTH), F32),
        compiler_params=_cparams(("arbitrary", "arbitrary")),
    )(ub, s, u, m_big, v_big, dvec)


def _even_out_kernel(x_ref, c_ref, att_ref, y_ref, modx_ref, modc_ref, wglu_ref, bglu_ref, wout_ref, o_ref):
    is_ctx = pl.program_id(0) == 0
    gate_a = jnp.where(is_ctx, modc_ref[2:3, :], modx_ref[2:3, :])
    g = _gelu_tanh(y_ref[...])
    ssm = g * _sigmoid(_dot(g.astype(BF16), wglu_ref[...]) + bglu_ref[...])
    mix = (_dot(att_ref[...].astype(BF16), wout_ref[0:GROUP_WIDTH, :])
           + _dot(ssm.astype(BF16), wout_ref[GROUP_WIDTH:, :]))
    o_ref[...] = jnp.where(is_ctx, c_ref[...], x_ref[...]) + gate_a * mix


def _even_out(stream, att, y, modx, modc, w_glu, b_glu, w_out):
    lt = att.shape[0]
    row = lambda w: pl.BlockSpec((ROW_TILE, w), lambda i: (i, 0))
    rows, row_specs = _stream_specs(stream)
    return pl.pallas_call(
        _even_out_kernel,
        grid=(lt // ROW_TILE,),
        in_specs=row_specs + [row(GROUP_WIDTH), row(GROUP_WIDTH), _const_spec(modx.shape),
                              _const_spec(modc.shape), _const_spec(w_glu.shape), _const_spec((1, GROUP_WIDTH)),
                              _const_spec(w_out.shape)],
        out_specs=row(D_MODEL),
        out_shape=jax.ShapeDtypeStruct((lt, D_MODEL), F32),
        compiler_params=_cparams(("arbitrary",)),
    )(*rows, att, y, modx, modc, w_glu.astype(BF16), b_glu.reshape(1, -1), w_out.astype(BF16))


def _odd_out_kernel(x_ref, att_ref, yf_ref, yr_ref, z_ref, modx_ref, modc_ref, gn_ref, wout_ref, o_ref):
    is_ctx = pl.program_id(0) == 0
    gate_a = jnp.where(is_ctx, modc_ref[2:3, :], modx_ref[2:3, :])
    gy = (yf_ref[...] + yr_ref[...]) * _silu(z_ref[...])
    gw = SSD_INNER // SSD_GROUPS
    parts = [_rms(gy[:, j * gw:(j + 1) * gw]) for j in range(SSD_GROUPS)]
    ssm = jnp.concatenate(parts, axis=-1) * gn_ref[...]
    mix = (_dot(att_ref[...].astype(BF16), wout_ref[0:GROUP_WIDTH, :])
           + _dot(ssm.astype(BF16), wout_ref[GROUP_WIDTH:, :]))
    o_ref[...] = x_ref[...] + gate_a * mix


def _odd_out(xs, att, yf, yr, z, modx, modc, g_norm, w_out):
    lt = xs.shape[0]
    row = lambda w: pl.BlockSpec((ROW_TILE, w), lambda i: (i, 0))
    return pl.pallas_call(
        _odd_out_kernel,
        grid=(lt // ROW_TILE,),
        in_specs=[row(D_MODEL), row(GROUP_WIDTH), row(SSD_INNER), row(SSD_INNER), row(SSD_INNER),
                  _const_spec(modx.shape), _const_spec(modc.shape), _const_spec((1, SSD_INNER)),
                  _const_spec(w_out.shape)],
        out_specs=row(D_MODEL),
        out_shape=jax.ShapeDtypeStruct((lt, D_MODEL), F32),
        compiler_params=_cparams(("arbitrary",)),
    )(xs, att, yf, yr, z, modx, modc, g_norm.reshape(1, -1), w_out.astype(BF16))


def _ffn_kernel(x_ref, modx_ref, modc_ref, w1_ref, w3_ref, w2_ref, o_ref, *, row_offset):
    tm = x_ref.shape[0]
    row0 = row_offset + pl.program_id(0) * tm
    x = x_ref[...]
    sh = _mod_rows(modx_ref, modc_ref, 3, row0, tm)
    sc = _mod_rows(modx_ref, modc_ref, 4, row0, tm)
    gate = _mod_rows(modx_ref, modc_ref, 5, row0, tm)
    h = (_rms(x) * (1.0 + sc) + sh).astype(BF16)
    acc = jnp.zeros((tm, D_MODEL), F32)
    for c in range(D_FF // FFN_CHUNK):
        blk = slice(c * FFN_CHUNK, (c + 1) * FFN_CHUNK)
        a = _dot(h, w1_ref[:, blk])
        b = _dot(h, w3_ref[:, blk])
        acc = acc + _dot((_silu(a) * b).astype(BF16), w2_ref[blk, :])
    o_ref[...] = x + gate * acc


def _ffn(xs, modx, modc, w1, w3, w2, latents_only=False):
    lt = xs.shape[0]
    if latents_only:
        nrows = lt - CTX_LEN
        tm = FFN_TILE_LATENT if nrows % FFN_TILE_LATENT == 0 else ROW_TILE
        in_row = pl.BlockSpec((pl.Element(tm), pl.Element(D_MODEL)), lambda i: (pl.multiple_of(CTX_LEN + i * tm, ROW_TILE), 0))
    else:
        nrows = lt
        tm = FFN_TILE if lt % FFN_TILE == 0 else ROW_TILE
        in_row = pl.BlockSpec((tm, D_MODEL), lambda i: (i, 0))
    single = lambda shape: pl.BlockSpec(shape, lambda i: (0, 0), pipeline_mode=pl.Buffered(1))
    return pl.pallas_call(
        functools.partial(_ffn_kernel, row_offset=lt - nrows),
        grid=(nrows // tm,),
        in_specs=[in_row, _const_spec(modx.shape), _const_spec(modc.shape), single(w1.shape),
                  single(w3.shape), single(w2.shape)],
        out_specs=pl.BlockSpec((tm, D_MODEL), lambda i: (i, 0)),
        out_shape=jax.ShapeDtypeStruct((nrows, D_MODEL), F32),
        compiler_params=_cparams(("arbitrary",)),
    )(xs, modx, modc, w1.astype(BF16), w3.astype(BF16), w2.astype(BF16))


ODD_COLS = 2048 + SSD_XBC + LANES


def _odd_in_kernel(x_ref, modx_ref, modc_ref, win_ref, gq_ref, gk_ref, e_ref, et_ref, rrow_ref, rcol_ref,
                   q_ref, k_ref, v_ref, z_ref, xbc_ref, dt_ref):
    i = pl.program_id(0)
    is_ctx = i == 0
    sh = jnp.where(is_ctx, modc_ref[0:1, :], modx_ref[0:1, :])
    sc = jnp.where(is_ctx, modc_ref[1:2, :], modx_ref[1:2, :])
    h = (_rms(x_ref[...]) * (1.0 + sc) + sh).astype(BF16)
    proj = _dot(h, win_ref[...])
    ones = jnp.ones((ROW_TILE, DIFF_V), BF16)
    for hd in range(DIFF_HEADS):
        v_ref[:, 2 * hd * DIFF_V:(2 * hd + 1) * DIFF_V] = proj[:, 1024 + hd * DIFF_V:1024 + (hd + 1) * DIFF_V].astype(BF16)
        v_ref[:, (2 * hd + 1) * DIFF_V:(2 * hd + 2) * DIFF_V] = ones
    z_ref[...] = proj[:, 1536:2048]
    xbc_ref[...] = proj[:, 2048:2048 + SSD_XBC]
    dt_ref[...] = proj[:, 2048 + SSD_XBC:ODD_COLS]
    nrep = GROUP_WIDTH // LANES
    cos = jnp.concatenate([_rope_tile(rrow_ref, rcol_ref, 0, is_ctx, True)] * nrep, axis=-1)
    sina = jnp.concatenate([_rope_tile(rrow_ref, rcol_ref, 1, is_ctx, False)] * nrep, axis=-1)
    sinb = jnp.concatenate([_rope_tile(rrow_ref, rcol_ref, 2, is_ctx, False)] * nrep, axis=-1)
    half = DIFF_QK // 4
    qscale = DIFF_QK ** -0.5 * LOG2E

    def prep(t, g_ref, scale):
        ss = _split_dot(t * t, e_ref[...], 3)
        r = lax.rsqrt(ss * (1.0 / DIFF_QK) + EPS)
        rb = _split_dot(r, et_ref[...], 3)
        tn = t * rb * g_ref[...]
        up = pltpu.roll(tn, GROUP_WIDTH - half, axis=1)
        dn = pltpu.roll(tn, half, axis=1)
        return ((tn * cos + up * sina + dn * sinb) * scale).astype(BF16)

    q_ref[...] = prep(proj[:, 0:512], gq_ref, qscale)
    k_ref[...] = prep(proj[:, 512:1024], gk_ref, 1.0)


def _odd_in(xs, modx, modc, w_in, g_q, g_k, rope_row, rope_col):
    lt = xs.shape[0]
    win = jnp.pad(w_in, ((0, 0), (0, ODD_COLS - w_in.shape[1]))).astype(BF16)
    nblk = GROUP_WIDTH // DIFF_QK
    e = jnp.zeros((GROUP_WIDTH, LANES), F32).at[np.arange(GROUP_WIDTH), np.arange(GROUP_WIDTH) // DIFF_QK].set(1.0)
    gq = jnp.tile(g_q, nblk).reshape(1, -1)
    gk = jnp.tile(g_k, nblk).reshape(1, -1)
    args = (xs, modx, modc, win, gq, gk, e.astype(BF16), e.T.astype(BF16), rope_row, rope_col)
    row = lambda w: pl.BlockSpec((ROW_TILE, w), lambda i: (i, 0))
    in_specs = ([row(D_MODEL)] + [_const_spec(a.shape) for a in args[1:8]]
                + [pl.BlockSpec((1,) + rope_row.shape[1:], lambda i: (i, 0, 0)), _const_spec(rope_col.shape)])
    return pl.pallas_call(
        _odd_in_kernel,
        grid=(lt // ROW_TILE,),
        in_specs=in_specs,
        out_specs=[row(512), row(512), row(1024), row(512), row(SSD_XBC), row(LANES)],
        out_shape=[jax.ShapeDtypeStruct((lt, 512), BF16), jax.ShapeDtypeStruct((lt, 512), BF16),
                   jax.ShapeDtypeStruct((lt, 1024), BF16), jax.ShapeDtypeStruct((lt, 512), F32),
                   jax.ShapeDtypeStruct((lt, SSD_XBC), F32), jax.ShapeDtypeStruct((lt, LANES), F32)],
        compiler_params=_cparams(("arbitrary",)),
    )(*args)


def _conv_kernel(prev_ref, cur_ref, next_ref, w_ref, b_ref, o_ref, ext_scr, *, ntiles):
    i = pl.program_id(0)
    pad = SSD_CONV // 2
    has_prev = i >= 2
    has_next = jnp.logical_and(i >= 1, i < ntiles - 1)
    ext_scr[0:8, :] = jnp.where(has_prev, prev_ref[...], 0.0)
    ext_scr[8:8 + ROW_TILE, :] = cur_ref[...]
    ext_scr[8 + ROW_TILE:16 + ROW_TILE, :] = jnp.where(has_next, next_ref[...], 0.0)
    acc = jnp.zeros((ROW_TILE, SSD_XBC), F32) + b_ref[...]
    for k in range(SSD_CONV):
        acc = acc + ext_scr[pl.ds(8 - pad + k, ROW_TILE), :] * w_ref[k:k + 1, :]
    o_ref[...] = _silu(acc)


def _ssd_conv(xbc, conv_w, conv_b):
    lt = xbc.shape[0]
    ntiles = lt // ROW_TILE
    per = ROW_TILE // 8
    last8 = lt // 8 - 1
    w = jnp.pad(conv_w, ((0, 8 - SSD_CONV), (0, 0)))
    return pl.pallas_call(
        functools.partial(_conv_kernel, ntiles=ntiles),
        grid=(ntiles,),
        in_specs=[
            pl.BlockSpec((8, SSD_XBC), lambda i: (jnp.maximum(i * per - 1, 0), 0)),
            pl.BlockSpec((ROW_TILE, SSD_XBC), lambda i: (i, 0)),
            pl.BlockSpec((8, SSD_XBC), lambda i: (jnp.minimum((i + 1) * per, last8), 0)),
            _const_spec((8, SSD_XBC)),
            _const_spec((1, SSD_XBC)),
        ],
        out_specs=pl.BlockSpec((ROW_TILE, SSD_XBC), lambda i: (i, 0)),
        out_shape=jax.ShapeDtypeStruct((lt, SSD_XBC), F32),
        scratch_shapes=[pltpu.VMEM((ROW_TILE + 16, SSD_XBC), F32)],
        compiler_params=_cparams(("arbitrary",)),
    )(xbc, xbc, xbc, w, conv_b.reshape(1, -1))


def _split_parts(x, parts):
    out = []
    for _ in range(parts):
        piece = x.astype(BF16)
        out.append(piece)
        x = x - piece.astype(F32)
    return out


def _ssd_direction(xc, dtr, dtr_t, bias, alog, bias_t, alog_t, sel, dskip, state_ref, rev):
    q = SSD_CHUNK
    hp = SSD_HEAD_DIM
    dt = _softplus(dtr + bias)
    da = dt * -jnp.exp(alog)
    dt_t = _softplus(dtr_t + bias_t)
    da_t = dt_t * -jnp.exp(alog_t)
    r_idx = lax.broadcasted_iota(jnp.int32, (q, q), 0)
    c_idx = lax.broadcasted_iota(jnp.int32, (q, q), 1)
    keep = c_idx >= r_idx if rev else c_idx <= r_idx
    tri = jnp.where(keep, 1.0, 0.0).astype(BF16)
    nt = (((1,), (1,)), ((), ()))
    cs = sum(_dot(tri, p) for p in _split_parts(da, 3))
    cs_t = sum(lax.dot_general(p, tri, nt, preferred_element_type=F32) for p in _split_parts(da_t, 3))
    csb = sum(_dot(p, sel) for p in _split_parts(cs, 3))
    dtb = sum(_dot(p, sel) for p in _split_parts(dt, 2))
    tot = csb[0:1, :] if rev else csb[q - 1:q, :]
    x = xc[:, 0:SSD_INNER]
    xdt = x * dtb
    xdt_b = xdt.astype(BF16)
    xdec = (xdt * jnp.exp(tot - csb)).astype(BF16)
    csb_up = pltpu.roll(csb, SSD_INNER - hp, axis=1)
    csb_dn = pltpu.roll(csb, hp, axis=1)
    low = lax.broadcasted_iota(jnp.int32, (1, 2 * hp), 1) < hp
    gw = SSD_GROUPS * SSD_STATE
    bm = xc[:, SSD_INNER:SSD_INNER + gw].astype(BF16)
    cm = xc[:, SSD_INNER + gw:SSD_INNER + 2 * gw].astype(BF16)
    hpg = SSD_HEADS // SSD_GROUPS
    head_row = SSD_HEADS if rev else 0
    outs = []
    for g in range(SSD_GROUPS):
        bg = bm[:, g * SSD_STATE:(g + 1) * SSD_STATE]
        cg = cm[:, g * SSD_STATE:(g + 1) * SSD_STATE]
        scores = lax.dot_general(cg, bg, nt, preferred_element_type=F32)
        cols = slice(g * hpg * hp, (g + 1) * hpg * hp)
        s_in = state_ref[:, cols]
        y_off = _dot(cg, s_in.astype(BF16))
        pairs = []
        for pp in range(hpg // 2):
            blk = slice((g * hpg + 2 * pp) * hp, (g * hpg + 2 * pp + 2) * hp)
            here, up, dn = csb[:, blk], csb_up[:, blk], csb_dn[:, blk]
            ys = []
            for e in range(2):
                hd = g * hpg + 2 * pp + e
                col = jnp.where(low, here, dn) if e == 0 else jnp.where(low, up, here)
                row = cs_t[head_row + hd:head_row + hd + 1, :]
                lmat = jnp.exp(jnp.where(keep, col - row, -jnp.inf))
                ys.append(_dot((scores * lmat).astype(BF16), xdt_b[:, blk]))
            pairs.append(jnp.where(low, ys[0], ys[1]))
        y = jnp.concatenate(pairs, axis=-1) + y_off * jnp.exp(csb[:, cols])
        if dskip is not None:
            y = y + x[:, cols] * dskip[:, cols]
        outs.append(y)
        new = lax.dot_general(bg, xdec[:, cols], (((0,), (0,)), ((), ())), preferred_element_type=F32)
        state_ref[:, cols] = s_in * jnp.exp(tot[:, cols]) + new
    return jnp.concatenate(outs, axis=-1)


def _ssd_kernel(xf_ref, dtf_ref, dtft_ref, xr_ref, dtr_ref, dtrt_ref, bias_ref, alog_ref, biast_ref,
                alogt_ref, sel_ref, dskip_ref, yf_ref, yr_ref, sf_scr, sr_scr):
    @pl.when(pl.program_id(0) == 0)
    def _():
        sf_scr[...] = jnp.zeros_like(sf_scr)
        sr_scr[...] = jnp.zeros_like(sr_scr)

    consts = (bias_ref[...], alog_ref[...], biast_ref[...], alogt_ref[...])
    yf_ref[...] = _ssd_direction(xf_ref[...], dtf_ref[...], dtft_ref[0], *consts, sel_ref[0], dskip_ref[...],
                                 sf_scr, False)
    yr_ref[...] = _ssd_direction(xr_ref[...], dtr_ref[...], dtrt_ref[0], *consts, sel_ref[1], None, sr_scr, True)


def _ssd_scan(xc, dtr, dt_bias, a_log, d_skip):
    lt = xc.shape[0]
    nc = lt // SSD_CHUNK
    nctx = CTX_LEN // SSD_CHUNK
    ndt = 2 * SSD_HEADS
    dtr_t = dtr[:, :ndt].reshape(nc, SSD_CHUNK, ndt).transpose(0, 2, 1)

    def rev_chunk(k):
        return jnp.where(k < nctx, nctx - 1 - k, nc - 1 + nctx - k)

    fwd = lambda w: pl.BlockSpec((SSD_CHUNK, w), lambda k: (k, 0))
    bwd = lambda w: pl.BlockSpec((SSD_CHUNK, w), lambda k: (rev_chunk(k), 0))
    lanes = lambda v: jnp.pad(v.reshape(1, ndt), ((0, 0), (0, LANES - ndt)))
    rows = lambda v: jnp.broadcast_to(v.reshape(ndt, 1), (ndt, SSD_CHUNK))
    sel = np.zeros((2, LANES, SSD_INNER), np.float32)
    for d in range(2):
        for hd in range(SSD_HEADS):
            sel[d, d * SSD_HEADS + hd, hd * SSD_HEAD_DIM:(hd + 1) * SSD_HEAD_DIM] = 1.0
    small = (lanes(dt_bias), lanes(a_log), rows(dt_bias), rows(a_log), jnp.asarray(sel, BF16),
             jnp.repeat(d_skip, SSD_HEAD_DIM).reshape(1, -1))
    return pl.pallas_call(
        _ssd_kernel,
        grid=(nc,),
        in_specs=[fwd(SSD_XBC), fwd(LANES), pl.BlockSpec((1, ndt, SSD_CHUNK), lambda k: (k, 0, 0)),
                  bwd(SSD_XBC), bwd(LANES), pl.BlockSpec((1, ndt, SSD_CHUNK), lambda k: (rev_chunk(k), 0, 0))]
                 + [_const_spec(a.shape) for a in small],
        out_specs=[fwd(SSD_INNER), bwd(SSD_INNER)],
        out_shape=[jax.ShapeDtypeStruct((lt, SSD_INNER), F32)] * 2,
        scratch_shapes=[pltpu.VMEM((SSD_STATE, SSD_INNER), F32)] * 2,
        compiler_params=_cparams(("arbitrary",)),
    )(xc, dtr, dtr_t, xc, dtr, dtr_t, *small)


ROWS_PER_TILE = ROW_TILE // GRID_W


def _rope_tables(seq):
    ng = seq // GRID_W
    nctx = CTX_LEN // GRID_W

    def trig(n, count):
        inv = ROPE_THETA ** (-jnp.arange(n, dtype=F32) / n)
        ang = jnp.arange(count, dtype=F32)[:, None] * inv[None, :]
        return jnp.cos(ang), jnp.sin(ang)

    def lay(count, parts):
        return jnp.concatenate([p if hasattr(p, "shape") else jnp.full((count, p[0]), p[1], F32) for p in parts],
                               axis=1)

    def row_slabs(tabs, idents):
        full = [jnp.concatenate([jnp.full((nctx, LANES), ident, F32), t], axis=0) for t, ident in zip(tabs, idents)]
        packed = jnp.concatenate(full, axis=1).reshape(-1, ROWS_PER_TILE, len(tabs) * LANES)
        return jnp.pad(packed, ((0, 0), (0, 8 - ROWS_PER_TILE), (0, 0)))

    n0 = MLA_ROPE // 4
    cr, sr = trig(n0, ng)
    cc, sc = trig(n0, GRID_W)
    tail = LANES - MLA_QK
    row0 = row_slabs([lay(ng, [(MLA_NOPE, 1.0), cr, cr, (2 * n0, 1.0), (tail, 1.0)]),
                      lay(ng, [(MLA_NOPE, 0.0), -sr, sr, (2 * n0, 0.0), (tail, 0.0)])], (1.0, 0.0))
    col0 = jnp.concatenate([lay(GRID_W, [(MLA_NOPE, 1.0), (2 * n0, 1.0), cc, cc, (tail, 1.0)]),
                            lay(GRID_W, [(MLA_NOPE, 0.0), (2 * n0, 0.0), -sc, sc, (tail, 0.0)])], axis=1)
    n1 = DIFF_QK // 4
    cr, sr = trig(n1, ng)
    cc, sc = trig(n1, GRID_W)
    row1 = row_slabs([lay(ng, [cr, cr, (2 * n1, 1.0)] * 2),
                      lay(ng, [-sr, (n1, 0.0), (2 * n1, 0.0)] * 2),
                      lay(ng, [(n1, 0.0), sr, (2 * n1, 0.0)] * 2)], (1.0, 0.0, 0.0))
    col1 = jnp.concatenate([lay(GRID_W, [(2 * n1, 1.0), cc, cc] * 2),
                            lay(GRID_W, [(2 * n1, 0.0), -sc, (n1, 0.0)] * 2),
                            lay(GRID_W, [(2 * n1, 0.0), (n1, 0.0), sc] * 2)], axis=1)
    return row0, col0, row1, col1


def _rope_tile(row_ref, col_ref, k, is_ctx, product):
    r = row_ref[0, :, k * LANES:(k + 1) * LANES]
    c = jnp.where(is_ctx, 1.0 if product else 0.0, col_ref[:, k * LANES:(k + 1) * LANES])
    rb = jnp.concatenate([jnp.broadcast_to(r[j:j + 1], (GRID_W, LANES)) for j in range(ROWS_PER_TILE)], axis=0)
    cb = jnp.concatenate([c] * ROWS_PER_TILE, axis=0)
    return rb * cb if product else rb + cb


def kernel(x, c, ctx, c_ctx, ada_w, ada_b, ffn_w1, ffn_w3, ffn_w2, e_w_in, e_w_out, mla_g_qa, mla_w_qb, mla_g_kva, mla_w_kvb, mla_g_qn, mla_g_kn, s5_lam_re, s5_lam_im, s5_log_dt, s5_b_re, s5_b_im, s5_c_re, s5_c_im, s5_d, s5_w_glu, s5_b_glu, o_w_in, o_w_out, diff_g_q, diff_g_k, diff_lq1, diff_lk1, diff_lq2, diff_lk2, diff_g_o, ssd_conv_w, ssd_conv_b, ssd_dt_bias, ssd_a_log, ssd_d, ssd_g):
    depth = ada_w.shape[0]
    seq = x.shape[1]
    assert x.shape[0] == 1 and ctx.shape[1] == CTX_LEN and seq % ROW_TILE == 0
    lt = seq + CTX_LEN
    xs = None
    mods = _mod_vectors(c, c_ctx, ada_w, ada_b)
    rope_row0, rope_col0, rope_row1, rope_col1 = _rope_tables(seq)
    for i in range(depth):
        j = i // 2
        modx, modc = mods[i, 0], mods[i, 1]
        if xs is None and i % 2 == 1:
            xs = jnp.concatenate([ctx[0], x[0]], axis=0)
        if i % 2 == 0:
            stream = (x[0], ctx[0], 0) if xs is None else (xs, xs, 1)
            q, k, v, u = _even_in(stream, lt, modx, modc, e_w_in[j], mla_g_qa[j], mla_w_qb[j], mla_g_kva[j],
                                  mla_w_kvb[j], mla_g_qn[j], mla_g_kn[j], rope_row0, rope_col0)
            att = _mla_attention(q, k, v)
            mats = _s5_matrices(s5_lam_re[j], s5_lam_im[j], s5_log_dt[j], s5_b_re[j], s5_b_im[j],
                                s5_c_re[j], s5_c_im[j], s5_d[j])
            y = _s5_mixer(u, mats)
            xs = _even_out(stream, att, y, modx, modc, s5_w_glu[j], s5_b_glu[j], e_w_out[j])
        else:
            q, k, v, z, xbc, dtr = _odd_in(xs, modx, modc, o_w_in[j], diff_g_q[j], diff_g_k[j],
                                           rope_row1, rope_col1)
            lam_init = 0.8 - 0.6 * math.exp(-0.3 * i)
            lam = (jnp.exp(jnp.sum(diff_lq1[j] * diff_lk1[j])) - jnp.exp(jnp.sum(diff_lq2[j] * diff_lk2[j]))
                   + lam_init)
            att = _diff_attention(q, k, v, jnp.full((1, LANES), lam, F32), diff_g_o[j].reshape(1, -1),
                                  1.0 - lam_init)
            xc = _ssd_conv(xbc, ssd_conv_w[j], ssd_conv_b[j])
            yf, yr = _ssd_scan(xc, dtr, ssd_dt_bias[j], ssd_a_log[j], ssd_d[j])
            xs = _odd_out(xs, att, yf, yr, z, modx, modc, ssd_g[j], o_w_out[j])
        last = i == depth - 1
        xs = _ffn(xs, modx, modc, ffn_w1[i], ffn_w3[i], ffn_w2[i], latents_only=last)
    return xs[None]
```

```python
import functools
import math

import numpy as np
import jax
import jax.numpy as jnp
from jax import lax
from jax.experimental import pallas as pl
from jax.experimental.pallas import tpu as pltpu

F32 = jnp.float32
BF16 = jnp.bfloat16
HIGHEST = lax.Precision.HIGHEST

D_MODEL = 1024
CTX_LEN = 256
GRID_W = 64
GROUP_WIDTH = 512
D_FF = 2816
EPS = 1e-6
ROPE_THETA = 10000.0
LOG2E = math.log2(math.e)

MLA_HEADS, MLA_NOPE, MLA_ROPE, MLA_V = 8, 64, 32, 64
MLA_QK = MLA_NOPE + MLA_ROPE
MLA_Q_RANK, MLA_KV_RANK = 384, 256
S5_CH, S5_GROUPS, S5_STATE = 16, 32, 64
S5_T = 16
S5_GPB = 8
DIFF_HEADS, DIFF_QK, DIFF_V = 4, 64, 128
SSD_HEADS, SSD_HEAD_DIM, SSD_GROUPS, SSD_STATE, SSD_CONV, SSD_CHUNK = 8, 64, 2, 128, 5, 128
SSD_INNER = 512
SSD_XBC = SSD_INNER + 2 * SSD_GROUPS * SSD_STATE

LANES = 128
ROW_TILE = 256
FFN_TILE = 640
FFN_TILE_LATENT = 512
FFN_CHUNK = 256
KV_UNROLL = 4
KV_UNROLL_FULL = 12
VMEM_LIMIT = 56 * 1024 * 1024


def _cparams(sem):
    return pltpu.CompilerParams(dimension_semantics=sem, vmem_limit_bytes=VMEM_LIMIT)


def _dot(a, b):
    return jnp.dot(a, b, preferred_element_type=F32)


def _split_dot(x, w, parts):
    acc = None
    for _ in range(parts):
        piece = x.astype(BF16)
        term = _dot(piece, w)
        acc = term if acc is None else acc + term
        x = x - piece.astype(F32)
    return acc


def _rms(x):
    return x * lax.rsqrt(jnp.mean(x * x, axis=-1, keepdims=True) + EPS)


def _sigmoid(x):
    return 1.0 / (1.0 + jnp.exp(-x))


def _silu(x):
    return x * _sigmoid(x)


def _gelu_tanh(x):
    return 0.5 * x * (1.0 + jnp.tanh(math.sqrt(2.0 / math.pi) * (x + 0.044715 * (x * x * x))))


def _softplus(x):
    return jnp.maximum(x, 0.0) + jnp.log(1.0 + jnp.exp(-jnp.abs(x)))


def _const_spec(shape):
    nd = len(shape)
    return pl.BlockSpec(shape, lambda *_: (0,) * nd)


def _mod_kernel(c_ref, w_ref, b_ref, o_ref):
    s = _silu(c_ref[...]).astype(BF16)
    o_ref[0] = _dot(s, w_ref[0].astype(BF16)) + b_ref[0]


def _mod_vectors(c, c_ctx, ada_w, ada_b):
    depth = ada_w.shape[0]
    cc = jnp.zeros((8, D_MODEL), F32).at[0].set(c[0]).at[1].set(c_ctx)
    nblk = 6
    out = pl.pallas_call(
        _mod_kernel,
        grid=(depth, nblk),
        in_specs=[
            pl.BlockSpec((8, D_MODEL), lambda i, j: (0, 0)),
            pl.BlockSpec((1, D_MODEL, D_MODEL), lambda i, j: (i, 0, j)),
            pl.BlockSpec((1, 1, D_MODEL), lambda i, j: (i, 0, j)),
        ],
        out_specs=pl.BlockSpec((1, 8, D_MODEL), lambda i, j: (i, 0, j)),
        out_shape=jax.ShapeDtypeStruct((depth, 8, 6 * D_MODEL), F32),
        compiler_params=_cparams(("arbitrary", "arbitrary")),
    )(cc, ada_w, ada_b.reshape(depth, 1, 6 * D_MODEL))
    mods = out[:, :2].reshape(depth, 2, 6, D_MODEL)
    return jnp.pad(mods, ((0, 0), (0, 0), (0, 2), (0, 0)))


def _mod_rows(modx_ref, modc_ref, k, row0, nrows):
    rows = row0 + lax.broadcasted_iota(jnp.int32, (nrows, 1), 0)
    return jnp.where(rows < CTX_LEN, modc_ref[k:k + 1, :], modx_ref[k:k + 1, :])


def _rope_partner(n_half):
    return np.array([i + n_half if i < n_half else i - n_half for i in range(2 * n_half)])


def _stream_specs(stream):
    lat, crows, first = stream
    specs = [pl.BlockSpec((ROW_TILE, D_MODEL), lambda i: (jnp.maximum(i - 1 + first, 0), 0)),
             pl.BlockSpec((ROW_TILE, D_MODEL), lambda i: (0, 0))]
    return [lat, crows], specs


def _even_in_kernel(x_ref, c_ref, modx_ref, modc_ref, win_ref, gqa_ref, wq_ref, wqs_ref, gkva_ref, wk_ref,
                    wv_ref, gq_ref, gqs_ref, gk_ref, gks_ref, rrow_ref, rcol_ref,
                    q_ref, k_ref, v_ref, u_ref):
    i = pl.program_id(0)
    is_ctx = i == 0
    sh = jnp.where(is_ctx, modc_ref[0:1, :], modx_ref[0:1, :])
    sc = jnp.where(is_ctx, modc_ref[1:2, :], modx_ref[1:2, :])
    xt = jnp.where(is_ctx, c_ref[...], x_ref[...])
    h = (_rms(xt) * (1.0 + sc) + sh).astype(BF16)
    proj = _dot(h, win_ref[...])
    cq = proj[:, 0:384]
    ckv = proj[:, 384:640]
    u_ref[...] = proj[:, 640:1152]
    krb = proj[:, 1152:1280]
    krs = proj[:, 1280:1408]
    cqn = (_rms(cq) * gqa_ref[...]).astype(BF16)
    qf = _dot(cqn, wq_ref[...])
    qs = _dot(cqn, wqs_ref[...])
    ckvn = (_rms(ckv) * gkva_ref[...]).astype(BF16)
    kf = _dot(ckvn, wk_ref[...])
    vf = _dot(ckvn, wv_ref[...])
    lane = lax.broadcasted_iota(jnp.int32, (1, LANES), 1)
    for pr in range(MLA_HEADS // 2):
        vp = vf[:, pr * LANES:(pr + 1) * LANES]
        v_ref[:, (2 * pr) * LANES:(2 * pr + 1) * LANES] = jnp.where(lane < MLA_V, vp, 1.0).astype(BF16)
        v_ref[:, (2 * pr + 1) * LANES:(2 * pr + 2) * LANES] = jnp.where(lane < MLA_V, 1.0, vp).astype(BF16)
    cos = _rope_tile(rrow_ref, rcol_ref, 0, is_ctx, True)
    sin = _rope_tile(rrow_ref, rcol_ref, 1, is_ctx, False)
    qscale = MLA_QK ** -0.5 * LOG2E
    inv_n = 1.0 / MLA_QK
    for hd in range(MLA_HEADS):
        blk = slice(hd * LANES, (hd + 1) * LANES)
        qh = qf[:, blk]
        rq = lax.rsqrt(jnp.sum(qh * qh, axis=-1, keepdims=True) * inv_n + EPS) * qscale
        qo = (qh * (gq_ref[...] * cos) + qs[:, blk] * (gqs_ref[...] * sin)) * rq
        q_ref[:, blk] = qo.astype(BF16)
        kh = kf[:, blk] + krb
        rk = lax.rsqrt(jnp.sum(kh * kh, axis=-1, keepdims=True) * inv_n + EPS)
        ko = (kh * (gk_ref[...] * cos) + krs * (gks_ref[...] * sin)) * rk
        k_ref[:, blk] = ko.astype(BF16)


def _even_in(stream, lt, modx, modc, w_in, g_qa, w_qb, g_kva, w_kvb, g_qn, g_kn, rope_row, rope_col):
    perm = _rope_partner(MLA_ROPE // 4)
    perm = np.concatenate([perm, perm + MLA_ROPE // 2])
    cq_w, ckv_w = w_in[:, :384], w_in[:, 384:640]
    kr_w, u_w = w_in[:, 640:672], w_in[:, 672:]
    zblk = jnp.zeros((D_MODEL, LANES), F32)
    krblk = zblk.at[:, MLA_NOPE:MLA_QK].set(kr_w)
    krsblk = zblk.at[:, MLA_NOPE:MLA_QK].set(kr_w[:, perm])
    win = jnp.concatenate([cq_w, ckv_w, u_w, krblk, krsblk], axis=1).astype(BF16)
    wq3 = w_qb.reshape(MLA_Q_RANK, MLA_HEADS, MLA_QK)
    wq = jnp.zeros((MLA_Q_RANK, MLA_HEADS, LANES), F32).at[:, :, :MLA_QK].set(wq3)
    wqs = jnp.zeros((MLA_Q_RANK, MLA_HEADS, LANES), F32).at[:, :, MLA_NOPE:MLA_QK].set(
        wq3[:, :, MLA_NOPE:][:, :, perm])
    wkv3 = w_kvb.reshape(MLA_KV_RANK, MLA_HEADS, MLA_NOPE + MLA_V)
    wk = jnp.zeros((MLA_KV_RANK, MLA_HEADS, LANES), F32).at[:, :, :MLA_NOPE].set(wkv3[:, :, :MLA_NOPE])
    wv = wkv3[:, :, MLA_NOPE:].reshape(MLA_KV_RANK, MLA_HEADS * MLA_V)

    def pad_gain(g):
        gp = jnp.zeros((1, LANES), F32).at[0, :MLA_QK].set(g)
        gs = jnp.zeros((1, LANES), F32).at[0, MLA_NOPE:MLA_QK].set(g[MLA_NOPE:][perm])
        return gp, gs

    gq, gqs = pad_gain(g_qn)
    gk, gks = pad_gain(g_kn)
    hw = MLA_HEADS * LANES
    consts = (modx, modc, win, g_qa.reshape(1, -1), wq.reshape(MLA_Q_RANK, hw).astype(BF16),
              wqs.reshape(MLA_Q_RANK, hw).astype(BF16), g_kva.reshape(1, -1),
              wk.reshape(MLA_KV_RANK, hw).astype(BF16), wv.astype(BF16), gq, gqs, gk, gks)
    row = lambda w: pl.BlockSpec((ROW_TILE, w), lambda i: (i, 0))
    rows, row_specs = _stream_specs(stream)
    args = (*rows, *consts, rope_row, rope_col)
    in_specs = (row_specs + [_const_spec(a.shape) for a in consts]
                + [pl.BlockSpec((1,) + rope_row.shape[1:], lambda i: (i, 0, 0)), _const_spec(rope_col.shape)])
    return pl.pallas_call(
        _even_in_kernel,
        grid=(lt // ROW_TILE,),
        in_specs=in_specs,
        out_specs=[row(hw), row(hw), row(hw), row(GROUP_WIDTH)],
        out_shape=[jax.ShapeDtypeStruct((lt, hw), BF16), jax.ShapeDtypeStruct((lt, hw), BF16),
                   jax.ShapeDtypeStruct((lt, hw), BF16),
                   jax.ShapeDtypeStruct((lt, GROUP_WIDTH), F32)],
        compiler_params=_cparams(("arbitrary",)),
    )(*args)


def _kv_chunk(lt):
    for tk in (1280, 640, 256):
        if lt % tk == 0:
            return tk
    raise ValueError(lt)


def _scores(q, k_ref, start, size):
    kc = k_ref[pl.ds(start, size), :]
    return lax.dot_general(q, kc, (((1,), (1,)), ((), ())), preferred_element_type=F32)


def _softmax_update(s_ref, v_ref, start, size, carry, row_max=None):
    m, acc = carry
    if row_max is None:
        row_max = jnp.max(s_ref[...], axis=-1, keepdims=True)
    mn = jnp.maximum(m, row_max)
    alpha = jnp.exp2(m - mn)
    p = jnp.exp2(s_ref[...] - mn)
    acc = alpha * acc + _dot(p.astype(BF16), v_ref[pl.ds(start, size), :])
    return mn, acc


def _attend(streams, s_scr, is_ctx, lt, tk, unroll=KV_UNROLL):
    tq = streams[0][0].shape[0]
    n = lt // tk
    init = tuple((jnp.full((tq, 1), -jnp.inf, F32), jnp.zeros((tq, v_ref.shape[-1]), F32), jnp.zeros((tq, 1), F32))
                 for _, _, _, v_ref in streams)
    slot_of = lambda chunk: 2 if chunk == 0 else chunk % 2

    def issue_next(t, q_next, k_ref):
        s_scr[3 * t + 2] = _scores(q_next, k_ref, 0, tk)

    def consume(chunk, slot, carries, last):
        start = chunk * tk
        if not isinstance(start, int):
            start = pl.multiple_of(start, tk)
        out = []
        for t, ((q, q_next, k_ref, v_ref), (m, acc, ahead_max)) in enumerate(zip(streams, carries)):
            here_max = None if slot == 2 else ahead_max
            if last and slot != 2:
                issue_next(t, q_next, k_ref)
            elif not last:
                ahead = _scores(q, k_ref, start + tk, tk)
                s_scr[3 * t + (1 - slot if slot < 2 else 1)] = ahead
                ahead_max = jnp.max(ahead, axis=-1, keepdims=True)
            out.append(_softmax_update(s_scr.at[3 * t + slot], v_ref, start, tk, (m, acc), here_max) + (ahead_max,))
            if last and slot == 2:
                issue_next(t, q_next, k_ref)
        return tuple(out)

    def group(i, carries):
        for u in range(unroll):
            carries = consume(1 + unroll * i + u, (1 + u) % 2, carries, False)
        return carries

    def ctx_branch():
        out = []
        for t, ((q, q_next, k_ref, v_ref), (m, acc, ahead_max)) in enumerate(zip(streams, init)):
            issue_next(t, q_next, k_ref)
            s_scr[3 * t, :, 0:CTX_LEN] = _scores(q, k_ref, 0, CTX_LEN)
            out.append(_softmax_update(s_scr.at[3 * t, :, 0:CTX_LEN], v_ref, 0, CTX_LEN, (m, acc)) + (ahead_max,))
        return tuple(out)

    def full_branch():
        carries = consume(0, 2, init, n == 1)
        ngroups = max(n - 2, 0) // unroll
        if ngroups:
            carries = lax.fori_loop(0, ngroups, group, carries)
        for chunk in range(1 + unroll * ngroups, n):
            carries = consume(chunk, slot_of(chunk), carries, chunk == n - 1)
        return carries

    out = lax.cond(is_ctx, ctx_branch, full_branch)
    return [acc for (_, acc, _) in out]


def _mla_attn_kernel(q_ref, qn_ref, k_ref, v_ref, o_ref, s_scr, *, lt, tk):
    is_ctx = pl.program_id(1) == 0
    lane = lax.broadcasted_iota(jnp.int32, (1, LANES), 1)
    blks = [slice(hh * LANES, (hh + 1) * LANES) for hh in range(2)]
    acc_a, acc_b = _attend([(q_ref[:, b], qn_ref[:, b], k_ref.at[:, b], v_ref.at[:, b]) for b in blks],
                           s_scr, is_ctx, lt, tk, unroll=KV_UNROLL_FULL)
    o_ref[...] = jnp.where(lane < MLA_V, acc_a / acc_a[:, MLA_V:MLA_V + 1], acc_b / acc_b[:, 0:1])


def _mla_attention(q, k, v):
    lt = q.shape[0]
    tk = _kv_chunk(lt)
    npair = MLA_HEADS // 2
    last = lt // ROW_TILE - 1
    return pl.pallas_call(
        functools.partial(_mla_attn_kernel, lt=lt, tk=tk),
        grid=(npair, lt // ROW_TILE),
        in_specs=[
            pl.BlockSpec((ROW_TILE, 2 * LANES), lambda p, i: (i, p)),
            pl.BlockSpec((ROW_TILE, 2 * LANES), lambda p, i: (jnp.minimum(i + 1, last), p)),
            pl.BlockSpec((lt, 2 * LANES), lambda p, i: (0, p), pipeline_mode=pl.Buffered(1)),
            pl.BlockSpec((lt, 2 * LANES), lambda p, i: (0, p), pipeline_mode=pl.Buffered(1)),
        ],
        out_specs=pl.BlockSpec((ROW_TILE, LANES), lambda p, i: (i, p)),
        out_shape=jax.ShapeDtypeStruct((lt, GROUP_WIDTH), F32),
        scratch_shapes=[pltpu.VMEM((6, ROW_TILE, tk), F32)],
        compiler_params=_cparams(("arbitrary", "arbitrary")),
    )(q, q, k, v)


def _diff_attn_kernel(q_ref, qn_ref, k_ref, v_ref, lam_ref, go_ref, o_ref, s_scr, *, lt, tk, out_scale):
    is_ctx = pl.program_id(1) == 0
    lane = lax.broadcasted_iota(jnp.int32, (1, LANES), 1)
    zero = jnp.zeros((ROW_TILE, LANES), BF16)
    first = lambda q: jnp.where(lane < DIFF_QK, q, zero)
    second = lambda q: jnp.where(lane < DIFF_QK, zero, q)
    q, qn = q_ref[...], qn_ref[...]
    a1, a2 = _attend([(first(q), first(qn), k_ref, v_ref), (second(q), second(qn), k_ref, v_ref)],
                     s_scr, is_ctx, lt, tk, unroll=KV_UNROLL_FULL)
    o = (a1[:, 0:DIFF_V] / a1[:, DIFF_V:DIFF_V + 1]
         - lam_ref[...] * (a2[:, 0:DIFF_V] / a2[:, DIFF_V:DIFF_V + 1]))
    o_ref[...] = _rms(o) * go_ref[...] * out_scale


def _diff_attention(q, k, v, lam, g_o, out_scale):
    lt = q.shape[0]
    tk = _kv_chunk(lt)
    last = lt // ROW_TILE - 1
    return pl.pallas_call(
        functools.partial(_diff_attn_kernel, lt=lt, tk=tk, out_scale=out_scale),
        grid=(DIFF_HEADS, lt // ROW_TILE),
        in_specs=[
            pl.BlockSpec((ROW_TILE, LANES), lambda h, i: (i, h)),
            pl.BlockSpec((ROW_TILE, LANES), lambda h, i: (jnp.minimum(i + 1, last), h)),
            pl.BlockSpec((lt, LANES), lambda h, i: (0, h), pipeline_mode=pl.Buffered(1)),
            pl.BlockSpec((lt, 2 * LANES), lambda h, i: (0, h), pipeline_mode=pl.Buffered(1)),
            _const_spec((1, LANES)),
            _const_spec((1, LANES)),
        ],
        out_specs=pl.BlockSpec((ROW_TILE, LANES), lambda h, i: (i, h)),
        out_shape=jax.ShapeDtypeStruct((lt, GROUP_WIDTH), F32),
        scratch_shapes=[pltpu.VMEM((6, ROW_TILE, tk), F32)],
        compiler_params=_cparams(("arbitrary", "arbitrary")),
    )(q, q, k, v, lam, g_o)


def _s5_matrices(lam_re, lam_im, log_dt, b_re, b_im, c_re, c_im, d_skip):
    t = S5_T
    g, n, ch = S5_GROUPS, S5_STATE, S5_CH
    dt = jnp.exp(log_dt)[:, :, None, None]
    tau = jnp.arange(t + 1, dtype=F32)
    mag = jnp.exp(lam_re[..., None] * dt * tau)
    ang = lam_im[..., None] * dt * tau
    p_re, p_im = mag * jnp.cos(ang), mag * jnp.sin(ang)
    a_re, a_im = p_re[..., 1], p_im[..., 1]
    den = lam_re * lam_re + lam_im * lam_im
    k_re = ((a_re - 1.0) * lam_re + a_im * lam_im) / den
    k_im = (a_im * lam_re - (a_re - 1.0) * lam_im) / den
    bb_re = k_re[..., None] * b_re - k_im[..., None] * b_im
    bb_im = k_re[..., None] * b_im + k_im[..., None] * b_re
    ct_re = c_re.transpose(0, 1, 3, 2)[:, :, :, None, :]
    ct_im = c_im.transpose(0, 1, 3, 2)[:, :, :, None, :]
    ca_re = ct_re * p_re[..., None] - ct_im * p_im[..., None]
    ca_im = ct_re * p_im[..., None] + ct_im * p_re[..., None]
    bt_re = bb_re.transpose(0, 1, 3, 2)[..., None, None]
    bt_im = bb_im.transpose(0, 1, 3, 2)[..., None, None]
    kk = jnp.sum(bt_re * ca_re[:, :, None] - bt_im * ca_im[:, :, None], axis=3)
    strip = jnp.concatenate([kk[1][:, :, t - 1:0:-1], kk[0][:, :, 0:1] + kk[1][:, :, 0:1], kk[0][:, :, 1:t]],
                            axis=2).reshape(g, ch, (2 * t - 1) * ch)
    nb, gp = g // S5_GPB, S5_GPB
    wide = gp * t * ch
    strip = strip.reshape(g, ch, 2 * t - 1, ch)
    tiled = jnp.tile(jnp.pad(strip, ((0, 0), (0, 0), (0, 1), (0, 0))), (1, 1, t, 1))[:, :, :t * (2 * t - 1)]
    mc = tiled.reshape(g, ch, t, 2 * t - 1, ch)[:, :, :, t - 1:]
    mc = mc.reshape(nb, gp, ch, t, t * ch).transpose(0, 3, 1, 2, 4).reshape(nb, wide, t * ch)

    def w_of(d, reverse):
        pr = p_re[d][:, :, 0:t].transpose(0, 2, 1)[:, :, None, :]
        pi = p_im[d][:, :, 0:t].transpose(0, 2, 1)[:, :, None, :]
        br = bb_re[d].transpose(0, 2, 1)[:, None]
        bi = bb_im[d].transpose(0, 2, 1)[:, None]
        out = [pr * br - pi * bi, pr * bi + pi * br]
        return [jnp.flip(o, axis=1) for o in out] if reverse else out

    wc = jnp.concatenate(w_of(0, True) + w_of(1, False), axis=-1)
    wc = wc.reshape(nb, gp, t, ch, 4 * n).transpose(0, 2, 1, 3, 4).reshape(nb, wide, 4 * n)
    v4 = jnp.stack([ca_re[0][:, :, 1:t + 1], -ca_im[0][:, :, 1:t + 1],
                    ca_re[1][:, :, t:0:-1], -ca_im[1][:, :, t:0:-1]], axis=0)
    vc = v4.reshape(4, nb, gp, n, t * ch).transpose(1, 0, 2, 3, 4).reshape(nb, 4 * gp * n, t * ch)
    m_big = _s5_expand(mc.astype(BF16), row_bits=4, col_bits=4)
    w_big = _s5_expand(wc.astype(BF16), row_bits=4, col_bits=6)
    v_big = _s5_expand(vc.astype(BF16), row_bits=6, col_bits=4)
    a16 = jnp.stack([p_re[0][..., t], p_im[0][..., t], p_re[1][..., t], p_im[1][..., t]], axis=0)
    a16 = a16.reshape(4, nb, gp * n).transpose(1, 0, 2)
    a16 = jnp.pad(a16, ((0, 0), (0, 4), (0, 0)))
    return m_big, v_big, w_big, a16, d_skip.reshape(nb, 1, gp * ch)


def _s5_expand_kernel(c_ref, o_ref, *, row_bits, col_bits):
    nrow, ncol = o_ref.shape[1], o_ref.shape[2]
    kc = c_ref.shape[2]
    gbits = S5_GPB.bit_length() - 1
    r = lax.broadcasted_iota(jnp.int32, (kc, ncol), 0)
    c = lax.broadcasted_iota(jnp.int32, (kc, ncol), 1)
    bmask = (1 << col_bits) - 1
    same = jnp.logical_and((r >> col_bits) == (c >> (col_bits + gbits)), (r & bmask) == (c & bmask))
    tiling = jnp.where(same, 1.0, 0.0).astype(BF16)
    slab = S5_GPB << row_bits
    rr = lax.broadcasted_iota(jnp.int32, (slab, ncol), 0)
    cc = lax.broadcasted_iota(jnp.int32, (slab, ncol), 1)
    keep = ((rr >> row_bits) & (S5_GPB - 1)) == ((cc >> col_bits) & (S5_GPB - 1))
    for k in range(nrow // slab):
        rows = slice(k * slab, (k + 1) * slab)
        o_ref[0, rows, :] = jnp.where(keep, _dot(c_ref[0, rows, :], tiling), 0.0).astype(BF16)


def _s5_expand(compact, row_bits, col_bits):
    nb, nrow, kc = compact.shape
    ncol = kc * S5_GPB
    return pl.pallas_call(
        functools.partial(_s5_expand_kernel, row_bits=row_bits, col_bits=col_bits),
        grid=(nb,),
        in_specs=[pl.BlockSpec((1, nrow, kc), lambda b: (b, 0, 0))],
        out_specs=pl.BlockSpec((1, nrow, ncol), lambda b: (b, 0, 0)),
        out_shape=jax.ShapeDtypeStruct((nb, nrow, ncol), BF16),
        compiler_params=_cparams(("arbitrary",)),
    )(compact)


def _s5_gather_kernel(u_ref, wz_ref, ub_ref, z_ref):
    nrow = ub_ref.shape[1]
    for j in range(S5_T):
        ub_ref[0, :, j * LANES:(j + 1) * LANES] = u_ref[pl.ds(j, nrow, stride=S5_T), :].astype(BF16)
    z_ref[0] = _dot(ub_ref[0], wz_ref[0])


def _s5_scan_kernel(z_ref, a_ref, s_ref, *, nblk, nctx):
    w = S5_GPB * S5_STATE
    cols = [slice(k * w, (k + 1) * w) for k in range(4)]
    a = a_ref[0]
    afr, afi, arr, ari = a[0:1], a[1:2], a[2:3], a[3:4]

    def advance(rows_f, rows_r, carry):
        fr, fi, rr, ri = carry
        for col, val, rows in zip(cols, carry, (rows_f, rows_f, rows_r, rows_r)):
            s_ref[0, rows, col] = val
        zfr, zfi = z_ref[0, rows_f, cols[0]], z_ref[0, rows_f, cols[1]]
        zrr, zri = z_ref[0, rows_r, cols[2]], z_ref[0, rows_r, cols[3]]
        return (afr * fr - afi * fi + zfr, afr * fi + afi * fr + zfi,
                arr * rr - ari * ri + zrr, arr * ri + ari * rr + zri)

    zero = jnp.zeros((1, w), F32)
    carry = lax.fori_loop(0, nctx, lambda k, c: advance(pl.ds(k, 1), pl.ds(nctx - 1 - k, 1), c), (zero,) * 4)
    lax.fori_loop(0, nblk - nctx, lambda k, c: advance(pl.ds(nctx + k, 1), pl.ds(nblk - 1 - k, 1), c), carry)


def _s5_out_kernel(ub_ref, s_ref, u_ref, m_ref, v_ref, d_ref, y_ref):
    nrow = ub_ref.shape[1]
    yb = _dot(ub_ref[0], m_ref[0]) + _dot(s_ref[0].astype(BF16), v_ref[0])
    for i in range(S5_T):
        rows = pl.ds(i, nrow, stride=S5_T)
        y_ref[rows, :] = yb[:, i * LANES:(i + 1) * LANES] + u_ref[rows, :] * d_ref[0]


def _s5_mixer(u, mats):
    lt = u.shape[0]
    nblk = lt // S5_T
    nb = S5_GROUPS // S5_GPB
    wide = S5_GPB * S5_T * S5_CH
    swide = 4 * S5_GPB * S5_STATE
    m_big, v_big, wz, a16, dvec = mats
    tb = max(c for c in range(16, 257, 16) if nblk % c == 0)
    ntile = nblk // tb
    tok = pl.BlockSpec((tb * S5_T, LANES), lambda b, r: (r, b))
    blk = lambda w: pl.BlockSpec((1, tb, w), lambda b, r: (b, r, 0))
    per_b = lambda shape: pl.BlockSpec((1,) + shape, lambda b, r: (b, 0, 0))
    ub, z = pl.pallas_call(
        _s5_gather_kernel,
        grid=(nb, ntile),
        in_specs=[tok, per_b(wz.shape[1:])],
        out_specs=[blk(wide), blk(swide)],
        out_shape=[jax.ShapeDtypeStruct((nb, nblk, wide), BF16), jax.ShapeDtypeStruct((nb, nblk, swide), F32)],
        compiler_params=_cparams(("arbitrary", "arbitrary")),
    )(u, wz)
    s = pl.pallas_call(
        functools.partial(_s5_scan_kernel, nblk=nblk, nctx=CTX_LEN // S5_T),
        grid=(nb,),
        in_specs=[pl.BlockSpec((1, nblk, swide), lambda b: (b, 0, 0)),
                  pl.BlockSpec((1,) + a16.shape[1:], lambda b: (b, 0, 0))],
        out_specs=pl.BlockSpec((1, nblk, swide), lambda b: (b, 0, 0)),
        out_shape=jax.ShapeDtypeStruct((nb, nblk, swide), F32),
        compiler_params=_cparams(("arbitrary",)),
    )(z, a16)
    return pl.pallas_call(
        _s5_out_kernel,
        grid=(nb, ntile),
        in_specs=[blk(wide), blk(swide), tok,
                  pl.BlockSpec((1,) + m_big.shape[1:], lambda b, r: (b, 0, 0), pipeline_mode=pl.Buffered(1)),
                  pl.BlockSpec((1,) + v_big.shape[1:], lambda b, r: (b, 0, 0), pipeline_mode=pl.Buffered(1)),
                  per_b((1, LANES))],
        out_specs=tok,
        out_shape=jax.ShapeDtypeStruct((lt, GROUP_WIDTH), F32),
        compiler_params=_cparams(("arbitrary", "arbitrary")),
    )(ub, s, u, m_big, v_big, dvec)


def _even_out_kernel(x_ref, c_ref, att_ref, y_ref, modx_ref, modc_ref, wglu_ref, bglu_ref, wout_ref, o_ref):
    is_ctx = pl.program_id(0) == 0
    gate_a = jnp.where(is_ctx, modc_ref[2:3, :], modx_ref[2:3, :])
    g = _gelu_tanh(y_ref[...])
    ssm = g * _sigmoid(_dot(g.astype(BF16), wglu_ref[...]) + bglu_ref[...])
    mix = (_dot(att_ref[...].astype(BF16), wout_ref[0:GROUP_WIDTH, :])
           + _dot(ssm.astype(BF16), wout_ref[GROUP_WIDTH:, :]))
    o_ref[...] = jnp.where(is_ctx, c_ref[...], x_ref[...]) + gate_a * mix


def _even_out(stream, att, y, modx, modc, w_glu, b_glu, w_out):
    lt = att.shape[0]
    row = lambda w: pl.BlockSpec((ROW_TILE, w), lambda i: (i, 0))
    rows, row_specs = _stream_specs(stream)
    return pl.pallas_call(
        _even_out_kernel,
        grid=(lt // ROW_TILE,),
        in_specs=row_specs + [row(GROUP_WIDTH), row(GROUP_WIDTH), _const_spec(modx.shape),
                              _const_spec(modc.shape), _const_spec(w_glu.shape), _const_spec((1, GROUP_WIDTH)),
                              _const_spec(w_out.shape)],
        out_specs=row(D_MODEL),
        out_shape=jax.ShapeDtypeStruct((lt, D_MODEL), F32),
        compiler_params=_cparams(("arbitrary",)),
    )(*rows, att, y, modx, modc, w_glu.astype(BF16), b_glu.reshape(1, -1), w_out.astype(BF16))


def _odd_out_kernel(x_ref, att_ref, yf_ref, yr_ref, z_ref, modx_ref, modc_ref, gn_ref, wout_ref, o_ref):
    is_ctx = pl.program_id(0) == 0
    gate_a = jnp.where(is_ctx, modc_ref[2:3, :], modx_ref[2:3, :])
    gy = (yf_ref[...] + yr_ref[...]) * _silu(z_ref[...])
    gw = SSD_INNER // SSD_GROUPS
    parts = [_rms(gy[:, j * gw:(j + 1) * gw]) for j in range(SSD_GROUPS)]
    ssm = jnp.concatenate(parts, axis=-1) * gn_ref[...]
    mix = (_dot(att_ref[...].astype(BF16), wout_ref[0:GROUP_WIDTH, :])
           + _dot(ssm.astype(BF16), wout_ref[GROUP_WIDTH:, :]))
    o_ref[...] = x_ref[...] + gate_a * mix


def _odd_out(xs, att, yf, yr, z, modx, modc, g_norm, w_out):
    lt = xs.shape[0]
    row = lambda w: pl.BlockSpec((ROW_TILE, w), lambda i: (i, 0))
    return pl.pallas_call(
        _odd_out_kernel,
        grid=(lt // ROW_TILE,),
        in_specs=[row(D_MODEL), row(GROUP_WIDTH), row(SSD_INNER), row(SSD_INNER), row(SSD_INNER),
                  _const_spec(modx.shape), _const_spec(modc.shape), _const_spec((1, SSD_INNER)),
                  _const_spec(w_out.shape)],
        out_specs=row(D_MODEL),
        out_shape=jax.ShapeDtypeStruct((lt, D_MODEL), F32),
        compiler_params=_cparams(("arbitrary",)),
    )(xs, att, yf, yr, z, modx, modc, g_norm.reshape(1, -1), w_out.astype(BF16))


def _ffn_kernel(x_ref, modx_ref, modc_ref, w1_ref, w3_ref, w2_ref, o_ref, *, row_offset):
    tm = x_ref.shape[0]
    row0 = row_offset + pl.program_id(0) * tm
    x = x_ref[...]
    sh = _mod_rows(modx_ref, modc_ref, 3, row0, tm)
    sc = _mod_rows(modx_ref, modc_ref, 4, row0, tm)
    gate = _mod_rows(modx_ref, modc_ref, 5, row0, tm)
    h = (_rms(x) * (1.0 + sc) + sh).astype(BF16)
    acc = jnp.zeros((tm, D_MODEL), F32)
    for c in range(D_FF // FFN_CHUNK):
        blk = slice(c * FFN_CHUNK, (c + 1) * FFN_CHUNK)
        a = _dot(h, w1_ref[:, blk])
        b = _dot(h, w3_ref[:, blk])
        acc = acc + _dot((_silu(a) * b).astype(BF16), w2_ref[blk, :])
    o_ref[...] = x + gate * acc


def _ffn(xs, modx, modc, w1, w3, w2, latents_only=False):
    lt = xs.shape[0]
    if latents_only:
        nrows = lt - CTX_LEN
        tm = FFN_TILE_LATENT if nrows % FFN_TILE_LATENT == 0 else ROW_TILE
        in_row = pl.BlockSpec((pl.Element(tm), pl.Element(D_MODEL)), lambda i: (pl.multiple_of(CTX_LEN + i * tm, ROW_TILE), 0))
    else:
        nrows = lt
        tm = FFN_TILE if lt % FFN_TILE == 0 else ROW_TILE
        in_row = pl.BlockSpec((tm, D_MODEL), lambda i: (i, 0))
    single = lambda shape: pl.BlockSpec(shape, lambda i: (0, 0), pipeline_mode=pl.Buffered(1))
    return pl.pallas_call(
        functools.partial(_ffn_kernel, row_offset=lt - nrows),
        grid=(nrows // tm,),
        in_specs=[in_row, _const_spec(modx.shape), _const_spec(modc.shape), single(w1.shape),
                  single(w3.shape), single(w2.shape)],
        out_specs=pl.BlockSpec((tm, D_MODEL), lambda i: (i, 0)),
        out_shape=jax.ShapeDtypeStruct((nrows, D_MODEL), F32),
        compiler_params=_cparams(("arbitrary",)),
    )(xs, modx, modc, w1.astype(BF16), w3.astype(BF16), w2.astype(BF16))


ODD_COLS = 2048 + SSD_XBC + LANES


def _odd_in_kernel(x_ref, modx_ref, modc_ref, win_ref, gq_ref, gk_ref, ee_ref, rrow_ref, rcol_ref,
                   q_ref, k_ref, v_ref, z_ref, xbc_ref, dt_ref):
    i = pl.program_id(0)
    is_ctx = i == 0
    sh = jnp.where(is_ctx, modc_ref[0:1, :], modx_ref[0:1, :])
    sc = jnp.where(is_ctx, modc_ref[1:2, :], modx_ref[1:2, :])
    h = (_rms(x_ref[...]) * (1.0 + sc) + sh).astype(BF16)
    proj = _dot(h, win_ref[...])
    ones = jnp.ones((ROW_TILE, DIFF_V), BF16)
    for hd in range(DIFF_HEADS):
        v_ref[:, 2 * hd * DIFF_V:(2 * hd + 1) * DIFF_V] = proj[:, 1024 + hd * DIFF_V:1024 + (hd + 1) * DIFF_V].astype(BF16)
        v_ref[:, (2 * hd + 1) * DIFF_V:(2 * hd + 2) * DIFF_V] = ones
    z_ref[...] = proj[:, 1536:2048]
    xbc_ref[...] = proj[:, 2048:2048 + SSD_XBC]
    dt_ref[...] = proj[:, 2048 + SSD_XBC:ODD_COLS]
    nrep = GROUP_WIDTH // LANES
    cos = jnp.concatenate([_rope_tile(rrow_ref, rcol_ref, 0, is_ctx, True)] * nrep, axis=-1)
    sina = jnp.concatenate([_rope_tile(rrow_ref, rcol_ref, 1, is_ctx, False)] * nrep, axis=-1)
    sinb = jnp.concatenate([_rope_tile(rrow_ref, rcol_ref, 2, is_ctx, False)] * nrep, axis=-1)
    half = DIFF_QK // 4
    qscale = DIFF_QK ** -0.5 * LOG2E

    def prep(t, g_ref, scale):
        ss = _split_dot(t * t, ee_ref[...], 2)
        tn = t * lax.rsqrt(ss * (1.0 / DIFF_QK) + EPS) * g_ref[...]
        up = pltpu.roll(tn, GROUP_WIDTH - half, axis=1)
        dn = pltpu.roll(tn, half, axis=1)
        return ((tn * cos + up * sina + dn * sinb) * scale).astype(BF16)

    q_ref[...] = prep(proj[:, 0:512], gq_ref, qscale)
    k_ref[...] = prep(proj[:, 512:1024], gk_ref, 1.0)


def _odd_in(xs, modx, modc, w_in, g_q, g_k, rope_row, rope_col):
    lt = xs.shape[0]
    win = jnp.pad(w_in, ((0, 0), (0, ODD_COLS - w_in.shape[1]))).astype(BF16)
    nblk = GROUP_WIDTH // DIFF_QK
    blk_id = np.arange(GROUP_WIDTH) // DIFF_QK
    ee = jnp.asarray(blk_id[:, None] == blk_id[None, :], BF16)
    gq = jnp.tile(g_q, nblk).reshape(1, -1)
    gk = jnp.tile(g_k, nblk).reshape(1, -1)
    args = (xs, modx, modc, win, gq, gk, ee, rope_row, rope_col)
    row = lambda w: pl.BlockSpec((ROW_TILE, w), lambda i: (i, 0))
    in_specs = ([row(D_MODEL)] + [_const_spec(a.shape) for a in args[1:7]]
                + [pl.BlockSpec((1,) + rope_row.shape[1:], lambda i: (i, 0, 0)), _const_spec(rope_col.shape)])
    return pl.pallas_call(
        _odd_in_kernel,
        grid=(lt // ROW_TILE,),
        in_specs=in_specs,
        out_specs=[row(512), row(512), row(1024), row(512), row(SSD_XBC), row(LANES)],
        out_shape=[jax.ShapeDtypeStruct((lt, 512), BF16), jax.ShapeDtypeStruct((lt, 512), BF16),
                   jax.ShapeDtypeStruct((lt, 1024), BF16), jax.ShapeDtypeStruct((lt, 512), F32),
                   jax.ShapeDtypeStruct((lt, SSD_XBC), F32), jax.ShapeDtypeStruct((lt, LANES), F32)],
        compiler_params=_cparams(("arbitrary",)),
    )(*args)


def _conv_kernel(prev_ref, cur_ref, next_ref, w_ref, b_ref, o_ref, ext_scr, *, ntiles):
    i = pl.program_id(0)
    pad = SSD_CONV // 2
    has_prev = i >= 2
    has_next = jnp.logical_and(i >= 1, i < ntiles - 1)
    ext_scr[0:8, :] = jnp.where(has_prev, prev_ref[...], 0.0)
    ext_scr[8:8 + ROW_TILE, :] = cur_ref[...]
    ext_scr[8 + ROW_TILE:16 + ROW_TILE, :] = jnp.where(has_next, next_ref[...], 0.0)
    acc = jnp.zeros((ROW_TILE, SSD_XBC), F32) + b_ref[...]
    for k in range(SSD_CONV):
        acc = acc + ext_scr[pl.ds(8 - pad + k, ROW_TILE), :] * w_ref[k:k + 1, :]
    o_ref[...] = _silu(acc)


def _ssd_conv(xbc, conv_w, conv_b):
    lt = xbc.shape[0]
    ntiles = lt // ROW_TILE
    per = ROW_TILE // 8
    last8 = lt // 8 - 1
    w = jnp.pad(conv_w, ((0, 8 - SSD_CONV), (0, 0)))
    return pl.pallas_call(
        functools.partial(_conv_kernel, ntiles=ntiles),
        grid=(ntiles,),
        in_specs=[
            pl.BlockSpec((8, SSD_XBC), lambda i: (jnp.maximum(i * per - 1, 0), 0)),
            pl.BlockSpec((ROW_TILE, SSD_XBC), lambda i: (i, 0)),
            pl.BlockSpec((8, SSD_XBC), lambda i: (jnp.minimum((i + 1) * per, last8), 0)),
            _const_spec((8, SSD_XBC)),
            _const_spec((1, SSD_XBC)),
        ],
        out_specs=pl.BlockSpec((ROW_TILE, SSD_XBC), lambda i: (i, 0)),
        out_shape=jax.ShapeDtypeStruct((lt, SSD_XBC), F32),
        scratch_shapes=[pltpu.VMEM((ROW_TILE + 16, SSD_XBC), F32)],
        compiler_params=_cparams(("arbitrary",)),
    )(xbc, xbc, xbc, w, conv_b.reshape(1, -1))


def _split_parts(x, parts):
    out = []
    for _ in range(parts):
        piece = x.astype(BF16)
        out.append(piece)
        x = x - piece.astype(F32)
    return out


def _ssd_direction(xc, dtr, dtr_t, bias, alog, bias_t, alog_t, sel, dskip, state_ref, rev):
    q = SSD_CHUNK
    hp = SSD_HEAD_DIM
    dt = _softplus(dtr + bias)
    da = dt * -jnp.exp(alog)
    dt_t = _softplus(dtr_t + bias_t)
    da_t = dt_t * -jnp.exp(alog_t)
    r_idx = lax.broadcasted_iota(jnp.int32, (q, q), 0)
    c_idx = lax.broadcasted_iota(jnp.int32, (q, q), 1)
    keep = c_idx >= r_idx if rev else c_idx <= r_idx
    tri = jnp.where(keep, 1.0, 0.0).astype(BF16)
    nt = (((1,), (1,)), ((), ()))
    cs = sum(_dot(tri, p) for p in _split_parts(da, 3))
    cs_t = sum(lax.dot_general(p, tri, nt, preferred_element_type=F32) for p in _split_parts(da_t, 3))
    csb = sum(_dot(p, sel) for p in _split_parts(cs, 3))
    dtb = sum(_dot(p, sel) for p in _split_parts(dt, 2))
    tot = csb[0:1, :] if rev else csb[q - 1:q, :]
    x = xc[:, 0:SSD_INNER]
    xdt = x * dtb
    xdt_b = xdt.astype(BF16)
    xdec = (xdt * jnp.exp(tot - csb)).astype(BF16)
    csb_up = pltpu.roll(csb, SSD_INNER - hp, axis=1)
    csb_dn = pltpu.roll(csb, hp, axis=1)
    low = lax.broadcasted_iota(jnp.int32, (1, 2 * hp), 1) < hp
    gw = SSD_GROUPS * SSD_STATE
    bm = xc[:, SSD_INNER:SSD_INNER + gw].astype(BF16)
    cm = xc[:, SSD_INNER + gw:SSD_INNER + 2 * gw].astype(BF16)
    hpg = SSD_HEADS // SSD_GROUPS
    head_row = SSD_HEADS if rev else 0
    outs = []
    for g in range(SSD_GROUPS):
        bg = bm[:, g * SSD_STATE:(g + 1) * SSD_STATE]
        cg = cm[:, g * SSD_STATE:(g + 1) * SSD_STATE]
        scores = lax.dot_general(cg, bg, nt, preferred_element_type=F32)
        cols = slice(g * hpg * hp, (g + 1) * hpg * hp)
        s_in = state_ref[:, cols]
        y_off = _dot(cg, s_in.astype(BF16))
        pairs = []
        for pp in range(hpg // 2):
            blk = slice((g * hpg + 2 * pp) * hp, (g * hpg + 2 * pp + 2) * hp)
            here, up, dn = csb[:, blk], csb_up[:, blk], csb_dn[:, blk]
            ys = []
            for e in range(2):
                hd = g * hpg + 2 * pp + e
                col = jnp.where(low, here, dn) if e == 0 else jnp.where(low, up, here)
                row = cs_t[head_row + hd:head_row + hd + 1, :]
                lmat = jnp.exp(jnp.where(keep, col - row, -jnp.inf))
                ys.append(_dot((scores * lmat).astype(BF16), xdt_b[:, blk]))
            pairs.append(jnp.where(low, ys[0], ys[1]))
        y = jnp.concatenate(pairs, axis=-1) + y_off * jnp.exp(csb[:, cols])
        if dskip is not None:
            y = y + x[:, cols] * dskip[:, cols]
        outs.append(y)
        new = lax.dot_general(bg, xdec[:, cols], (((0,), (0,)), ((), ())), preferred_element_type=F32)
        state_ref[:, cols] = s_in * jnp.exp(tot[:, cols]) + new
    return jnp.concatenate(outs, axis=-1)


def _ssd_kernel(xf_ref, dtf_ref, dtft_ref, xr_ref, dtr_ref, dtrt_ref, bias_ref, alog_ref, biast_ref,
                alogt_ref, sel_ref, dskip_ref, yf_ref, yr_ref, sf_scr, sr_scr):
    @pl.when(pl.program_id(0) == 0)
    def _():
        sf_scr[...] = jnp.zeros_like(sf_scr)
        sr_scr[...] = jnp.zeros_like(sr_scr)

    consts = (bias_ref[...], alog_ref[...], biast_ref[...], alogt_ref[...])
    per_block = ROW_TILE // SSD_CHUNK
    for c in range(per_block):
        rf = slice(c * SSD_CHUNK, (c + 1) * SSD_CHUNK)
        cr = per_block - 1 - c
        rr = slice(cr * SSD_CHUNK, (cr + 1) * SSD_CHUNK)
        yf_ref[rf, :] = _ssd_direction(xf_ref[rf, :], dtf_ref[rf, :], dtft_ref[0, c], *consts, sel_ref[0],
                                       dskip_ref[...], sf_scr, False)
        yr_ref[rr, :] = _ssd_direction(xr_ref[rr, :], dtr_ref[rr, :], dtrt_ref[0, cr], *consts, sel_ref[1], None,
                                       sr_scr, True)


def _ssd_scan(xc, dtr, dt_bias, a_log, d_skip):
    lt = xc.shape[0]
    nblk = lt // ROW_TILE
    per_block = ROW_TILE // SSD_CHUNK
    ndt = 2 * SSD_HEADS
    dtr_t = dtr[:, :ndt].reshape(nblk, per_block, SSD_CHUNK, ndt).transpose(0, 1, 3, 2)

    def rev_block(k):
        return jnp.where(k == 0, 0, nblk - k)

    fwd = lambda w: pl.BlockSpec((ROW_TILE, w), lambda k: (k, 0))
    bwd = lambda w: pl.BlockSpec((ROW_TILE, w), lambda k: (rev_block(k), 0))
    dtt = lambda index: pl.BlockSpec((1, per_block, ndt, SSD_CHUNK), lambda k: (index(k), 0, 0, 0))
    lanes = lambda v: jnp.pad(v.reshape(1, ndt), ((0, 0), (0, LANES - ndt)))
    rows = lambda v: jnp.broadcast_to(v.reshape(ndt, 1), (ndt, SSD_CHUNK))
    sel = np.zeros((2, LANES, SSD_INNER), np.float32)
    for d in range(2):
        for hd in range(SSD_HEADS):
            sel[d, d * SSD_HEADS + hd, hd * SSD_HEAD_DIM:(hd + 1) * SSD_HEAD_DIM] = 1.0
    small = (lanes(dt_bias), lanes(a_log), rows(dt_bias), rows(a_log), jnp.asarray(sel, BF16),
             jnp.repeat(d_skip, SSD_HEAD_DIM).reshape(1, -1))
    return pl.pallas_call(
        _ssd_kernel,
        grid=(nblk,),
        in_specs=[fwd(SSD_XBC), fwd(LANES), dtt(lambda k: k), bwd(SSD_XBC), bwd(LANES), dtt(rev_block)]
                 + [_const_spec(a.shape) for a in small],
        out_specs=[fwd(SSD_INNER), bwd(SSD_INNER)],
        out_shape=[jax.ShapeDtypeStruct((lt, SSD_INNER), F32)] * 2,
        scratch_shapes=[pltpu.VMEM((SSD_STATE, SSD_INNER), F32)] * 2,
        compiler_params=_cparams(("arbitrary",)),
    )(xc, dtr, dtr_t, xc, dtr, dtr_t, *small)


ROWS_PER_TILE = ROW_TILE // GRID_W


def _rope_tables(seq):
    ng = seq // GRID_W
    nctx = CTX_LEN // GRID_W

    def trig(n, count):
        inv = ROPE_THETA ** (-jnp.arange(n, dtype=F32) / n)
        ang = jnp.arange(count, dtype=F32)[:, None] * inv[None, :]
        return jnp.cos(ang), jnp.sin(ang)

    def lay(count, parts):
        return jnp.concatenate([p if hasattr(p, "shape") else jnp.full((count, p[0]), p[1], F32) for p in parts],
                               axis=1)

    def row_slabs(tabs, idents):
        full = [jnp.concatenate([jnp.full((nctx, LANES), ident, F32), t], axis=0) for t, ident in zip(tabs, idents)]
        packed = jnp.concatenate(full, axis=1).reshape(-1, ROWS_PER_TILE, len(tabs) * LANES)
        return jnp.pad(packed, ((0, 0), (0, 8 - ROWS_PER_TILE), (0, 0)))

    n0 = MLA_ROPE // 4
    cr, sr = trig(n0, ng)
    cc, sc = trig(n0, GRID_W)
    tail = LANES - MLA_QK
    row0 = row_slabs([lay(ng, [(MLA_NOPE, 1.0), cr, cr, (2 * n0, 1.0), (tail, 1.0)]),
                      lay(ng, [(MLA_NOPE, 0.0), -sr, sr, (2 * n0, 0.0), (tail, 0.0)])], (1.0, 0.0))
    col0 = jnp.concatenate([lay(GRID_W, [(MLA_NOPE, 1.0), (2 * n0, 1.0), cc, cc, (tail, 1.0)]),
                            lay(GRID_W, [(MLA_NOPE, 0.0), (2 * n0, 0.0), -sc, sc, (tail, 0.0)])], axis=1)
    n1 = DIFF_QK // 4
    cr, sr = trig(n1, ng)
    cc, sc = trig(n1, GRID_W)
    row1 = row_slabs([lay(ng, [cr, cr, (2 * n1, 1.0)] * 2),
                      lay(ng, [-sr, (n1, 0.0), (2 * n1, 0.0)] * 2),
                      lay(ng, [(n1, 0.0), sr, (2 * n1, 0.0)] * 2)], (1.0, 0.0, 0.0))
    col1 = jnp.concatenate([lay(GRID_W, [(2 * n1, 1.0), cc, cc] * 2),
                            lay(GRID_W, [(2 * n1, 0.0), -sc, (n1, 0.0)] * 2),
                            lay(GRID_W, [(2 * n1, 0.0), (n1, 0.0), sc] * 2)], axis=1)
    return row0, col0, row1, col1


def _rope_tile(row_ref, col_ref, k, is_ctx, product):
    r = row_ref[0, :, k * LANES:(k + 1) * LANES]
    c = jnp.where(is_ctx, 1.0 if product else 0.0, col_ref[:, k * LANES:(k + 1) * LANES])
    rb = jnp.concatenate([jnp.broadcast_to(r[j:j + 1], (GRID_W, LANES)) for j in range(ROWS_PER_TILE)], axis=0)
    cb = jnp.concatenate([c] * ROWS_PER_TILE, axis=0)
    return rb * cb if product else rb + cb


def kernel(x, c, ctx, c_ctx, ada_w, ada_b, ffn_w1, ffn_w3, ffn_w2, e_w_in, e_w_out, mla_g_qa, mla_w_qb, mla_g_kva, mla_w_kvb, mla_g_qn, mla_g_kn, s5_lam_re, s5_lam_im, s5_log_dt, s5_b_re, s5_b_im, s5_c_re, s5_c_im, s5_d, s5_w_glu, s5_b_glu, o_w_in, o_w_out, diff_g_q, diff_g_k, diff_lq1, diff_lk1, diff_lq2, diff_lk2, diff_g_o, ssd_conv_w, ssd_conv_b, ssd_dt_bias, ssd_a_log, ssd_d, ssd_g):
    depth = ada_w.shape[0]
    seq = x.shape[1]
    assert x.shape[0] == 1 and ctx.shape[1] == CTX_LEN and seq % ROW_TILE == 0
    lt = seq + CTX_LEN
    xs = None
    mods = _mod_vectors(c, c_ctx, ada_w, ada_b)
    rope_row0, rope_col0, rope_row1, rope_col1 = _rope_tables(seq)
    for i in range(depth):
        j = i // 2
        modx, modc = mods[i, 0], mods[i, 1]
        if xs is None and i % 2 == 1:
            xs = jnp.concatenate([ctx[0], x[0]], axis=0)
        if i % 2 == 0:
            stream = (x[0], ctx[0], 0) if xs is None else (xs, xs, 1)
            q, k, v, u = _even_in(stream, lt, modx, modc, e_w_in[j], mla_g_qa[j], mla_w_qb[j], mla_g_kva[j],
                                  mla_w_kvb[j], mla_g_qn[j], mla_g_kn[j], rope_row0, rope_col0)
            att = _mla_attention(q, k, v)
            mats = _s5_matrices(s5_lam_re[j], s5_lam_im[j], s5_log_dt[j], s5_b_re[j], s5_b_im[j],
                                s5_c_re[j], s5_c_im[j], s5_d[j])
            y = _s5_mixer(u, mats)
            xs = _even_out(stream, att, y, modx, modc, s5_w_glu[j], s5_b_glu[j], e_w_out[j])
        else:
            q, k, v, z, xbc, dtr = _odd_in(xs, modx, modc, o_w_in[j], diff_g_q[j], diff_g_k[j],
                                           rope_row1, rope_col1)
            lam_init = 0.8 - 0.6 * math.exp(-0.3 * i)
            lam = (jnp.exp(jnp.sum(diff_lq1[j] * diff_lk1[j])) - jnp.exp(jnp.sum(diff_lq2[j] * diff_lk2[j]))
                   + lam_init)
            att = _diff_attention(q, k, v, jnp.full((1, LANES), lam, F32), diff_g_o[j].reshape(1, -1),
                                  1.0 - lam_init)
            xc = _ssd_conv(xbc, ssd_conv_w[j], ssd_conv_b[j])
            yf, yr = _ssd_scan(xc, dtr, ssd_dt_bias[j], ssd_a_log[j], ssd_d[j])
            xs = _odd_out(xs, att, yf, yr, z, modx, modc, ssd_g[j], o_w_out[j])
        last = i == depth - 1
        xs = _ffn(xs, modx, modc, ffn_w1[i], ffn_w3[i], ffn_w2[i], latents_only=last)
    return xs[None]
```

```python
import functools
import math

import numpy as np
import jax
import jax.numpy as jnp
from jax import lax
from jax.experimental import pallas as pl
from jax.experimental.pallas import tpu as pltpu

F32 = jnp.float32
BF16 = jnp.bfloat16
HIGHEST = lax.Precision.HIGHEST

D_MODEL = 1024
CTX_LEN = 256
GRID_W = 64
GROUP_WIDTH = 512
D_FF = 2816
EPS = 1e-6
ROPE_THETA = 10000.0
LOG2E = math.log2(math.e)

MLA_HEADS, MLA_NOPE, MLA_ROPE, MLA_V = 8, 64, 32, 64
MLA_QK = MLA_NOPE + MLA_ROPE
MLA_Q_RANK, MLA_KV_RANK = 384, 256
S5_CH, S5_GROUPS, S5_STATE = 16, 32, 64
S5_T = 16
S5_GPB = 8
DIFF_HEADS, DIFF_QK, DIFF_V = 4, 64, 128
SSD_HEADS, SSD_HEAD_DIM, SSD_GROUPS, SSD_STATE, SSD_CONV, SSD_CHUNK = 8, 64, 2, 128, 5, 128
SSD_INNER = 512
SSD_XBC = SSD_INNER + 2 * SSD_GROUPS * SSD_STATE

LANES = 128
ROW_TILE = 256
FFN_TILE = 640
FFN_TILE_LATENT = 512
FFN_CHUNK = 256
KV_UNROLL = 4
KV_UNROLL_FULL = 12
VMEM_LIMIT = 56 * 1024 * 1024


def _cparams(sem):
    return pltpu.CompilerParams(dimension_semantics=sem, vmem_limit_bytes=VMEM_LIMIT)


def _dot(a, b):
    return jnp.dot(a, b, preferred_element_type=F32)


def _split_dot(x, w, parts):
    acc = None
    for _ in range(parts):
        piece = x.astype(BF16)
        term = _dot(piece, w)
        acc = term if acc is None else acc + term
        x = x - piece.astype(F32)
    return acc


def _rms(x):
    return x * lax.rsqrt(jnp.mean(x * x, axis=-1, keepdims=True) + EPS)


def _sigmoid(x):
    return 1.0 / (1.0 + jnp.exp(-x))


def _silu(x):
    return x * _sigmoid(x)


def _gelu_tanh(x):
    return 0.5 * x * (1.0 + jnp.tanh(math.sqrt(2.0 / math.pi) * (x + 0.044715 * (x * x * x))))


def _softplus(x):
    return jnp.maximum(x, 0.0) + jnp.log(1.0 + jnp.exp(-jnp.abs(x)))


def _const_spec(shape):
    nd = len(shape)
    return pl.BlockSpec(shape, lambda *_: (0,) * nd)


def _mod_kernel(c_ref, w_ref, b_ref, o_ref):
    s = _silu(c_ref[...]).astype(BF16)
    o_ref[0] = _dot(s, w_ref[0].astype(BF16)) + b_ref[0]


def _mod_vectors(c, c_ctx, ada_w, ada_b):
    depth = ada_w.shape[0]
    cc = jnp.zeros((8, D_MODEL), F32).at[0].set(c[0]).at[1].set(c_ctx)
    nblk = 6
    out = pl.pallas_call(
        _mod_kernel,
        grid=(depth, nblk),
        in_specs=[
            pl.BlockSpec((8, D_MODEL), lambda i, j: (0, 0)),
            pl.BlockSpec((1, D_MODEL, D_MODEL), lambda i, j: (i, 0, j)),
            pl.BlockSpec((1, 1, D_MODEL), lambda i, j: (i, 0, j)),
        ],
        out_specs=pl.BlockSpec((1, 8, D_MODEL), lambda i, j: (i, 0, j)),
        out_shape=jax.ShapeDtypeStruct((depth, 8, 6 * D_MODEL), F32),
        compiler_params=_cparams(("arbitrary", "arbitrary")),
    )(cc, ada_w, ada_b.reshape(depth, 1, 6 * D_MODEL))
    mods = out[:, :2].reshape(depth, 2, 6, D_MODEL)
    return jnp.pad(mods, ((0, 0), (0, 0), (0, 2), (0, 0)))


def _mod_rows(modx_ref, modc_ref, k, row0, nrows):
    rows = row0 + lax.broadcasted_iota(jnp.int32, (nrows, 1), 0)
    return jnp.where(rows < CTX_LEN, modc_ref[k:k + 1, :], modx_ref[k:k + 1, :])


def _rope_partner(n_half):
    return np.array([i + n_half if i < n_half else i - n_half for i in range(2 * n_half)])


def _stream_specs(stream):
    lat, crows, first = stream
    specs = [pl.BlockSpec((ROW_TILE, D_MODEL), lambda i: (jnp.maximum(i - 1 + first, 0), 0)),
             pl.BlockSpec((ROW_TILE, D_MODEL), lambda i: (0, 0))]
    return [lat, crows], specs


def _even_in_kernel(x_ref, c_ref, modx_ref, modc_ref, win_ref, gqa_ref, wq_ref, wqs_ref, gkva_ref, wk_ref,
                    wv_ref, gq_ref, gqs_ref, gk_ref, gks_ref, rrow_ref, rcol_ref,
                    q_ref, k_ref, v_ref, u_ref):
    i = pl.program_id(0)
    is_ctx = i == 0
    sh = jnp.where(is_ctx, modc_ref[0:1, :], modx_ref[0:1, :])
    sc = jnp.where(is_ctx, modc_ref[1:2, :], modx_ref[1:2, :])
    xt = jnp.where(is_ctx, c_ref[...], x_ref[...])
    h = (_rms(xt) * (1.0 + sc) + sh).astype(BF16)
    proj = _dot(h, win_ref[...])
    cq = proj[:, 0:384]
    ckv = proj[:, 384:640]
    u_ref[...] = proj[:, 640:1152]
    krb = proj[:, 1152:1280]
    krs = proj[:, 1280:1408]
    lane = lax.broadcasted_iota(jnp.int32, (1, LANES), 1)
    cqn = (_rms(cq) * gqa_ref[...]).astype(BF16)
    qf = _dot(cqn, wq_ref[...])
    qs = _dot(cqn, wqs_ref[...])
    ckvn = (_rms(ckv) * gkva_ref[...]).astype(BF16)
    kf = _dot(ckvn, wk_ref[...])
    vf = _dot(ckvn, wv_ref[...])
    for pr in range(MLA_HEADS // 2):
        vp = vf[:, pr * LANES:(pr + 1) * LANES]
        v_ref[:, (2 * pr) * LANES:(2 * pr + 1) * LANES] = jnp.where(lane < MLA_V, vp, 1.0).astype(BF16)
        v_ref[:, (2 * pr + 1) * LANES:(2 * pr + 2) * LANES] = jnp.where(lane < MLA_V, 1.0, vp).astype(BF16)
    cos = _rope_tile(rrow_ref, rcol_ref, 0, is_ctx, True)
    sin = _rope_tile(rrow_ref, rcol_ref, 1, is_ctx, False)
    qscale = MLA_QK ** -0.5 * LOG2E
    inv_n = 1.0 / MLA_QK
    for hd in range(MLA_HEADS):
        blk = slice(hd * LANES, (hd + 1) * LANES)
        qh = qf[:, blk]
        rq = lax.rsqrt(jnp.sum(qh * qh, axis=-1, keepdims=True) * inv_n + EPS) * qscale
        qo = (qh * (gq_ref[...] * cos) + qs[:, blk] * (gqs_ref[...] * sin)) * rq
        q_ref[:, blk] = qo.astype(BF16)
        kh = kf[:, blk] + krb
        rk = lax.rsqrt(jnp.sum(kh * kh, axis=-1, keepdims=True) * inv_n + EPS)
        ko = (kh * (gk_ref[...] * cos) + krs * (gks_ref[...] * sin)) * rk
        k_ref[:, blk] = ko.astype(BF16)


def _even_in(stream, lt, modx, modc, w_in, g_qa, w_qb, g_kva, w_kvb, g_qn, g_kn, rope_row, rope_col):
    perm = _rope_partner(MLA_ROPE // 4)
    perm = np.concatenate([perm, perm + MLA_ROPE // 2])
    cq_w, ckv_w = w_in[:, :384], w_in[:, 384:640]
    kr_w, u_w = w_in[:, 640:672], w_in[:, 672:]
    zblk = jnp.zeros((D_MODEL, LANES), F32)
    krblk = zblk.at[:, MLA_NOPE:MLA_QK].set(kr_w)
    krsblk = zblk.at[:, MLA_NOPE:MLA_QK].set(kr_w[:, perm])
    win = jnp.concatenate([cq_w, ckv_w, u_w, krblk, krsblk], axis=1).astype(BF16)
    wq3 = w_qb.reshape(MLA_Q_RANK, MLA_HEADS, MLA_QK)
    wq = jnp.zeros((MLA_Q_RANK, MLA_HEADS, LANES), F32).at[:, :, :MLA_QK].set(wq3)
    wqs = jnp.zeros((MLA_Q_RANK, MLA_HEADS, LANES), F32).at[:, :, MLA_NOPE:MLA_QK].set(
        wq3[:, :, MLA_NOPE:][:, :, perm])
    wkv3 = w_kvb.reshape(MLA_KV_RANK, MLA_HEADS, MLA_NOPE + MLA_V)
    wk = jnp.zeros((MLA_KV_RANK, MLA_HEADS, LANES), F32).at[:, :, :MLA_NOPE].set(wkv3[:, :, :MLA_NOPE])
    wv = wkv3[:, :, MLA_NOPE:].reshape(MLA_KV_RANK, MLA_HEADS * MLA_V)

    def pad_gain(g):
        gp = jnp.zeros((1, LANES), F32).at[0, :MLA_QK].set(g)
        gs = jnp.zeros((1, LANES), F32).at[0, MLA_NOPE:MLA_QK].set(g[MLA_NOPE:][perm])
        return gp, gs

    gq, gqs = pad_gain(g_qn)
    gk, gks = pad_gain(g_kn)
    hw = MLA_HEADS * LANES
    consts = (modx, modc, win, g_qa.reshape(1, -1), wq.reshape(MLA_Q_RANK, hw).astype(BF16),
              wqs.reshape(MLA_Q_RANK, hw).astype(BF16), g_kva.reshape(1, -1),
              wk.reshape(MLA_KV_RANK, hw).astype(BF16), wv.astype(BF16), gq, gqs, gk, gks)
    row = lambda w: pl.BlockSpec((ROW_TILE, w), lambda i: (i, 0))
    rows, row_specs = _stream_specs(stream)
    args = (*rows, *consts, rope_row, rope_col)
    in_specs = (row_specs + [_const_spec(a.shape) for a in consts]
                + [pl.BlockSpec((1,) + rope_row.shape[1:], lambda i: (i, 0, 0)), _const_spec(rope_col.shape)])
    return pl.pallas_call(
        _even_in_kernel,
        grid=(lt // ROW_TILE,),
        in_specs=in_specs,
        out_specs=[row(hw), row(hw), row(hw), row(GROUP_WIDTH)],
        out_shape=[jax.ShapeDtypeStruct((lt, hw), BF16), jax.ShapeDtypeStruct((lt, hw), BF16),
                   jax.ShapeDtypeStruct((lt, hw), BF16),
                   jax.ShapeDtypeStruct((lt, GROUP_WIDTH), F32)],
        compiler_params=_cparams(("arbitrary",)),
    )(*args)


def _kv_chunk(lt):
    for tk in (1280, 640, 256):
        if lt % tk == 0:
            return tk
    raise ValueError(lt)


def _scores(q, k_ref, start, size):
    kc = k_ref[pl.ds(start, size), :]
    return lax.dot_general(q, kc, (((1,), (1,)), ((), ())), preferred_element_type=F32)


def _softmax_update(s_ref, v_ref, start, size, carry, row_max=None):
    m, acc = carry
    if row_max is None:
        row_max = jnp.max(s_ref[...], axis=-1, keepdims=True)
    mn = jnp.maximum(m, row_max)
    alpha = jnp.exp2(m - mn)
    p = jnp.exp2(s_ref[...] - mn)
    acc = alpha * acc + _dot(p.astype(BF16), v_ref[pl.ds(start, size), :])
    return mn, acc


def _attend(streams, s_scr, is_ctx, lt, tk, unroll=KV_UNROLL):
    tq = streams[0][0].shape[0]
    n = lt // tk
    init = tuple((jnp.full((tq, 1), -jnp.inf, F32), jnp.zeros((tq, v_ref.shape[-1]), F32), jnp.zeros((tq, 1), F32))
                 for _, _, _, v_ref in streams)
    slot_of = lambda chunk: 2 if chunk == 0 else chunk % 2

    def issue_next(t, q_next, k_ref):
        s_scr[3 * t + 2] = _scores(q_next, k_ref, 0, tk)

    def consume(chunk, slot, carries, last):
        start = chunk * tk
        if not isinstance(start, int):
            start = pl.multiple_of(start, tk)
        out = []
        for t, ((q, q_next, k_ref, v_ref), (m, acc, ahead_max)) in enumerate(zip(streams, carries)):
            here_max = None if slot == 2 else ahead_max
            if last and slot != 2:
                issue_next(t, q_next, k_ref)
            elif not last:
                ahead = _scores(q, k_ref, start + tk, tk)
                s_scr[3 * t + (1 - slot if slot < 2 else 1)] = ahead
                ahead_max = jnp.max(ahead, axis=-1, keepdims=True)
            out.append(_softmax_update(s_scr.at[3 * t + slot], v_ref, start, tk, (m, acc), here_max) + (ahead_max,))
            if last and slot == 2:
                issue_next(t, q_next, k_ref)
        return tuple(out)

    def group(i, carries):
        for u in range(unroll):
            carries = consume(1 + unroll * i + u, (1 + u) % 2, carries, False)
        return carries

    def ctx_branch():
        out = []
        for t, ((q, q_next, k_ref, v_ref), (m, acc, ahead_max)) in enumerate(zip(streams, init)):
            issue_next(t, q_next, k_ref)
            s_scr[3 * t, :, 0:CTX_LEN] = _scores(q, k_ref, 0, CTX_LEN)
            out.append(_softmax_update(s_scr.at[3 * t, :, 0:CTX_LEN], v_ref, 0, CTX_LEN, (m, acc)) + (ahead_max,))
        return tuple(out)

    def full_branch():
        carries = consume(0, 2, init, n == 1)
        ngroups = max(n - 2, 0) // unroll
        if ngroups:
            carries = lax.fori_loop(0, ngroups, group, carries)
        for chunk in range(1 + unroll * ngroups, n):
            carries = consume(chunk, slot_of(chunk), carries, chunk == n - 1)
        return carries

    out = lax.cond(is_ctx, ctx_branch, full_branch)
    return [acc for (_, acc, _) in out]


def _mla_attn_kernel(q_ref, qn_ref, k_ref, v_ref, o_ref, s_scr, *, lt, tk):
    is_ctx = pl.program_id(1) == 0
    lane = lax.broadcasted_iota(jnp.int32, (1, LANES), 1)
    blks = [slice(hh * LANES, (hh + 1) * LANES) for hh in range(2)]
    acc_a, acc_b = _attend([(q_ref[:, b], qn_ref[:, b], k_ref.at[:, b], v_ref.at[:, b]) for b in blks],
                           s_scr, is_ctx, lt, tk, unroll=KV_UNROLL_FULL)
    o_ref[...] = jnp.where(lane < MLA_V, acc_a / acc_a[:, MLA_V:MLA_V + 1], acc_b / acc_b[:, 0:1])


def _mla_attention(q, k, v):
    lt = q.shape[0]
    tk = _kv_chunk(lt)
    npair = MLA_HEADS // 2
    last = lt // ROW_TILE - 1
    return pl.pallas_call(
        functools.partial(_mla_attn_kernel, lt=lt, tk=tk),
        grid=(npair, lt // ROW_TILE),
        in_specs=[
            pl.BlockSpec((ROW_TILE, 2 * LANES), lambda p, i: (i, p)),
            pl.BlockSpec((ROW_TILE, 2 * LANES), lambda p, i: (jnp.minimum(i + 1, last), p)),
            pl.BlockSpec((lt, 2 * LANES), lambda p, i: (0, p), pipeline_mode=pl.Buffered(1)),
            pl.BlockSpec((lt, 2 * LANES), lambda p, i: (0, p), pipeline_mode=pl.Buffered(1)),
        ],
        out_specs=pl.BlockSpec((ROW_TILE, LANES), lambda p, i: (i, p)),
        out_shape=jax.ShapeDtypeStruct((lt, GROUP_WIDTH), F32),
        scratch_shapes=[pltpu.VMEM((6, ROW_TILE, tk), F32)],
        compiler_params=_cparams(("arbitrary", "arbitrary")),
    )(q, q, k, v)


def _diff_attn_kernel(q_ref, qn_ref, k_ref, v_ref, lam_ref, go_ref, o_ref, s_scr, *, lt, tk, out_scale):
    is_ctx = pl.program_id(1) == 0
    lane = lax.broadcasted_iota(jnp.int32, (1, LANES), 1)
    zero = jnp.zeros((ROW_TILE, LANES), BF16)
    first = lambda q: jnp.where(lane < DIFF_QK, q, zero)
    second = lambda q: jnp.where(lane < DIFF_QK, zero, q)
    q, qn = q_ref[...], qn_ref[...]
    a1, a2 = _attend([(first(q), first(qn), k_ref, v_ref), (second(q), second(qn), k_ref, v_ref)],
                     s_scr, is_ctx, lt, tk, unroll=KV_UNROLL_FULL)
    o = (a1[:, 0:DIFF_V] / a1[:, DIFF_V:DIFF_V + 1]
         - lam_ref[...] * (a2[:, 0:DIFF_V] / a2[:, DIFF_V:DIFF_V + 1]))
    o_ref[...] = _rms(o) * go_ref[...] * out_scale


def _diff_attention(q, k, v, lam, g_o, out_scale):
    lt = q.shape[0]
    tk = _kv_chunk(lt)
    last = lt // ROW_TILE - 1
    return pl.pallas_call(
        functools.partial(_diff_attn_kernel, lt=lt, tk=tk, out_scale=out_scale),
        grid=(DIFF_HEADS, lt // ROW_TILE),
        in_specs=[
            pl.BlockSpec((ROW_TILE, LANES), lambda h, i: (i, h)),
            pl.BlockSpec((ROW_TILE, LANES), lambda h, i: (jnp.minimum(i + 1, last), h)),
            pl.BlockSpec((lt, LANES), lambda h, i: (0, h), pipeline_mode=pl.Buffered(1)),
            pl.BlockSpec((lt, 2 * LANES), lambda h, i: (0, h), pipeline_mode=pl.Buffered(1)),
            _const_spec((1, LANES)),
            _const_spec((1, LANES)),
        ],
        out_specs=pl.BlockSpec((ROW_TILE, LANES), lambda h, i: (i, h)),
        out_shape=jax.ShapeDtypeStruct((lt, GROUP_WIDTH), F32),
        scratch_shapes=[pltpu.VMEM((6, ROW_TILE, tk), F32)],
        compiler_params=_cparams(("arbitrary", "arbitrary")),
    )(q, q, k, v, lam, g_o)


def _s5_matrices(lam_re, lam_im, log_dt, b_re, b_im, c_re, c_im, d_skip):
    t = S5_T
    g, n, ch = S5_GROUPS, S5_STATE, S5_CH
    dt = jnp.exp(log_dt)[:, :, None, None]
    tau = jnp.arange(t + 1, dtype=F32)
    mag = jnp.exp(lam_re[..., None] * dt * tau)
    ang = lam_im[..., None] * dt * tau
    p_re, p_im = mag * jnp.cos(ang), mag * jnp.sin(ang)
    a_re, a_im = p_re[..., 1], p_im[..., 1]
    den = lam_re * lam_re + lam_im * lam_im
    k_re = ((a_re - 1.0) * lam_re + a_im * lam_im) / den
    k_im = (a_im * lam_re - (a_re - 1.0) * lam_im) / den
    bb_re = k_re[..., None] * b_re - k_im[..., None] * b_im
    bb_im = k_re[..., None] * b_im + k_im[..., None] * b_re
    ct_re = c_re.transpose(0, 1, 3, 2)[:, :, :, None, :]
    ct_im = c_im.transpose(0, 1, 3, 2)[:, :, :, None, :]
    ca_re = ct_re * p_re[..., None] - ct_im * p_im[..., None]
    ca_im = ct_re * p_im[..., None] + ct_im * p_re[..., None]
    bt_re = bb_re.transpose(0, 1, 3, 2)[..., None, None]
    bt_im = bb_im.transpose(0, 1, 3, 2)[..., None, None]
    kk = jnp.sum(bt_re * ca_re[:, :, None] - bt_im * ca_im[:, :, None], axis=3)
    strip = jnp.concatenate([kk[1][:, :, t - 1:0:-1], kk[0][:, :, 0:1] + kk[1][:, :, 0:1], kk[0][:, :, 1:t]],
                            axis=2).reshape(g, ch, (2 * t - 1) * ch)
    nb, gp = g // S5_GPB, S5_GPB
    wide = gp * t * ch
    strip = strip.reshape(g, ch, 2 * t - 1, ch)
    tiled = jnp.tile(jnp.pad(strip, ((0, 0), (0, 0), (0, 1), (0, 0))), (1, 1, t, 1))[:, :, :t * (2 * t - 1)]
    mc = tiled.reshape(g, ch, t, 2 * t - 1, ch)[:, :, :, t - 1:]
    mc = mc.reshape(nb, gp, ch, t, t * ch).transpose(0, 3, 1, 2, 4).reshape(nb, wide, t * ch)

    def w_of(d, reverse):
        pr = p_re[d][:, :, 0:t].transpose(0, 2, 1)[:, :, None, :]
        pi = p_im[d][:, :, 0:t].transpose(0, 2, 1)[:, :, None, :]
        br = bb_re[d].transpose(0, 2, 1)[:, None]
        bi = bb_im[d].transpose(0, 2, 1)[:, None]
        out = [pr * br - pi * bi, pr * bi + pi * br]
        return [jnp.flip(o, axis=1) for o in out] if reverse else out

    wc = jnp.concatenate(w_of(0, True) + w_of(1, False), axis=-1)
    wc = wc.reshape(nb, gp, t, ch, 4 * n).transpose(0, 2, 1, 3, 4).reshape(nb, wide, 4 * n)
    v4 = jnp.stack([ca_re[0][:, :, 1:t + 1], -ca_im[0][:, :, 1:t + 1],
                    ca_re[1][:, :, t:0:-1], -ca_im[1][:, :, t:0:-1]], axis=0)
    vc = v4.reshape(4, nb, gp, n, t * ch).transpose(1, 0, 2, 3, 4).reshape(nb, 4 * gp * n, t * ch)
    m_big = _s5_expand(mc.astype(BF16), row_bits=4, col_bits=4)
    w_big = _s5_expand(wc.astype(BF16), row_bits=4, col_bits=6)
    v_big = _s5_expand(vc.astype(BF16), row_bits=6, col_bits=4)
    a16 = jnp.stack([p_re[0][..., t], p_im[0][..., t], p_re[1][..., t], p_im[1][..., t]], axis=0)
    a16 = a16.reshape(4, nb, gp * n).transpose(1, 0, 2)
    a16 = jnp.pad(a16, ((0, 0), (0, 4), (0, 0)))
    return m_big, v_big, w_big, a16, d_skip.reshape(nb, 1, gp * ch)


def _s5_expand_kernel(c_ref, o_ref, *, row_bits, col_bits):
    nrow, ncol = o_ref.shape[1], o_ref.shape[2]
    kc = c_ref.shape[2]
    gbits = S5_GPB.bit_length() - 1
    r = lax.broadcasted_iota(jnp.int32, (kc, ncol), 0)
    c = lax.broadcasted_iota(jnp.int32, (kc, ncol), 1)
    bmask = (1 << col_bits) - 1
    same = jnp.logical_and((r >> col_bits) == (c >> (col_bits + gbits)), (r & bmask) == (c & bmask))
    tiling = jnp.where(same, 1.0, 0.0).astype(BF16)
    slab = S5_GPB << row_bits
    rr = lax.broadcasted_iota(jnp.int32, (slab, ncol), 0)
    cc = lax.broadcasted_iota(jnp.int32, (slab, ncol), 1)
    keep = ((rr >> row_bits) & (S5_GPB - 1)) == ((cc >> col_bits) & (S5_GPB - 1))
    for k in range(nrow // slab):
        rows = slice(k * slab, (k + 1) * slab)
        o_ref[0, rows, :] = jnp.where(keep, _dot(c_ref[0, rows, :], tiling), 0.0).astype(BF16)


def _s5_expand(compact, row_bits, col_bits):
    nb, nrow, kc = compact.shape
    ncol = kc * S5_GPB
    return pl.pallas_call(
        functools.partial(_s5_expand_kernel, row_bits=row_bits, col_bits=col_bits),
        grid=(nb,),
        in_specs=[pl.BlockSpec((1, nrow, kc), lambda b: (b, 0, 0))],
        out_specs=pl.BlockSpec((1, nrow, ncol), lambda b: (b, 0, 0)),
        out_shape=jax.ShapeDtypeStruct((nb, nrow, ncol), BF16),
        compiler_params=_cparams(("arbitrary",)),
    )(compact)


def _s5_gather_kernel(u_ref, wz_ref, ub_ref, z_ref):
    nrow = ub_ref.shape[1]
    for j in range(S5_T):
        ub_ref[0, :, j * LANES:(j + 1) * LANES] = u_ref[pl.ds(j, nrow, stride=S5_T), :].astype(BF16)
    z_ref[0] = _dot(ub_ref[0], wz_ref[0])


def _s5_scan_kernel(z_ref, a_ref, s_ref, *, nblk, nctx):
    w = S5_GPB * S5_STATE
    cols = [slice(k * w, (k + 1) * w) for k in range(4)]
    a = a_ref[0]
    afr, afi, arr, ari = a[0:1], a[1:2], a[2:3], a[3:4]

    def advance(rows_f, rows_r, carry):
        fr, fi, rr, ri = carry
        for col, val, rows in zip(cols, carry, (rows_f, rows_f, rows_r, rows_r)):
            s_ref[0, rows, col] = val
        zfr, zfi = z_ref[0, rows_f, cols[0]], z_ref[0, rows_f, cols[1]]
        zrr, zri = z_ref[0, rows_r, cols[2]], z_ref[0, rows_r, cols[3]]
        return (afr * fr - afi * fi + zfr, afr * fi + afi * fr + zfi,
                arr * rr - ari * ri + zrr, arr * ri + ari * rr + zri)

    zero = jnp.zeros((1, w), F32)
    carry = lax.fori_loop(0, nctx, lambda k, c: advance(pl.ds(k, 1), pl.ds(nctx - 1 - k, 1), c), (zero,) * 4)
    lax.fori_loop(0, nblk - nctx, lambda k, c: advance(pl.ds(nctx + k, 1), pl.ds(nblk - 1 - k, 1), c), carry)


def _s5_out_kernel(ub_ref, s_ref, u_ref, m_ref, v_ref, d_ref, y_ref):
    nrow = ub_ref.shape[1]
    yb = _dot(ub_ref[0], m_ref[0]) + _dot(s_ref[0].astype(BF16), v_ref[0])
    for i in range(S5_T):
        rows = pl.ds(i, nrow, stride=S5_T)
        y_ref[rows, :] = yb[:, i * LANES:(i + 1) * LANES] + u_ref[rows, :] * d_ref[0]


def _s5_mixer(u, mats):
    lt = u.shape[0]
    nblk = lt // S5_T
    nb = S5_GROUPS // S5_GPB
    wide = S5_GPB * S5_T * S5_CH
    swide = 4 * S5_GPB * S5_STATE
    m_big, v_big, wz, a16, dvec = mats
    tb = max(c for c in range(16, 257, 16) if nblk % c == 0)
    ntile = nblk // tb
    tok = pl.BlockSpec((tb * S5_T, LANES), lambda b, r: (r, b))
    blk = lambda w: pl.BlockSpec((1, tb, w), lambda b, r: (b, r, 0))
    per_b = lambda shape: pl.BlockSpec((1,) + shape, lambda b, r: (b, 0, 0))
    ub, z = pl.pallas_call(
        _s5_gather_kernel,
        grid=(nb, ntile),
        in_specs=[tok, per_b(wz.shape[1:])],
        out_specs=[blk(wide), blk(swide)],
        out_shape=[jax.ShapeDtypeStruct((nb, nblk, wide), BF16), jax.ShapeDtypeStruct((nb, nblk, swide), F32)],
        compiler_params=_cparams(("arbitrary", "arbitrary")),
    )(u, wz)
    s = pl.pallas_call(
        functools.partial(_s5_scan_kernel, nblk=nblk, nctx=CTX_LEN // S5_T),
        grid=(nb,),
        in_specs=[pl.BlockSpec((1, nblk, swide), lambda b: (b, 0, 0)),
                  pl.BlockSpec((1,) + a16.shape[1:], lambda b: (b, 0, 0))],
        out_specs=pl.BlockSpec((1, nblk, swide), lambda b: (b, 0, 0)),
        out_shape=jax.ShapeDtypeStruct((nb, nblk, swide), F32),
        compiler_params=_cparams(("arbitrary",)),
    )(z, a16)
    return pl.pallas_call(
        _s5_out_kernel,
        grid=(nb, ntile),
        in_specs=[blk(wide), blk(swide), tok,
                  pl.BlockSpec((1,) + m_big.shape[1:], lambda b, r: (b, 0, 0), pipeline_mode=pl.Buffered(1)),
                  pl.BlockSpec((1,) + v_big.shape[1:], lambda b, r: (b, 0, 0), pipeline_mode=pl.Buffered(1)),
                  per_b((1, LANES))],
        out_specs=tok,
        out_shape=jax.ShapeDtypeStruct((lt, GROUP_WIDTH), F32),
        compiler_params=_cparams(("arbitrary", "arbitrary")),
    )(ub, s, u, m_big, v_big, dvec)


def _even_out_kernel(x_ref, c_ref, att_ref, y_ref, modx_ref, modc_ref, wglu_ref, bglu_ref, wout_ref, o_ref):
    is_ctx = pl.program_id(0) == 0
    gate_a = jnp.where(is_ctx, modc_ref[2:3, :], modx_ref[2:3, :])
    g = _gelu_tanh(y_ref[...])
    ssm = g * _sigmoid(_dot(g.astype(BF16), wglu_ref[...]) + bglu_ref[...])
    mix = (_dot(att_ref[...].astype(BF16), wout_ref[0:GROUP_WIDTH, :])
           + _dot(ssm.astype(BF16), wout_ref[GROUP_WIDTH:, :]))
    o_ref[...] = jnp.where(is_ctx, c_ref[...], x_ref[...]) + gate_a * mix


def _even_out(stream, att, y, modx, modc, w_glu, b_glu, w_out):
    lt = att.shape[0]
    row = lambda w: pl.BlockSpec((ROW_TILE, w), lambda i: (i, 0))
    rows, row_specs = _stream_specs(stream)
    return pl.pallas_call(
        _even_out_kernel,
        grid=(lt // ROW_TILE,),
        in_specs=row_specs + [row(GROUP_WIDTH), row(GROUP_WIDTH), _const_spec(modx.shape),
                              _const_spec(modc.shape), _const_spec(w_glu.shape), _const_spec((1, GROUP_WIDTH)),
                              _const_spec(w_out.shape)],
        out_specs=row(D_MODEL),
        out_shape=jax.ShapeDtypeStruct((lt, D_MODEL), F32),
        compiler_params=_cparams(("arbitrary",)),
    )(*rows, att, y, modx, modc, w_glu.astype(BF16), b_glu.reshape(1, -1), w_out.astype(BF16))


def _odd_mix(att_ref, yf_ref, yr_ref, z_ref, gn_ref, wout_ref):
    gy = (yf_ref[...] + yr_ref[...]) * _silu(z_ref[...])
    gw = SSD_INNER // SSD_GROUPS
    parts = [_rms(gy[:, j * gw:(j + 1) * gw]) for j in range(SSD_GROUPS)]
    ssm = jnp.concatenate(parts, axis=-1) * gn_ref[...]
    return (_dot(att_ref[...].astype(BF16), wout_ref[0:GROUP_WIDTH, :])
            + _dot(ssm.astype(BF16), wout_ref[GROUP_WIDTH:, :]))


def _odd_out_kernel(x_ref, att_ref, yf_ref, yr_ref, z_ref, modx_ref, modc_ref, gn_ref, wout_ref, o_ref):
    is_ctx = pl.program_id(0) == 0
    gate_a = jnp.where(is_ctx, modc_ref[2:3, :], modx_ref[2:3, :])
    o_ref[...] = x_ref[...] + gate_a * _odd_mix(att_ref, yf_ref, yr_ref, z_ref, gn_ref, wout_ref)


def _odd_out(xs, att, yf, yr, z, modx, modc, g_norm, w_out):
    lt = xs.shape[0]
    row = lambda w: pl.BlockSpec((ROW_TILE, w), lambda i: (i, 0))
    return pl.pallas_call(
        _odd_out_kernel,
        grid=(lt // ROW_TILE,),
        in_specs=[row(D_MODEL), row(GROUP_WIDTH), row(SSD_INNER), row(SSD_INNER), row(SSD_INNER),
                  _const_spec(modx.shape), _const_spec(modc.shape), _const_spec((1, SSD_INNER)),
                  _const_spec(w_out.shape)],
        out_specs=row(D_MODEL),
        out_shape=jax.ShapeDtypeStruct((lt, D_MODEL), F32),
        compiler_params=_cparams(("arbitrary",)),
    )(xs, att, yf, yr, z, modx, modc, g_norm.reshape(1, -1), w_out.astype(BF16))


def _ffn_kernel(x_ref, *refs, row_offset, odd_mixer):
    modx_ref, modc_ref, w1_ref, w3_ref, w2_ref, o_ref = refs[-6:]
    tm = x_ref.shape[0]
    row0 = row_offset + pl.program_id(0) * tm
    x = x_ref[...]
    if odd_mixer:
        x = x + _mod_rows(modx_ref, modc_ref, 2, row0, tm) * _odd_mix(*refs[:6])
    sh = _mod_rows(modx_ref, modc_ref, 3, row0, tm)
    sc = _mod_rows(modx_ref, modc_ref, 4, row0, tm)
    gate = _mod_rows(modx_ref, modc_ref, 5, row0, tm)
    h = (_rms(x) * (1.0 + sc) + sh).astype(BF16)
    acc = jnp.zeros((tm, D_MODEL), F32)
    for c in range(D_FF // FFN_CHUNK):
        blk = slice(c * FFN_CHUNK, (c + 1) * FFN_CHUNK)
        a = _dot(h, w1_ref[:, blk])
        b = _dot(h, w3_ref[:, blk])
        acc = acc + _dot((_silu(a) * b).astype(BF16), w2_ref[blk, :])
    o_ref[...] = x + gate * acc


def _ffn(xs, modx, modc, w1, w3, w2, latents_only=False, odd_mixer=None):
    lt = xs.shape[0]
    if latents_only:
        nrows = lt - CTX_LEN
        tm = FFN_TILE_LATENT if nrows % FFN_TILE_LATENT == 0 else ROW_TILE
        in_row = lambda w: pl.BlockSpec((pl.Element(tm), pl.Element(w)),
                                        lambda i: (pl.multiple_of(CTX_LEN + i * tm, ROW_TILE), 0))
    else:
        nrows = lt
        tm = FFN_TILE if lt % FFN_TILE == 0 else ROW_TILE
        in_row = lambda w: pl.BlockSpec((tm, w), lambda i: (i, 0))
    single = lambda shape: pl.BlockSpec(shape, lambda i: (0, 0), pipeline_mode=pl.Buffered(1))
    mix_args, mix_specs = (), []
    if odd_mixer is not None:
        att, yf, yr, z, g_norm, w_out = odd_mixer
        mix_args = (att, yf, yr, z, g_norm.reshape(1, -1), w_out.astype(BF16))
        mix_specs = [in_row(a.shape[1]) for a in mix_args[:4]] + [_const_spec((1, SSD_INNER)), single(w_out.shape)]
    return pl.pallas_call(
        functools.partial(_ffn_kernel, row_offset=lt - nrows, odd_mixer=odd_mixer is not None),
        grid=(nrows // tm,),
        in_specs=[in_row(D_MODEL)] + mix_specs + [_const_spec(modx.shape), _const_spec(modc.shape), single(w1.shape),
                                                  single(w3.shape), single(w2.shape)],
        out_specs=pl.BlockSpec((tm, D_MODEL), lambda i: (i, 0)),
        out_shape=jax.ShapeDtypeStruct((nrows, D_MODEL), F32),
        compiler_params=_cparams(("arbitrary",)),
    )(xs, *mix_args, modx, modc, w1.astype(BF16), w3.astype(BF16), w2.astype(BF16))


ODD_COLS = 2048 + SSD_XBC + LANES


def _odd_in_kernel(x_ref, modx_ref, modc_ref, win_ref, gq_ref, gk_ref, ee_ref, rrow_ref, rcol_ref,
                   q_ref, k_ref, v_ref, z_ref, xbc_ref, dt_ref):
    i = pl.program_id(0)
    is_ctx = i == 0
    sh = jnp.where(is_ctx, modc_ref[0:1, :], modx_ref[0:1, :])
    sc = jnp.where(is_ctx, modc_ref[1:2, :], modx_ref[1:2, :])
    h = (_rms(x_ref[...]) * (1.0 + sc) + sh).astype(BF16)
    proj = _dot(h, win_ref[...])
    ones = jnp.ones((ROW_TILE, DIFF_V), BF16)
    for hd in range(DIFF_HEADS):
        v_ref[:, 2 * hd * DIFF_V:(2 * hd + 1) * DIFF_V] = proj[:, 1024 + hd * DIFF_V:1024 + (hd + 1) * DIFF_V].astype(BF16)
        v_ref[:, (2 * hd + 1) * DIFF_V:(2 * hd + 2) * DIFF_V] = ones
    z_ref[...] = proj[:, 1536:2048]
    xbc_ref[...] = proj[:, 2048:2048 + SSD_XBC]
    dt_ref[...] = proj[:, 2048 + SSD_XBC:ODD_COLS]
    nrep = GROUP_WIDTH // LANES
    cos = jnp.concatenate([_rope_tile(rrow_ref, rcol_ref, 0, is_ctx, True)] * nrep, axis=-1)
    sina = jnp.concatenate([_rope_tile(rrow_ref, rcol_ref, 1, is_ctx, False)] * nrep, axis=-1)
    sinb = jnp.concatenate([_rope_tile(rrow_ref, rcol_ref, 2, is_ctx, False)] * nrep, axis=-1)
    half = DIFF_QK // 4
    qscale = DIFF_QK ** -0.5 * LOG2E

    def prep(t, g_ref, scale):
        ss = _split_dot(t * t, ee_ref[...], 2)
        tn = t * lax.rsqrt(ss * (1.0 / DIFF_QK) + EPS) * g_ref[...]
        up = pltpu.roll(tn, GROUP_WIDTH - half, axis=1)
        dn = pltpu.roll(tn, half, axis=1)
        return ((tn * cos + up * sina + dn * sinb) * scale).astype(BF16)

    q_ref[...] = prep(proj[:, 0:512], gq_ref, qscale)
    k_ref[...] = prep(proj[:, 512:1024], gk_ref, 1.0)


def _odd_in(xs, modx, modc, w_in, g_q, g_k, rope_row, rope_col):
    lt = xs.shape[0]
    win = jnp.pad(w_in, ((0, 0), (0, ODD_COLS - w_in.shape[1]))).astype(BF16)
    nblk = GROUP_WIDTH // DIFF_QK
    blk_id = np.arange(GROUP_WIDTH) // DIFF_QK
    ee = jnp.asarray(blk_id[:, None] == blk_id[None, :], BF16)
    gq = jnp.tile(g_q, nblk).reshape(1, -1)
    gk = jnp.tile(g_k, nblk).reshape(1, -1)
    args = (xs, modx, modc, win, gq, gk, ee, rope_row, rope_col)
    row = lambda w: pl.BlockSpec((ROW_TILE, w), lambda i: (i, 0))
    in_specs = ([row(D_MODEL)] + [_const_spec(a.shape) for a in args[1:7]]
                + [pl.BlockSpec((1,) + rope_row.shape[1:], lambda i: (i, 0, 0)), _const_spec(rope_col.shape)])
    return pl.pallas_call(
        _odd_in_kernel,
        grid=(lt // ROW_TILE,),
        in_specs=in_specs,
        out_specs=[row(512), row(512), row(1024), row(512), row(SSD_XBC), row(LANES)],
        out_shape=[jax.ShapeDtypeStruct((lt, 512), BF16), jax.ShapeDtypeStruct((lt, 512), BF16),
                   jax.ShapeDtypeStruct((lt, 1024), BF16), jax.ShapeDtypeStruct((lt, 512), F32),
                   jax.ShapeDtypeStruct((lt, SSD_XBC), F32), jax.ShapeDtypeStruct((lt, LANES), F32)],
        compiler_params=_cparams(("arbitrary",)),
    )(*args)


def _conv_kernel(prev_ref, cur_ref, next_ref, w_ref, b_ref, o_ref, ext_scr, *, ntiles):
    i = pl.program_id(0)
    pad = SSD_CONV // 2
    has_prev = i >= 2
    has_next = jnp.logical_and(i >= 1, i < ntiles - 1)
    ext_scr[0:8, :] = jnp.where(has_prev, prev_ref[...], 0.0)
    ext_scr[8:8 + ROW_TILE, :] = cur_ref[...]
    ext_scr[8 + ROW_TILE:16 + ROW_TILE, :] = jnp.where(has_next, next_ref[...], 0.0)
    acc = jnp.zeros((ROW_TILE, SSD_XBC), F32) + b_ref[...]
    for k in range(SSD_CONV):
        acc = acc + ext_scr[pl.ds(8 - pad + k, ROW_TILE), :] * w_ref[k:k + 1, :]
    o_ref[...] = _silu(acc)


def _ssd_conv(xbc, conv_w, conv_b):
    lt = xbc.shape[0]
    ntiles = lt // ROW_TILE
    per = ROW_TILE // 8
    last8 = lt // 8 - 1
    w = jnp.pad(conv_w, ((0, 8 - SSD_CONV), (0, 0)))
    return pl.pallas_call(
        functools.partial(_conv_kernel, ntiles=ntiles),
        grid=(ntiles,),
        in_specs=[
            pl.BlockSpec((8, SSD_XBC), lambda i: (jnp.maximum(i * per - 1, 0), 0)),
            pl.BlockSpec((ROW_TILE, SSD_XBC), lambda i: (i, 0)),
            pl.BlockSpec((8, SSD_XBC), lambda i: (jnp.minimum((i + 1) * per, last8), 0)),
            _const_spec((8, SSD_XBC)),
            _const_spec((1, SSD_XBC)),
        ],
        out_specs=pl.BlockSpec((ROW_TILE, SSD_XBC), lambda i: (i, 0)),
        out_shape=jax.ShapeDtypeStruct((lt, SSD_XBC), F32),
        scratch_shapes=[pltpu.VMEM((ROW_TILE + 16, SSD_XBC), F32)],
        compiler_params=_cparams(("arbitrary",)),
    )(xbc, xbc, xbc, w, conv_b.reshape(1, -1))


def _split_parts(x, parts):
    out = []
    for _ in range(parts):
        piece = x.astype(BF16)
        out.append(piece)
        x = x - piece.astype(F32)
    return out


def _ssd_direction(xc, dtr, dtr_t, bias, alog, bias_t, alog_t, sel, dskip, state_ref, rev):
    q = SSD_CHUNK
    hp = SSD_HEAD_DIM
    dt = _softplus(dtr + bias)
    da = dt * -jnp.exp(alog)
    dt_t = _softplus(dtr_t + bias_t)
    da_t = dt_t * -jnp.exp(alog_t)
    r_idx = lax.broadcasted_iota(jnp.int32, (q, q), 0)
    c_idx = lax.broadcasted_iota(jnp.int32, (q, q), 1)
    keep = c_idx >= r_idx if rev else c_idx <= r_idx
    tri = jnp.where(keep, 1.0, 0.0).astype(BF16)
    nt = (((1,), (1,)), ((), ()))
    cs = sum(_dot(tri, p) for p in _split_parts(da, 3))
    cs_t = sum(lax.dot_general(p, tri, nt, preferred_element_type=F32) for p in _split_parts(da_t, 3))
    csb = sum(_dot(p, sel) for p in _split_parts(cs, 3))
    dtb = sum(_dot(p, sel) for p in _split_parts(dt, 2))
    tot = csb[0:1, :] if rev else csb[q - 1:q, :]
    x = xc[:, 0:SSD_INNER]
    xdt = x * dtb
    xdt_b = xdt.astype(BF16)
    xdec = (xdt * jnp.exp(tot - csb)).astype(BF16)
    csb_up = pltpu.roll(csb, SSD_INNER - hp, axis=1)
    csb_dn = pltpu.roll(csb, hp, axis=1)
    low = lax.broadcasted_iota(jnp.int32, (1, 2 * hp), 1) < hp
    gw = SSD_GROUPS * SSD_STATE
    bm = xc[:, SSD_INNER:SSD_INNER + gw].astype(BF16)
    cm = xc[:, SSD_INNER + gw:SSD_INNER + 2 * gw].astype(BF16)
    hpg = SSD_HEADS // SSD_GROUPS
    head_row = SSD_HEADS if rev else 0
    outs = []
    for g in range(SSD_GROUPS):
        bg = bm[:, g * SSD_STATE:(g + 1) * SSD_STATE]
        cg = cm[:, g * SSD_STATE:(g + 1) * SSD_STATE]
        scores = lax.dot_general(cg, bg, nt, preferred_element_type=F32)
        cols = slice(g * hpg * hp, (g + 1) * hpg * hp)
        s_in = state_ref[:, cols]
        y_off = _dot(cg, s_in.astype(BF16))
        pairs = []
        for pp in range(hpg // 2):
            blk = slice((g * hpg + 2 * pp) * hp, (g * hpg + 2 * pp + 2) * hp)
            here, up, dn = csb[:, blk], csb_up[:, blk], csb_dn[:, blk]
            ys = []
            for e in range(2):
                hd = g * hpg + 2 * pp + e
                col = jnp.where(low, here, dn) if e == 0 else jnp.where(low, up, here)
                row = cs_t[head_row + hd:head_row + hd + 1, :]
                lmat = jnp.exp(jnp.where(keep, col - row, -jnp.inf))
                ys.append(_dot((scores * lmat).astype(BF16), xdt_b[:, blk]))
            pairs.append(jnp.where(low, ys[0], ys[1]))
        y = jnp.concatenate(pairs, axis=-1) + y_off * jnp.exp(csb[:, cols])
        if dskip is not None:
            y = y + x[:, cols] * dskip[:, cols]
        outs.append(y)
        new = lax.dot_general(bg, xdec[:, cols], (((0,), (0,)), ((), ())), preferred_element_type=F32)
        state_ref[:, cols] = s_in * jnp.exp(tot[:, cols]) + new
    return jnp.concatenate(outs, axis=-1)


def _ssd_kernel(xf_ref, dtf_ref, dtft_ref, xr_ref, dtr_ref, dtrt_ref, bias_ref, alog_ref, biast_ref,
                alogt_ref, sel_ref, dskip_ref, yf_ref, yr_ref, sf_scr, sr_scr):
    @pl.when(pl.program_id(0) == 0)
    def _():
        sf_scr[...] = jnp.zeros_like(sf_scr)
        sr_scr[...] = jnp.zeros_like(sr_scr)

    consts = (bias_ref[...], alog_ref[...], biast_ref[...], alogt_ref[...])
    per_block = ROW_TILE // SSD_CHUNK
    for c in range(per_block):
        rf = slice(c * SSD_CHUNK, (c + 1) * SSD_CHUNK)
        cr = per_block - 1 - c
        rr = slice(cr * SSD_CHUNK, (cr + 1) * SSD_CHUNK)
        yf_ref[rf, :] = _ssd_direction(xf_ref[rf, :], dtf_ref[rf, :], dtft_ref[0, c], *consts, sel_ref[0],
                                       dskip_ref[...], sf_scr, False)
        yr_ref[rr, :] = _ssd_direction(xr_ref[rr, :], dtr_ref[rr, :], dtrt_ref[0, cr], *consts, sel_ref[1], None,
                                       sr_scr, True)


def _ssd_scan(xc, dtr, dt_bias, a_log, d_skip):
    lt = xc.shape[0]
    nblk = lt // ROW_TILE
    per_block = ROW_TILE // SSD_CHUNK
    ndt = 2 * SSD_HEADS
    dtr_t = dtr[:, :ndt].reshape(nblk, per_block, SSD_CHUNK, ndt).transpose(0, 1, 3, 2)

    def rev_block(k):
        return jnp.where(k == 0, 0, nblk - k)

    fwd = lambda w: pl.BlockSpec((ROW_TILE, w), lambda k: (k, 0))
    bwd = lambda w: pl.BlockSpec((ROW_TILE, w), lambda k: (rev_block(k), 0))
    dtt = lambda index: pl.BlockSpec((1, per_block, ndt, SSD_CHUNK), lambda k: (index(k), 0, 0, 0))
    lanes = lambda v: jnp.pad(v.reshape(1, ndt), ((0, 0), (0, LANES - ndt)))
    rows = lambda v: jnp.broadcast_to(v.reshape(ndt, 1), (ndt, SSD_CHUNK))
    sel = np.zeros((2, LANES, SSD_INNER), np.float32)
    for d in range(2):
        for hd in range(SSD_HEADS):
            sel[d, d * SSD_HEADS + hd, hd * SSD_HEAD_DIM:(hd + 1) * SSD_HEAD_DIM] = 1.0
    small = (lanes(dt_bias), lanes(a_log), rows(dt_bias), rows(a_log), jnp.asarray(sel, BF16),
             jnp.repeat(d_skip, SSD_HEAD_DIM).reshape(1, -1))
    return pl.pallas_call(
        _ssd_kernel,
        grid=(nblk,),
        in_specs=[fwd(SSD_XBC), fwd(LANES), dtt(lambda k: k), bwd(SSD_XBC), bwd(LANES), dtt(rev_block)]
                 + [_const_spec(a.shape) for a in small],
        out_specs=[fwd(SSD_INNER), bwd(SSD_INNER)],
        out_shape=[jax.ShapeDtypeStruct((lt, SSD_INNER), F32)] * 2,
        scratch_shapes=[pltpu.VMEM((SSD_STATE, SSD_INNER), F32)] * 2,
        compiler_params=_cparams(("arbitrary",)),
    )(xc, dtr, dtr_t, xc, dtr, dtr_t, *small)


ROWS_PER_TILE = ROW_TILE // GRID_W


def _rope_tables(seq):
    ng = seq // GRID_W
    nctx = CTX_LEN // GRID_W

    def trig(n, count):
        inv = ROPE_THETA ** (-jnp.arange(n, dtype=F32) / n)
        ang = jnp.arange(count, dtype=F32)[:, None] * inv[None, :]
        return jnp.cos(ang), jnp.sin(ang)

    def lay(count, parts):
        return jnp.concatenate([p if hasattr(p, "shape") else jnp.full((count, p[0]), p[1], F32) for p in parts],
                               axis=1)

    def row_slabs(tabs, idents):
        full = [jnp.concatenate([jnp.full((nctx, LANES), ident, F32), t], axis=0) for t, ident in zip(tabs, idents)]
        packed = jnp.concatenate(full, axis=1).reshape(-1, ROWS_PER_TILE, len(tabs) * LANES)
        return jnp.pad(packed, ((0, 0), (0, 8 - ROWS_PER_TILE), (0, 0)))

    n0 = MLA_ROPE // 4
    cr, sr = trig(n0, ng)
    cc, sc = trig(n0, GRID_W)
    tail = LANES - MLA_QK
    row0 = row_slabs([lay(ng, [(MLA_NOPE, 1.0), cr, cr, (2 * n0, 1.0), (tail, 1.0)]),
                      lay(ng, [(MLA_NOPE, 0.0), -sr, sr, (2 * n0, 0.0), (tail, 0.0)])], (1.0, 0.0))
    col0 = jnp.concatenate([lay(GRID_W, [(MLA_NOPE, 1.0), (2 * n0, 1.0), cc, cc, (tail, 1.0)]),
                            lay(GRID_W, [(MLA_NOPE, 0.0), (2 * n0, 0.0), -sc, sc, (tail, 0.0)])], axis=1)
    n1 = DIFF_QK // 4
    cr, sr = trig(n1, ng)
    cc, sc = trig(n1, GRID_W)
    row1 = row_slabs([lay(ng, [cr, cr, (2 * n1, 1.0)] * 2),
                      lay(ng, [-sr, (n1, 0.0), (2 * n1, 0.0)] * 2),
                      lay(ng, [(n1, 0.0), sr, (2 * n1, 0.0)] * 2)], (1.0, 0.0, 0.0))
    col1 = jnp.concatenate([lay(GRID_W, [(2 * n1, 1.0), cc, cc] * 2),
                            lay(GRID_W, [(2 * n1, 0.0), -sc, (n1, 0.0)] * 2),
                            lay(GRID_W, [(2 * n1, 0.0), (n1, 0.0), sc] * 2)], axis=1)
    return row0, col0, row1, col1


def _rope_tile(row_ref, col_ref, k, is_ctx, product):
    r = row_ref[0, :, k * LANES:(k + 1) * LANES]
    c = jnp.where(is_ctx, 1.0 if product else 0.0, col_ref[:, k * LANES:(k + 1) * LANES])
    rb = jnp.concatenate([jnp.broadcast_to(r[j:j + 1], (GRID_W, LANES)) for j in range(ROWS_PER_TILE)], axis=0)
    cb = jnp.concatenate([c] * ROWS_PER_TILE, axis=0)
    return rb * cb if product else rb + cb


def kernel(x, c, ctx, c_ctx, ada_w, ada_b, ffn_w1, ffn_w3, ffn_w2, e_w_in, e_w_out, mla_g_qa, mla_w_qb, mla_g_kva, mla_w_kvb, mla_g_qn, mla_g_kn, s5_lam_re, s5_lam_im, s5_log_dt, s5_b_re, s5_b_im, s5_c_re, s5_c_im, s5_d, s5_w_glu, s5_b_glu, o_w_in, o_w_out, diff_g_q, diff_g_k, diff_lq1, diff_lk1, diff_lq2, diff_lk2, diff_g_o, ssd_conv_w, ssd_conv_b, ssd_dt_bias, ssd_a_log, ssd_d, ssd_g):
    depth = ada_w.shape[0]
    seq = x.shape[1]
    assert x.shape[0] == 1 and ctx.shape[1] == CTX_LEN and seq % ROW_TILE == 0
    lt = seq + CTX_LEN
    xs = None
    mods = _mod_vectors(c, c_ctx, ada_w, ada_b)
    rope_row0, rope_col0, rope_row1, rope_col1 = _rope_tables(seq)
    for i in range(depth):
        j = i // 2
        modx, modc = mods[i, 0], mods[i, 1]
        if xs is None and i % 2 == 1:
            xs = jnp.concatenate([ctx[0], x[0]], axis=0)
        if i % 2 == 0:
            stream = (x[0], ctx[0], 0) if xs is None else (xs, xs, 1)
            q, k, v, u = _even_in(stream, lt, modx, modc, e_w_in[j], mla_g_qa[j], mla_w_qb[j], mla_g_kva[j],
                                  mla_w_kvb[j], mla_g_qn[j], mla_g_kn[j], rope_row0, rope_col0)
            att = _mla_attention(q, k, v)
            mats = _s5_matrices(s5_lam_re[j], s5_lam_im[j], s5_log_dt[j], s5_b_re[j], s5_b_im[j],
                                s5_c_re[j], s5_c_im[j], s5_d[j])
            y = _s5_mixer(u, mats)
            xs = _even_out(stream, att, y, modx, modc, s5_w_glu[j], s5_b_glu[j], e_w_out[j])
        else:
            q, k, v, z, xbc, dtr = _odd_in(xs, modx, modc, o_w_in[j], diff_g_q[j], diff_g_k[j],
                                           rope_row1, rope_col1)
            lam_init = 0.8 - 0.6 * math.exp(-0.3 * i)
            lam = (jnp.exp(jnp.sum(diff_lq1[j] * diff_lk1[j])) - jnp.exp(jnp.sum(diff_lq2[j] * diff_lk2[j]))
                   + lam_init)
            att = _diff_attention(q, k, v, jnp.full((1, LANES), lam, F32), diff_g_o[j].reshape(1, -1),
                                  1.0 - lam_init)
            xc = _ssd_conv(xbc, ssd_conv_w[j], ssd_conv_b[j])
            yf, yr = _ssd_scan(xc, dtr, ssd_dt_bias[j], ssd_a_log[j], ssd_d[j])
            odd_mixer = (att, yf, yr, z, ssd_g[j], o_w_out[j])
        last = i == depth - 1
        if i % 2 == 1 and not last:
            xs = _odd_out(xs, att, yf, yr, z, modx, modc, ssd_g[j], o_w_out[j])
        xs = _ffn(xs, modx, modc, ffn_w1[i], ffn_w3[i], ffn_w2[i], latents_only=last,
                  odd_mixer=odd_mixer if (i % 2 == 1 and last) else None)
    return xs[None]
```

```python
import functools
import math

import numpy as np
import jax
import jax.numpy as jnp
from jax import lax
from jax.experimental import pallas as pl
from jax.experimental.pallas import tpu as pltpu

F32 = jnp.float32
BF16 = jnp.bfloat16
HIGHEST = lax.Precision.HIGHEST

D_MODEL = 1024
CTX_LEN = 256
GRID_W = 64
GROUP_WIDTH = 512
D_FF = 2816
EPS = 1e-6
ROPE_THETA = 10000.0
LOG2E = math.log2(math.e)

MLA_HEADS, MLA_NOPE, MLA_ROPE, MLA_V = 8, 64, 32, 64
MLA_QK = MLA_NOPE + MLA_ROPE
MLA_Q_RANK, MLA_KV_RANK = 384, 256
S5_CH, S5_GROUPS, S5_STATE = 16, 32, 64
S5_T = 16
S5_GPB = 8
DIFF_HEADS, DIFF_QK, DIFF_V = 4, 64, 128
SSD_HEADS, SSD_HEAD_DIM, SSD_GROUPS, SSD_STATE, SSD_CONV, SSD_CHUNK = 8, 64, 2, 128, 5, 128
SSD_INNER = 512
SSD_XBC = SSD_INNER + 2 * SSD_GROUPS * SSD_STATE

LANES = 128
ROW_TILE = 256
FFN_TILE = 1280
FFN_TILE_LATENT = 512
FFN_CHUNK = 256
KV_UNROLL = 4
KV_UNROLL_FULL = 12
VMEM_LIMIT = 56 * 1024 * 1024


def _cparams(sem):
    return pltpu.CompilerParams(dimension_semantics=sem, vmem_limit_bytes=VMEM_LIMIT)


def _dot(a, b):
    return jnp.dot(a, b, preferred_element_type=F32)


def _split_dot(x, w, parts):
    acc = None
    for _ in range(parts):
        piece = x.astype(BF16)
        term = _dot(piece, w)
        acc = term if acc is None else acc + term
        x = x - piece.astype(F32)
    return acc


def _rms(x):
    return x * lax.rsqrt(jnp.mean(x * x, axis=-1, keepdims=True) + EPS)


def _sigmoid(x):
    return 1.0 / (1.0 + jnp.exp(-x))


def _silu(x):
    return x * _sigmoid(x)


def _gelu_tanh(x):
    return 0.5 * x * (1.0 + jnp.tanh(math.sqrt(2.0 / math.pi) * (x + 0.044715 * (x * x * x))))


def _softplus(x):
    return jnp.maximum(x, 0.0) + jnp.log(1.0 + jnp.exp(-jnp.abs(x)))


def _const_spec(shape):
    nd = len(shape)
    return pl.BlockSpec(shape, lambda *_: (0,) * nd)


def _mod_kernel(c_ref, w_ref, b_ref, o_ref):
    s = _silu(c_ref[...]).astype(BF16)
    o_ref[0] = _dot(s, w_ref[0].astype(BF16)) + b_ref[0]


def _mod_vectors(c, c_ctx, ada_w, ada_b):
    depth = ada_w.shape[0]
    cc = jnp.zeros((8, D_MODEL), F32).at[0].set(c[0]).at[1].set(c_ctx)
    nblk = 6
    out = pl.pallas_call(
        _mod_kernel,
        grid=(depth, nblk),
        in_specs=[
            pl.BlockSpec((8, D_MODEL), lambda i, j: (0, 0)),
            pl.BlockSpec((1, D_MODEL, D_MODEL), lambda i, j: (i, 0, j)),
            pl.BlockSpec((1, 1, D_MODEL), lambda i, j: (i, 0, j)),
        ],
        out_specs=pl.BlockSpec((1, 8, D_MODEL), lambda i, j: (i, 0, j)),
        out_shape=jax.ShapeDtypeStruct((depth, 8, 6 * D_MODEL), F32),
        compiler_params=_cparams(("arbitrary", "arbitrary")),
    )(cc, ada_w, ada_b.reshape(depth, 1, 6 * D_MODEL))
    mods = out[:, :2].reshape(depth, 2, 6, D_MODEL)
    return jnp.pad(mods, ((0, 0), (0, 0), (0, 2), (0, 0)))


def _mod_rows(modx_ref, modc_ref, k, row0, nrows):
    rows = row0 + lax.broadcasted_iota(jnp.int32, (nrows, 1), 0)
    return jnp.where(rows < CTX_LEN, modc_ref[k:k + 1, :], modx_ref[k:k + 1, :])


def _rope_partner(n_half):
    return np.array([i + n_half if i < n_half else i - n_half for i in range(2 * n_half)])


def _stream_specs(stream):
    lat, crows, first = stream
    specs = [pl.BlockSpec((ROW_TILE, D_MODEL), lambda i: (jnp.maximum(i - 1 + first, 0), 0)),
             pl.BlockSpec((ROW_TILE, D_MODEL), lambda i: (0, 0))]
    return [lat, crows], specs


def _even_in_kernel(x_ref, c_ref, modx_ref, modc_ref, win_ref, gqa_ref, wq_ref, wqs_ref, gkva_ref, wk_ref,
                    wv_ref, gq_ref, gqs_ref, gk_ref, gks_ref, rrow_ref, rcol_ref,
                    q_ref, k_ref, v_ref, u_ref):
    i = pl.program_id(0)
    is_ctx = i == 0
    sh = jnp.where(is_ctx, modc_ref[0:1, :], modx_ref[0:1, :])
    sc = jnp.where(is_ctx, modc_ref[1:2, :], modx_ref[1:2, :])
    xt = jnp.where(is_ctx, c_ref[...], x_ref[...])
    h = (_rms(xt) * (1.0 + sc) + sh).astype(BF16)
    proj = _dot(h, win_ref[...])
    cq = proj[:, 0:384]
    ckv = proj[:, 384:640]
    u_ref[...] = proj[:, 640:1152]
    krb = proj[:, 1152:1280]
    krs = proj[:, 1280:1408]
    lane = lax.broadcasted_iota(jnp.int32, (1, LANES), 1)
    cqn = (_rms(cq) * gqa_ref[...]).astype(BF16)
    qf = _dot(cqn, wq_ref[...])
    qs = _dot(cqn, wqs_ref[...])
    ckvn = (_rms(ckv) * gkva_ref[...]).astype(BF16)
    kf = _dot(ckvn, wk_ref[...])
    vf = _dot(ckvn, wv_ref[...])
    for pr in range(MLA_HEADS // 2):
        vp = vf[:, pr * LANES:(pr + 1) * LANES]
        v_ref[:, (2 * pr) * LANES:(2 * pr + 1) * LANES] = jnp.where(lane < MLA_V, vp, 1.0).astype(BF16)
        v_ref[:, (2 * pr + 1) * LANES:(2 * pr + 2) * LANES] = jnp.where(lane < MLA_V, 1.0, vp).astype(BF16)
    cos = _rope_tile(rrow_ref, rcol_ref, 0, is_ctx, True)
    sin = _rope_tile(rrow_ref, rcol_ref, 1, is_ctx, False)
    qscale = MLA_QK ** -0.5 * LOG2E
    inv_n = 1.0 / MLA_QK
    for hd in range(MLA_HEADS):
        blk = slice(hd * LANES, (hd + 1) * LANES)
        qh = qf[:, blk]
        rq = lax.rsqrt(jnp.sum(qh * qh, axis=-1, keepdims=True) * inv_n + EPS) * qscale
        qo = (qh * (gq_ref[...] * cos) + qs[:, blk] * (gqs_ref[...] * sin)) * rq
        q_ref[:, blk] = qo.astype(BF16)
        kh = kf[:, blk] + krb
        rk = lax.rsqrt(jnp.sum(kh * kh, axis=-1, keepdims=True) * inv_n + EPS)
        ko = (kh * (gk_ref[...] * cos) + krs * (gks_ref[...] * sin)) * rk
        k_ref[:, blk] = ko.astype(BF16)


def _even_in(stream, lt, modx, modc, w_in, g_qa, w_qb, g_kva, w_kvb, g_qn, g_kn, rope_row, rope_col):
    perm = _rope_partner(MLA_ROPE // 4)
    perm = np.concatenate([perm, perm + MLA_ROPE // 2])
    cq_w, ckv_w = w_in[:, :384], w_in[:, 384:640]
    kr_w, u_w = w_in[:, 640:672], w_in[:, 672:]
    zblk = jnp.zeros((D_MODEL, LANES), F32)
    krblk = zblk.at[:, MLA_NOPE:MLA_QK].set(kr_w)
    krsblk = zblk.at[:, MLA_NOPE:MLA_QK].set(kr_w[:, perm])
    win = jnp.concatenate([cq_w, ckv_w, u_w, krblk, krsblk], axis=1).astype(BF16)
    wq3 = w_qb.reshape(MLA_Q_RANK, MLA_HEADS, MLA_QK)
    wq = jnp.zeros((MLA_Q_RANK, MLA_HEADS, LANES), F32).at[:, :, :MLA_QK].set(wq3)
    wqs = jnp.zeros((MLA_Q_RANK, MLA_HEADS, LANES), F32).at[:, :, MLA_NOPE:MLA_QK].set(
        wq3[:, :, MLA_NOPE:][:, :, perm])
    wkv3 = w_kvb.reshape(MLA_KV_RANK, MLA_HEADS, MLA_NOPE + MLA_V)
    wk = jnp.zeros((MLA_KV_RANK, MLA_HEADS, LANES), F32).at[:, :, :MLA_NOPE].set(wkv3[:, :, :MLA_NOPE])
    wv = wkv3[:, :, MLA_NOPE:].reshape(MLA_KV_RANK, MLA_HEADS * MLA_V)

    def pad_gain(g):
        gp = jnp.zeros((1, LANES), F32).at[0, :MLA_QK].set(g)
        gs = jnp.zeros((1, LANES), F32).at[0, MLA_NOPE:MLA_QK].set(g[MLA_NOPE:][perm])
        return gp, gs

    gq, gqs = pad_gain(g_qn)
    gk, gks = pad_gain(g_kn)
    hw = MLA_HEADS * LANES
    consts = (modx, modc, win, g_qa.reshape(1, -1), wq.reshape(MLA_Q_RANK, hw).astype(BF16),
              wqs.reshape(MLA_Q_RANK, hw).astype(BF16), g_kva.reshape(1, -1),
              wk.reshape(MLA_KV_RANK, hw).astype(BF16), wv.astype(BF16), gq, gqs, gk, gks)
    row = lambda w: pl.BlockSpec((ROW_TILE, w), lambda i: (i, 0))
    rows, row_specs = _stream_specs(stream)
    args = (*rows, *consts, rope_row, rope_col)
    in_specs = (row_specs + [_const_spec(a.shape) for a in consts]
                + [pl.BlockSpec((1,) + rope_row.shape[1:], lambda i: (i, 0, 0)), _const_spec(rope_col.shape)])
    return pl.pallas_call(
        _even_in_kernel,
        grid=(lt // ROW_TILE,),
        in_specs=in_specs,
        out_specs=[row(hw), row(hw), row(hw), row(GROUP_WIDTH)],
        out_shape=[jax.ShapeDtypeStruct((lt, hw), BF16), jax.ShapeDtypeStruct((lt, hw), BF16),
                   jax.ShapeDtypeStruct((lt, hw), BF16),
                   jax.ShapeDtypeStruct((lt, GROUP_WIDTH), F32)],
        compiler_params=_cparams(("arbitrary",)),
    )(*args)


def _kv_chunk(lt):
    for tk in (1280, 640, 256):
        if lt % tk == 0:
            return tk
    raise ValueError(lt)


def _scores(q, k_ref, start, size):
    kc = k_ref[pl.ds(start, size), :]
    return lax.dot_general(q, kc, (((1,), (1,)), ((), ())), preferred_element_type=F32)


def _softmax_update(s_ref, v_ref, start, size, carry, row_max=None):
    m, acc = carry
    if row_max is None:
        row_max = jnp.max(s_ref[...], axis=-1, keepdims=True)
    mn = jnp.maximum(m, row_max)
    alpha = jnp.exp2(m - mn)
    p = jnp.exp2(s_ref[...] - mn)
    acc = alpha * acc + _dot(p.astype(BF16), v_ref[pl.ds(start, size), :])
    return mn, acc


def _attend(streams, s_scr, is_ctx, lt, tk, unroll=KV_UNROLL):
    tq = streams[0][0].shape[0]
    n = lt // tk
    init = tuple((jnp.full((tq, 1), -jnp.inf, F32), jnp.zeros((tq, v_ref.shape[-1]), F32), jnp.zeros((tq, 1), F32))
                 for _, _, _, v_ref in streams)
    slot_of = lambda chunk: 2 if chunk == 0 else chunk % 2

    def issue_next(t, q_next, k_ref):
        s_scr[3 * t + 2] = _scores(q_next, k_ref, 0, tk)

    def consume(chunk, slot, carries, last):
        start = chunk * tk
        if not isinstance(start, int):
            start = pl.multiple_of(start, tk)
        out = []
        for t, ((q, q_next, k_ref, v_ref), (m, acc, ahead_max)) in enumerate(zip(streams, carries)):
            here_max = None if slot == 2 else ahead_max
            if last and slot != 2:
                issue_next(t, q_next, k_ref)
            elif not last:
                ahead = _scores(q, k_ref, start + tk, tk)
                s_scr[3 * t + (1 - slot if slot < 2 else 1)] = ahead
                ahead_max = jnp.max(ahead, axis=-1, keepdims=True)
            out.append(_softmax_update(s_scr.at[3 * t + slot], v_ref, start, tk, (m, acc), here_max) + (ahead_max,))
            if last and slot == 2:
                issue_next(t, q_next, k_ref)
        return tuple(out)

    def group(i, carries):
        for u in range(unroll):
            carries = consume(1 + unroll * i + u, (1 + u) % 2, carries, False)
        return carries

    def ctx_branch():
        out = []
        for t, ((q, q_next, k_ref, v_ref), (m, acc, ahead_max)) in enumerate(zip(streams, init)):
            issue_next(t, q_next, k_ref)
            s_scr[3 * t, :, 0:CTX_LEN] = _scores(q, k_ref, 0, CTX_LEN)
            out.append(_softmax_update(s_scr.at[3 * t, :, 0:CTX_LEN], v_ref, 0, CTX_LEN, (m, acc)) + (ahead_max,))
        return tuple(out)

    def full_branch():
        carries = consume(0, 2, init, n == 1)
        ngroups = max(n - 2, 0) // unroll
        if ngroups:
            carries = lax.fori_loop(0, ngroups, group, carries)
        for chunk in range(1 + unroll * ngroups, n):
            carries = consume(chunk, slot_of(chunk), carries, chunk == n - 1)
        return carries

    out = lax.cond(is_ctx, ctx_branch, full_branch)
    return [acc for (_, acc, _) in out]


def _mla_attn_kernel(q_ref, qn_ref, k_ref, v_ref, o_ref, s_scr, *, lt, tk):
    is_ctx = pl.program_id(1) == 0
    lane = lax.broadcasted_iota(jnp.int32, (1, LANES), 1)
    blks = [slice(hh * LANES, (hh + 1) * LANES) for hh in range(2)]
    acc_a, acc_b = _attend([(q_ref[:, b], qn_ref[:, b], k_ref.at[:, b], v_ref.at[:, b]) for b in blks],
                           s_scr, is_ctx, lt, tk, unroll=KV_UNROLL_FULL)
    o_ref[...] = jnp.where(lane < MLA_V, acc_a / acc_a[:, MLA_V:MLA_V + 1], acc_b / acc_b[:, 0:1])


def _mla_attention(q, k, v):
    lt = q.shape[0]
    tk = _kv_chunk(lt)
    npair = MLA_HEADS // 2
    last = lt // ROW_TILE - 1
    return pl.pallas_call(
        functools.partial(_mla_attn_kernel, lt=lt, tk=tk),
        grid=(npair, lt // ROW_TILE),
        in_specs=[
            pl.BlockSpec((ROW_TILE, 2 * LANES), lambda p, i: (i, p)),
            pl.BlockSpec((ROW_TILE, 2 * LANES), lambda p, i: (jnp.minimum(i + 1, last), p)),
            pl.BlockSpec((lt, 2 * LANES), lambda p, i: (0, p), pipeline_mode=pl.Buffered(1)),
            pl.BlockSpec((lt, 2 * LANES), lambda p, i: (0, p), pipeline_mode=pl.Buffered(1)),
        ],
        out_specs=pl.BlockSpec((ROW_TILE, LANES), lambda p, i: (i, p)),
        out_shape=jax.ShapeDtypeStruct((lt, GROUP_WIDTH), F32),
        scratch_shapes=[pltpu.VMEM((6, ROW_TILE, tk), F32)],
        compiler_params=_cparams(("arbitrary", "arbitrary")),
    )(q, q, k, v)


def _diff_attn_kernel(q_ref, qn_ref, k_ref, v_ref, lam_ref, go_ref, o_ref, s_scr, *, lt, tk, out_scale):
    is_ctx = pl.program_id(1) == 0
    lane = lax.broadcasted_iota(jnp.int32, (1, LANES), 1)
    zero = jnp.zeros((ROW_TILE, LANES), BF16)
    first = lambda q: jnp.where(lane < DIFF_QK, q, zero)
    second = lambda q: jnp.where(lane < DIFF_QK, zero, q)
    q, qn = q_ref[...], qn_ref[...]
    a1, a2 = _attend([(first(q), first(qn), k_ref, v_ref), (second(q), second(qn), k_ref, v_ref)],
                     s_scr, is_ctx, lt, tk, unroll=KV_UNROLL_FULL)
    o = (a1[:, 0:DIFF_V] / a1[:, DIFF_V:DIFF_V + 1]
         - lam_ref[...] * (a2[:, 0:DIFF_V] / a2[:, DIFF_V:DIFF_V + 1]))
    o_ref[...] = _rms(o) * go_ref[...] * out_scale


def _diff_attention(q, k, v, lam, g_o, out_scale):
    lt = q.shape[0]
    tk = _kv_chunk(lt)
    last = lt // ROW_TILE - 1
    return pl.pallas_call(
        functools.partial(_diff_attn_kernel, lt=lt, tk=tk, out_scale=out_scale),
        grid=(DIFF_HEADS, lt // ROW_TILE),
        in_specs=[
            pl.BlockSpec((ROW_TILE, LANES), lambda h, i: (i, h)),
            pl.BlockSpec((ROW_TILE, LANES), lambda h, i: (jnp.minimum(i + 1, last), h)),
            pl.BlockSpec((lt, LANES), lambda h, i: (0, h), pipeline_mode=pl.Buffered(1)),
            pl.BlockSpec((lt, 2 * LANES), lambda h, i: (0, h), pipeline_mode=pl.Buffered(1)),
            _const_spec((1, LANES)),
            _const_spec((1, LANES)),
        ],
        out_specs=pl.BlockSpec((ROW_TILE, LANES), lambda h, i: (i, h)),
        out_shape=jax.ShapeDtypeStruct((lt, GROUP_WIDTH), F32),
        scratch_shapes=[pltpu.VMEM((6, ROW_TILE, tk), F32)],
        compiler_params=_cparams(("arbitrary", "arbitrary")),
    )(q, q, k, v, lam, g_o)


def _s5_matrices(lam_re, lam_im, log_dt, b_re, b_im, c_re, c_im, d_skip):
    t = S5_T
    g, n, ch = S5_GROUPS, S5_STATE, S5_CH
    dt = jnp.exp(log_dt)[:, :, None, None]
    tau = jnp.arange(t + 1, dtype=F32)
    mag = jnp.exp(lam_re[..., None] * dt * tau)
    ang = lam_im[..., None] * dt * tau
    p_re, p_im = mag * jnp.cos(ang), mag * jnp.sin(ang)
    a_re, a_im = p_re[..., 1], p_im[..., 1]
    den = lam_re * lam_re + lam_im * lam_im
    k_re = ((a_re - 1.0) * lam_re + a_im * lam_im) / den
    k_im = (a_im * lam_re - (a_re - 1.0) * lam_im) / den
    bb_re = k_re[..., None] * b_re - k_im[..., None] * b_im
    bb_im = k_re[..., None] * b_im + k_im[..., None] * b_re
    ct_re = c_re.transpose(0, 1, 3, 2)[:, :, :, None, :]
    ct_im = c_im.transpose(0, 1, 3, 2)[:, :, :, None, :]
    ca_re = ct_re * p_re[..., None] - ct_im * p_im[..., None]
    ca_im = ct_re * p_im[..., None] + ct_im * p_re[..., None]
    bt_re = bb_re.transpose(0, 1, 3, 2)[..., None, None]
    bt_im = bb_im.transpose(0, 1, 3, 2)[..., None, None]
    kk = jnp.sum(bt_re * ca_re[:, :, None] - bt_im * ca_im[:, :, None], axis=3)
    strip = jnp.concatenate([kk[1][:, :, t - 1:0:-1], kk[0][:, :, 0:1] + kk[1][:, :, 0:1], kk[0][:, :, 1:t]],
                            axis=2).reshape(g, ch, (2 * t - 1) * ch)
    nb, gp = g // S5_GPB, S5_GPB
    wide = gp * t * ch
    strip = strip.reshape(g, ch, 2 * t - 1, ch)
    tiled = jnp.tile(jnp.pad(strip, ((0, 0), (0, 0), (0, 1), (0, 0))), (1, 1, t, 1))[:, :, :t * (2 * t - 1)]
    mc = tiled.reshape(g, ch, t, 2 * t - 1, ch)[:, :, :, t - 1:]
    mc = mc.reshape(nb, gp, ch, t, t * ch).transpose(0, 3, 1, 2, 4).reshape(nb, wide, t * ch)

    def w_of(d, reverse):
        pr = p_re[d][:, :, 0:t].transpose(0, 2, 1)[:, :, None, :]
        pi = p_im[d][:, :, 0:t].transpose(0, 2, 1)[:, :, None, :]
        br = bb_re[d].transpose(0, 2, 1)[:, None]
        bi = bb_im[d].transpose(0, 2, 1)[:, None]
        out = [pr * br - pi * bi, pr * bi + pi * br]
        return [jnp.flip(o, axis=1) for o in out] if reverse else out

    wc = jnp.concatenate(w_of(0, True) + w_of(1, False), axis=-1)
    wc = wc.reshape(nb, gp, t, ch, 4 * n).transpose(0, 2, 1, 3, 4).reshape(nb, wide, 4 * n)
    v4 = jnp.stack([ca_re[0][:, :, 1:t + 1], -ca_im[0][:, :, 1:t + 1],
                    ca_re[1][:, :, t:0:-1], -ca_im[1][:, :, t:0:-1]], axis=0)
    vc = v4.reshape(4, nb, gp, n, t * ch).transpose(1, 0, 2, 3, 4).reshape(nb, 4 * gp * n, t * ch)
    m_big = _s5_expand(mc.astype(BF16), row_bits=4, col_bits=4)
    w_big = _s5_expand(wc.astype(BF16), row_bits=4, col_bits=6)
    v_big = _s5_expand(vc.astype(BF16), row_bits=6, col_bits=4)
    a16 = jnp.stack([p_re[0][..., t], p_im[0][..., t], p_re[1][..., t], p_im[1][..., t]], axis=0)
    a16 = a16.reshape(4, nb, gp * n).transpose(1, 0, 2)
    a16 = jnp.pad(a16, ((0, 0), (0, 4), (0, 0)))
    return m_big, v_big, w_big, a16, d_skip.reshape(nb, 1, gp * ch)


def _s5_expand_kernel(c_ref, o_ref, *, row_bits, col_bits):
    nrow, ncol = o_ref.shape[1], o_ref.shape[2]
    kc = c_ref.shape[2]
    gbits = S5_GPB.bit_length() - 1
    r = lax.broadcasted_iota(jnp.int32, (kc, ncol), 0)
    c = lax.broadcasted_iota(jnp.int32, (kc, ncol), 1)
    bmask = (1 << col_bits) - 1
    same = jnp.logical_and((r >> col_bits) == (c >> (col_bits + gbits)), (r & bmask) == (c & bmask))
    tiling = jnp.where(same, 1.0, 0.0).astype(BF16)
    slab = S5_GPB << row_bits
    rr = lax.broadcasted_iota(jnp.int32, (slab, ncol), 0)
    cc = lax.broadcasted_iota(jnp.int32, (slab, ncol), 1)
    keep = ((rr >> row_bits) & (S5_GPB - 1)) == ((cc >> col_bits) & (S5_GPB - 1))
    for k in range(nrow // slab):
        rows = slice(k * slab, (k + 1) * slab)
        o_ref[0, rows, :] = jnp.where(keep, _dot(c_ref[0, rows, :], tiling), 0.0).astype(BF16)


def _s5_expand(compact, row_bits, col_bits):
    nb, nrow, kc = compact.shape
    ncol = kc * S5_GPB
    return pl.pallas_call(
        functools.partial(_s5_expand_kernel, row_bits=row_bits, col_bits=col_bits),
        grid=(nb,),
        in_specs=[pl.BlockSpec((1, nrow, kc), lambda b: (b, 0, 0))],
        out_specs=pl.BlockSpec((1, nrow, ncol), lambda b: (b, 0, 0)),
        out_shape=jax.ShapeDtypeStruct((nb, nrow, ncol), BF16),
        compiler_params=_cparams(("arbitrary",)),
    )(compact)


def _s5_gather_kernel(u_ref, wz_ref, ub_ref, z_ref):
    nrow = ub_ref.shape[1]
    for j in range(S5_T):
        ub_ref[0, :, j * LANES:(j + 1) * LANES] = u_ref[pl.ds(j, nrow, stride=S5_T), :].astype(BF16)
    z_ref[0] = _dot(ub_ref[0], wz_ref[0])


def _s5_scan_kernel(z_ref, a_ref, s_ref, *, nblk, nctx):
    w = S5_GPB * S5_STATE
    cols = [slice(k * w, (k + 1) * w) for k in range(4)]
    a = a_ref[0]
    afr, afi, arr, ari = a[0:1], a[1:2], a[2:3], a[3:4]

    def advance(rows_f, rows_r, carry):
        fr, fi, rr, ri = carry
        for col, val, rows in zip(cols, carry, (rows_f, rows_f, rows_r, rows_r)):
            s_ref[0, rows, col] = val
        zfr, zfi = z_ref[0, rows_f, cols[0]], z_ref[0, rows_f, cols[1]]
        zrr, zri = z_ref[0, rows_r, cols[2]], z_ref[0, rows_r, cols[3]]
        return (afr * fr - afi * fi + zfr, afr * fi + afi * fr + zfi,
                arr * rr - ari * ri + zrr, arr * ri + ari * rr + zri)

    zero = jnp.zeros((1, w), F32)
    carry = lax.fori_loop(0, nctx, lambda k, c: advance(pl.ds(k, 1), pl.ds(nctx - 1 - k, 1), c), (zero,) * 4)
    lax.fori_loop(0, nblk - nctx, lambda k, c: advance(pl.ds(nctx + k, 1), pl.ds(nblk - 1 - k, 1), c), carry)


def _s5_out_kernel(ub_ref, s_ref, u_ref, m_ref, v_ref, d_ref, y_ref):
    nrow = ub_ref.shape[1]
    yb = _dot(ub_ref[0], m_ref[0]) + _dot(s_ref[0].astype(BF16), v_ref[0])
    for i in range(S5_T):
        rows = pl.ds(i, nrow, stride=S5_T)
        y_ref[rows, :] = yb[:, i * LANES:(i + 1) * LANES] + u_ref[rows, :] * d_ref[0]


def _s5_mixer(u, mats):
    lt = u.shape[0]
    nblk = lt // S5_T
    nb = S5_GROUPS // S5_GPB
    wide = S5_GPB * S5_T * S5_CH
    swide = 4 * S5_GPB * S5_STATE
    m_big, v_big, wz, a16, dvec = mats
    tb = max(c for c in range(16, 257, 16) if nblk % c == 0)
    ntile = nblk // tb
    tok = pl.BlockSpec((tb * S5_T, LANES), lambda b, r: (r, b))
    blk = lambda w: pl.BlockSpec((1, tb, w), lambda b, r: (b, r, 0))
    per_b = lambda shape: pl.BlockSpec((1,) + shape, lambda b, r: (b, 0, 0))
    ub, z = pl.pallas_call(
        _s5_gather_kernel,
        grid=(nb, ntile),
        in_specs=[tok, per_b(wz.shape[1:])],
        out_specs=[blk(wide), blk(swide)],
        out_shape=[jax.ShapeDtypeStruct((nb, nblk, wide), BF16), jax.ShapeDtypeStruct((nb, nblk, swide), F32)],
        compiler_params=_cparams(("arbitrary", "arbitrary")),
    )(u, wz)
    s = pl.pallas_call(
        functools.partial(_s5_scan_kernel, nblk=nblk, nctx=CTX_LEN // S5_T),
        grid=(nb,),
        in_specs=[pl.BlockSpec((1, nblk, swide), lambda b: (b, 0, 0)),
                  pl.BlockSpec((1,) + a16.shape[1:], lambda b: (b, 0, 0))],
        out_specs=pl.BlockSpec((1, nblk, swide), lambda b: (b, 0, 0)),
        out_shape=jax.ShapeDtypeStruct((nb, nblk, swide), F32),
        compiler_params=_cparams(("arbitrary",)),
    )(z, a16)
    return pl.pallas_call(
        _s5_out_kernel,
        grid=(nb, ntile),
        in_specs=[blk(wide), blk(swide), tok,
                  pl.BlockSpec((1,) + m_big.shape[1:], lambda b, r: (b, 0, 0), pipeline_mode=pl.Buffered(1)),
                  pl.BlockSpec((1,) + v_big.shape[1:], lambda b, r: (b, 0, 0), pipeline_mode=pl.Buffered(1)),
                  per_b((1, LANES))],
        out_specs=tok,
        out_shape=jax.ShapeDtypeStruct((lt, GROUP_WIDTH), F32),
        compiler_params=_cparams(("arbitrary", "arbitrary")),
    )(ub, s, u, m_big, v_big, dvec)


def _even_out_kernel(x_ref, c_ref, att_ref, y_ref, modx_ref, modc_ref, wglu_ref, bglu_ref, wout_ref, o_ref):
    is_ctx = pl.program_id(0) == 0
    gate_a = jnp.where(is_ctx, modc_ref[2:3, :], modx_ref[2:3, :])
    g = _gelu_tanh(y_ref[...])
    ssm = g * _sigmoid(_dot(g.astype(BF16), wglu_ref[...]) + bglu_ref[...])
    mix = (_dot(att_ref[...].astype(BF16), wout_ref[0:GROUP_WIDTH, :])
           + _dot(ssm.astype(BF16), wout_ref[GROUP_WIDTH:, :]))
    o_ref[...] = jnp.where(is_ctx, c_ref[...], x_ref[...]) + gate_a * mix


def _even_out(stream, att, y, modx, modc, w_glu, b_glu, w_out):
    lt = att.shape[0]
    row = lambda w: pl.BlockSpec((ROW_TILE, w), lambda i: (i, 0))
    rows, row_specs = _stream_specs(stream)
    return pl.pallas_call(
        _even_out_kernel,
        grid=(lt // ROW_TILE,),
        in_specs=row_specs + [row(GROUP_WIDTH), row(GROUP_WIDTH), _const_spec(modx.shape),
                              _const_spec(modc.shape), _const_spec(w_glu.shape), _const_spec((1, GROUP_WIDTH)),
                              _const_spec(w_out.shape)],
        out_specs=row(D_MODEL),
        out_shape=jax.ShapeDtypeStruct((lt, D_MODEL), F32),
        compiler_params=_cparams(("arbitrary",)),
    )(*rows, att, y, modx, modc, w_glu.astype(BF16), b_glu.reshape(1, -1), w_out.astype(BF16))


def _odd_mix(att_ref, yf_ref, yr_ref, z_ref, gn_ref, wout_ref):
    gy = (yf_ref[...] + yr_ref[...]) * _silu(z_ref[...])
    gw = SSD_INNER // SSD_GROUPS
    parts = [_rms(gy[:, j * gw:(j + 1) * gw]) for j in range(SSD_GROUPS)]
    ssm = jnp.concatenate(parts, axis=-1) * gn_ref[...]
    return (_dot(att_ref[...].astype(BF16), wout_ref[0:GROUP_WIDTH, :])
            + _dot(ssm.astype(BF16), wout_ref[GROUP_WIDTH:, :]))


def _odd_out_kernel(x_ref, att_ref, yf_ref, yr_ref, z_ref, modx_ref, modc_ref, gn_ref, wout_ref, o_ref):
    is_ctx = pl.program_id(0) == 0
    gate_a = jnp.where(is_ctx, modc_ref[2:3, :], modx_ref[2:3, :])
    o_ref[...] = x_ref[...] + gate_a * _odd_mix(att_ref, yf_ref, yr_ref, z_ref, gn_ref, wout_ref)


def _odd_out(xs, att, yf, yr, z, modx, modc, g_norm, w_out):
    lt = xs.shape[0]
    row = lambda w: pl.BlockSpec((ROW_TILE, w), lambda i: (i, 0))
    return pl.pallas_call(
        _odd_out_kernel,
        grid=(lt // ROW_TILE,),
        in_specs=[row(D_MODEL), row(GROUP_WIDTH), row(SSD_INNER), row(SSD_INNER), row(SSD_INNER),
                  _const_spec(modx.shape), _const_spec(modc.shape), _const_spec((1, SSD_INNER)),
                  _const_spec(w_out.shape)],
        out_specs=row(D_MODEL),
        out_shape=jax.ShapeDtypeStruct((lt, D_MODEL), F32),
        compiler_params=_cparams(("arbitrary",)),
    )(xs, att, yf, yr, z, modx, modc, g_norm.reshape(1, -1), w_out.astype(BF16))


def _ffn_kernel(x_ref, *refs, row_offset, odd_mixer):
    modx_ref, modc_ref, w1_ref, w3_ref, w2_ref, o_ref = refs[-6:]
    tm = x_ref.shape[0]
    row0 = row_offset + pl.program_id(0) * tm
    x = x_ref[...]
    if odd_mixer:
        x = x + _mod_rows(modx_ref, modc_ref, 2, row0, tm) * _odd_mix(*refs[:6])
    sh = _mod_rows(modx_ref, modc_ref, 3, row0, tm)
    sc = _mod_rows(modx_ref, modc_ref, 4, row0, tm)
    gate = _mod_rows(modx_ref, modc_ref, 5, row0, tm)
    h = (_rms(x) * (1.0 + sc) + sh).astype(BF16)
    acc = jnp.zeros((tm, D_MODEL), F32)
    for c in range(D_FF // FFN_CHUNK):
        blk = slice(c * FFN_CHUNK, (c + 1) * FFN_CHUNK)
        a = _dot(h, w1_ref[:, blk])
        b = _dot(h, w3_ref[:, blk])
        acc = acc + _dot((_silu(a) * b).astype(BF16), w2_ref[blk, :])
    o_ref[...] = x + gate * acc


def _ffn(xs, modx, modc, w1, w3, w2, latents_only=False, odd_mixer=None):
    lt = xs.shape[0]
    if latents_only:
        nrows = lt - CTX_LEN
        tm = FFN_TILE_LATENT if nrows % FFN_TILE_LATENT == 0 else ROW_TILE
        in_row = lambda w: pl.BlockSpec((pl.Element(tm), pl.Element(w)),
                                        lambda i: (pl.multiple_of(CTX_LEN + i * tm, ROW_TILE), 0))
    else:
        nrows = lt
        tm = FFN_TILE if lt % FFN_TILE == 0 else ROW_TILE
        in_row = lambda w: pl.BlockSpec((tm, w), lambda i: (i, 0))
    single = lambda shape: pl.BlockSpec(shape, lambda i: (0, 0), pipeline_mode=pl.Buffered(1))
    mix_args, mix_specs = (), []
    if odd_mixer is not None:
        att, yf, yr, z, g_norm, w_out = odd_mixer
        mix_args = (att, yf, yr, z, g_norm.reshape(1, -1), w_out.astype(BF16))
        mix_specs = [in_row(a.shape[1]) for a in mix_args[:4]] + [_const_spec((1, SSD_INNER)), single(w_out.shape)]
    return pl.pallas_call(
        functools.partial(_ffn_kernel, row_offset=lt - nrows, odd_mixer=odd_mixer is not None),
        grid=(nrows // tm,),
        in_specs=[in_row(D_MODEL)] + mix_specs + [_const_spec(modx.shape), _const_spec(modc.shape), single(w1.shape),
                                                  single(w3.shape), single(w2.shape)],
        out_specs=pl.BlockSpec((tm, D_MODEL), lambda i: (i, 0)),
        out_shape=jax.ShapeDtypeStruct((nrows, D_MODEL), F32),
        compiler_params=_cparams(("arbitrary",)),
    )(xs, *mix_args, modx, modc, w1.astype(BF16), w3.astype(BF16), w2.astype(BF16))


ODD_COLS = 2048 + SSD_XBC + LANES


def _odd_in_kernel(x_ref, modx_ref, modc_ref, win_ref, gq_ref, gk_ref, ee_ref, rrow_ref, rcol_ref,
                   q_ref, k_ref, v_ref, z_ref, xbc_ref, dt_ref):
    i = pl.program_id(0)
    is_ctx = i == 0
    sh = jnp.where(is_ctx, modc_ref[0:1, :], modx_ref[0:1, :])
    sc = jnp.where(is_ctx, modc_ref[1:2, :], modx_ref[1:2, :])
    h = (_rms(x_ref[...]) * (1.0 + sc) + sh).astype(BF16)
    proj = _dot(h, win_ref[...])
    ones = jnp.ones((ROW_TILE, DIFF_V), BF16)
    for hd in range(DIFF_HEADS):
        v_ref[:, 2 * hd * DIFF_V:(2 * hd + 1) * DIFF_V] = proj[:, 1024 + hd * DIFF_V:1024 + (hd + 1) * DIFF_V].astype(BF16)
        v_ref[:, (2 * hd + 1) * DIFF_V:(2 * hd + 2) * DIFF_V] = ones
    z_ref[...] = proj[:, 1536:2048]
    xbc_ref[...] = proj[:, 2048:2048 + SSD_XBC]
    dt_ref[...] = proj[:, 2048 + SSD_XBC:ODD_COLS]
    nrep = GROUP_WIDTH // LANES
    cos = jnp.concatenate([_rope_tile(rrow_ref, rcol_ref, 0, is_ctx, True)] * nrep, axis=-1)
    sina = jnp.concatenate([_rope_tile(rrow_ref, rcol_ref, 1, is_ctx, False)] * nrep, axis=-1)
    sinb = jnp.concatenate([_rope_tile(rrow_ref, rcol_ref, 2, is_ctx, False)] * nrep, axis=-1)
    half = DIFF_QK // 4
    qscale = DIFF_QK ** -0.5 * LOG2E

    def prep(t, g_ref, scale):
        ss = _split_dot(t * t, ee_ref[...], 2)
        tn = t * lax.rsqrt(ss * (1.0 / DIFF_QK) + EPS) * g_ref[...]
        up = pltpu.roll(tn, GROUP_WIDTH - half, axis=1)
        dn = pltpu.roll(tn, half, axis=1)
        return ((tn * cos + up * sina + dn * sinb) * scale).astype(BF16)

    q_ref[...] = prep(proj[:, 0:512], gq_ref, qscale)
    k_ref[...] = prep(proj[:, 512:1024], gk_ref, 1.0)


def _odd_in(xs, modx, modc, w_in, g_q, g_k, rope_row, rope_col):
    lt = xs.shape[0]
    win = jnp.pad(w_in, ((0, 0), (0, ODD_COLS - w_in.shape[1]))).astype(BF16)
    nblk = GROUP_WIDTH // DIFF_QK
    blk_id = np.arange(GROUP_WIDTH) // DIFF_QK
    ee = jnp.asarray(blk_id[:, None] == blk_id[None, :], BF16)
    gq = jnp.tile(g_q, nblk).reshape(1, -1)
    gk = jnp.tile(g_k, nblk).reshape(1, -1)
    args = (xs, modx, modc, win, gq, gk, ee, rope_row, rope_col)
    row = lambda w: pl.BlockSpec((ROW_TILE, w), lambda i: (i, 0))
    in_specs = ([row(D_MODEL)] + [_const_spec(a.shape) for a in args[1:7]]
                + [pl.BlockSpec((1,) + rope_row.shape[1:], lambda i: (i, 0, 0)), _const_spec(rope_col.shape)])
    return pl.pallas_call(
        _odd_in_kernel,
        grid=(lt // ROW_TILE,),
        in_specs=in_specs,
        out_specs=[row(512), row(512), row(1024), row(512), row(SSD_XBC), row(LANES)],
        out_shape=[jax.ShapeDtypeStruct((lt, 512), BF16), jax.ShapeDtypeStruct((lt, 512), BF16),
                   jax.ShapeDtypeStruct((lt, 1024), BF16), jax.ShapeDtypeStruct((lt, 512), F32),
                   jax.ShapeDtypeStruct((lt, SSD_XBC), F32), jax.ShapeDtypeStruct((lt, LANES), F32)],
        compiler_params=_cparams(("arbitrary",)),
    )(*args)


def _conv_kernel(prev_ref, cur_ref, next_ref, w_ref, b_ref, o_ref, ext_scr, *, ntiles):
    i = pl.program_id(0)
    pad = SSD_CONV // 2
    has_prev = i >= 2
    has_next = jnp.logical_and(i >= 1, i < ntiles - 1)
    ext_scr[0:8, :] = jnp.where(has_prev, prev_ref[...], 0.0)
    ext_scr[8:8 + ROW_TILE, :] = cur_ref[...]
    ext_scr[8 + ROW_TILE:16 + ROW_TILE, :] = jnp.where(has_next, next_ref[...], 0.0)
    acc = jnp.zeros((ROW_TILE, SSD_XBC), F32) + b_ref[...]
    for k in range(SSD_CONV):
        acc = acc + ext_scr[pl.ds(8 - pad + k, ROW_TILE), :] * w_ref[k:k + 1, :]
    o_ref[...] = _silu(acc)


def _ssd_conv(xbc, conv_w, conv_b):
    lt = xbc.shape[0]
    ntiles = lt // ROW_TILE
    per = ROW_TILE // 8
    last8 = lt // 8 - 1
    w = jnp.pad(conv_w, ((0, 8 - SSD_CONV), (0, 0)))
    return pl.pallas_call(
        functools.partial(_conv_kernel, ntiles=ntiles),
        grid=(ntiles,),
        in_specs=[
            pl.BlockSpec((8, SSD_XBC), lambda i: (jnp.maximum(i * per - 1, 0), 0)),
            pl.BlockSpec((ROW_TILE, SSD_XBC), lambda i: (i, 0)),
            pl.BlockSpec((8, SSD_XBC), lambda i: (jnp.minimum((i + 1) * per, last8), 0)),
            _const_spec((8, SSD_XBC)),
            _const_spec((1, SSD_XBC)),
        ],
        out_specs=pl.BlockSpec((ROW_TILE, SSD_XBC), lambda i: (i, 0)),
        out_shape=jax.ShapeDtypeStruct((lt, SSD_XBC), F32),
        scratch_shapes=[pltpu.VMEM((ROW_TILE + 16, SSD_XBC), F32)],
        compiler_params=_cparams(("arbitrary",)),
    )(xbc, xbc, xbc, w, conv_b.reshape(1, -1))


def _split_parts(x, parts):
    out = []
    for _ in range(parts):
        piece = x.astype(BF16)
        out.append(piece)
        x = x - piece.astype(F32)
    return out


def _ssd_direction(xc, dtr, dtr_t, bias, alog, bias_t, alog_t, sel, dskip, state_ref, rev):
    q = SSD_CHUNK
    hp = SSD_HEAD_DIM
    dt = _softplus(dtr + bias)
    da = dt * -jnp.exp(alog)
    dt_t = _softplus(dtr_t + bias_t)
    da_t = dt_t * -jnp.exp(alog_t)
    r_idx = lax.broadcasted_iota(jnp.int32, (q, q), 0)
    c_idx = lax.broadcasted_iota(jnp.int32, (q, q), 1)
    keep = c_idx >= r_idx if rev else c_idx <= r_idx
    tri = jnp.where(keep, 1.0, 0.0).astype(BF16)
    nt = (((1,), (1,)), ((), ()))
    cs = sum(_dot(tri, p) for p in _split_parts(da, 3))
    cs_t = sum(lax.dot_general(p, tri, nt, preferred_element_type=F32) for p in _split_parts(da_t, 3))
    csb = sum(_dot(p, sel) for p in _split_parts(cs, 3))
    dtb = sum(_dot(p, sel) for p in _split_parts(dt, 2))
    tot = csb[0:1, :] if rev else csb[q - 1:q, :]
    x = xc[:, 0:SSD_INNER]
    xdt = x * dtb
    xdt_b = xdt.astype(BF16)
    xdec = (xdt * jnp.exp(tot - csb)).astype(BF16)
    csb_up = pltpu.roll(csb, SSD_INNER - hp, axis=1)
    csb_dn = pltpu.roll(csb, hp, axis=1)
    low = lax.broadcasted_iota(jnp.int32, (1, 2 * hp), 1) < hp
    gw = SSD_GROUPS * SSD_STATE
    bm = xc[:, SSD_INNER:SSD_INNER + gw].astype(BF16)
    cm = xc[:, SSD_INNER + gw:SSD_INNER + 2 * gw].astype(BF16)
    hpg = SSD_HEADS // SSD_GROUPS
    head_row = SSD_HEADS if rev else 0
    outs = []
    for g in range(SSD_GROUPS):
        bg = bm[:, g * SSD_STATE:(g + 1) * SSD_STATE]
        cg = cm[:, g * SSD_STATE:(g + 1) * SSD_STATE]
        scores = lax.dot_general(cg, bg, nt, preferred_element_type=F32)
        cols = slice(g * hpg * hp, (g + 1) * hpg * hp)
        s_in = state_ref[:, cols]
        y_off = _dot(cg, s_in.astype(BF16))
        pairs = []
        for pp in range(hpg // 2):
            blk = slice((g * hpg + 2 * pp) * hp, (g * hpg + 2 * pp + 2) * hp)
            here, up, dn = csb[:, blk], csb_up[:, blk], csb_dn[:, blk]
            ys = []
            for e in range(2):
                hd = g * hpg + 2 * pp + e
                col = jnp.where(low, here, dn) if e == 0 else jnp.where(low, up, here)
                row = cs_t[head_row + hd:head_row + hd + 1, :]
                lmat = jnp.exp(jnp.where(keep, col - row, -jnp.inf))
                ys.append(_dot((scores * lmat).astype(BF16), xdt_b[:, blk]))
            pairs.append(jnp.where(low, ys[0], ys[1]))
        y = jnp.concatenate(pairs, axis=-1) + y_off * jnp.exp(csb[:, cols])
        if dskip is not None:
            y = y + x[:, cols] * dskip[:, cols]
        outs.append(y)
        new = lax.dot_general(bg, xdec[:, cols], (((0,), (0,)), ((), ())), preferred_element_type=F32)
        state_ref[:, cols] = s_in * jnp.exp(tot[:, cols]) + new
    return jnp.concatenate(outs, axis=-1)


def _ssd_kernel(xf_ref, dtf_ref, dtft_ref, xr_ref, dtr_ref, dtrt_ref, bias_ref, alog_ref, biast_ref,
                alogt_ref, sel_ref, dskip_ref, yf_ref, yr_ref, sf_scr, sr_scr):
    @pl.when(pl.program_id(0) == 0)
    def _():
        sf_scr[...] = jnp.zeros_like(sf_scr)
        sr_scr[...] = jnp.zeros_like(sr_scr)

    consts = (bias_ref[...], alog_ref[...], biast_ref[...], alogt_ref[...])
    per_block = ROW_TILE // SSD_CHUNK
    for c in range(per_block):
        rf = slice(c * SSD_CHUNK, (c + 1) * SSD_CHUNK)
        cr = per_block - 1 - c
        rr = slice(cr * SSD_CHUNK, (cr + 1) * SSD_CHUNK)
        yf_ref[rf, :] = _ssd_direction(xf_ref[rf, :], dtf_ref[rf, :], dtft_ref[0, c], *consts, sel_ref[0],
                                       dskip_ref[...], sf_scr, False)
        yr_ref[rr, :] = _ssd_direction(xr_ref[rr, :], dtr_ref[rr, :], dtrt_ref[0, cr], *consts, sel_ref[1], None,
                                       sr_scr, True)


def _ssd_scan(xc, dtr, dt_bias, a_log, d_skip):
    lt = xc.shape[0]
    nblk = lt // ROW_TILE
    per_block = ROW_TILE // SSD_CHUNK
    ndt = 2 * SSD_HEADS
    dtr_t = dtr[:, :ndt].reshape(nblk, per_block, SSD_CHUNK, ndt).transpose(0, 1, 3, 2)

    def rev_block(k):
        return jnp.where(k == 0, 0, nblk - k)

    fwd = lambda w: pl.BlockSpec((ROW_TILE, w), lambda k: (k, 0))
    bwd = lambda w: pl.BlockSpec((ROW_TILE, w), lambda k: (rev_block(k), 0))
    dtt = lambda index: pl.BlockSpec((1, per_block, ndt, SSD_CHUNK), lambda k: (index(k), 0, 0, 0))
    lanes = lambda v: jnp.pad(v.reshape(1, ndt), ((0, 0), (0, LANES - ndt)))
    rows = lambda v: jnp.broadcast_to(v.reshape(ndt, 1), (ndt, SSD_CHUNK))
    sel = np.zeros((2, LANES, SSD_INNER), np.float32)
    for d in range(2):
        for hd in range(SSD_HEADS):
            sel[d, d * SSD_HEADS + hd, hd * SSD_HEAD_DIM:(hd + 1) * SSD_HEAD_DIM] = 1.0
    small = (lanes(dt_bias), lanes(a_log), rows(dt_bias), rows(a_log), jnp.asarray(sel, BF16),
             jnp.repeat(d_skip, SSD_HEAD_DIM).reshape(1, -1))
    return pl.pallas_call(
        _ssd_kernel,
        grid=(nblk,),
        in_specs=[fwd(SSD_XBC), fwd(LANES), dtt(lambda k: k), bwd(SSD_XBC), bwd(LANES), dtt(rev_block)]
                 + [_const_spec(a.shape) for a in small],
        out_specs=[fwd(SSD_INNER), bwd(SSD_INNER)],
        out_shape=[jax.ShapeDtypeStruct((lt, SSD_INNER), F32)] * 2,
        scratch_shapes=[pltpu.VMEM((SSD_STATE, SSD_INNER), F32)] * 2,
        compiler_params=_cparams(("arbitrary",)),
    )(xc, dtr, dtr_t, xc, dtr, dtr_t, *small)


ROWS_PER_TILE = ROW_TILE // GRID_W


def _rope_tables(seq):
    ng = seq // GRID_W
    nctx = CTX_LEN // GRID_W

    def trig(n, count):
        inv = ROPE_THETA ** (-jnp.arange(n, dtype=F32) / n)
        ang = jnp.arange(count, dtype=F32)[:, None] * inv[None, :]
        return jnp.cos(ang), jnp.sin(ang)

    def lay(count, parts):
        return jnp.concatenate([p if hasattr(p, "shape") else jnp.full((count, p[0]), p[1], F32) for p in parts],
                               axis=1)

    def row_slabs(tabs, idents):
        full = [jnp.concatenate([jnp.full((nctx, LANES), ident, F32), t], axis=0) for t, ident in zip(tabs, idents)]
        packed = jnp.concatenate(full, axis=1).reshape(-1, ROWS_PER_TILE, len(tabs) * LANES)
        return jnp.pad(packed, ((0, 0), (0, 8 - ROWS_PER_TILE), (0, 0)))

    n0 = MLA_ROPE // 4
    cr, sr = trig(n0, ng)
    cc, sc = trig(n0, GRID_W)
    tail = LANES - MLA_QK
    row0 = row_slabs([lay(ng, [(MLA_NOPE, 1.0), cr, cr, (2 * n0, 1.0), (tail, 1.0)]),
                      lay(ng, [(MLA_NOPE, 0.0), -sr, sr, (2 * n0, 0.0), (tail, 0.0)])], (1.0, 0.0))
    col0 = jnp.concatenate([lay(GRID_W, [(MLA_NOPE, 1.0), (2 * n0, 1.0), cc, cc, (tail, 1.0)]),
                            lay(GRID_W, [(MLA_NOPE, 0.0), (2 * n0, 0.0), -sc, sc, (tail, 0.0)])], axis=1)
    n1 = DIFF_QK // 4
    cr, sr = trig(n1, ng)
    cc, sc = trig(n1, GRID_W)
    row1 = row_slabs([lay(ng, [cr, cr, (2 * n1, 1.0)] * 2),
                      lay(ng, [-sr, (n1, 0.0), (2 * n1, 0.0)] * 2),
                      lay(ng, [(n1, 0.0), sr, (2 * n1, 0.0)] * 2)], (1.0, 0.0, 0.0))
    col1 = jnp.concatenate([lay(GRID_W, [(2 * n1, 1.0), cc, cc] * 2),
                            lay(GRID_W, [(2 * n1, 0.0), -sc, (n1, 0.0)] * 2),
                            lay(GRID_W, [(2 * n1, 0.0), (n1, 0.0), sc] * 2)], axis=1)
    return row0, col0, row1, col1


def _rope_tile(row_ref, col_ref, k, is_ctx, product):
    r = row_ref[0, :, k * LANES:(k + 1) * LANES]
    c = jnp.where(is_ctx, 1.0 if product else 0.0, col_ref[:, k * LANES:(k + 1) * LANES])
    rb = jnp.concatenate([jnp.broadcast_to(r[j:j + 1], (GRID_W, LANES)) for j in range(ROWS_PER_TILE)], axis=0)
    cb = jnp.concatenate([c] * ROWS_PER_TILE, axis=0)
    return rb * cb if product else rb + cb


def kernel(x, c, ctx, c_ctx, ada_w, ada_b, ffn_w1, ffn_w3, ffn_w2, e_w_in, e_w_out, mla_g_qa, mla_w_qb, mla_g_kva, mla_w_kvb, mla_g_qn, mla_g_kn, s5_lam_re, s5_lam_im, s5_log_dt, s5_b_re, s5_b_im, s5_c_re, s5_c_im, s5_d, s5_w_glu, s5_b_glu, o_w_in, o_w_out, diff_g_q, diff_g_k, diff_lq1, diff_lk1, diff_lq2, diff_lk2, diff_g_o, ssd_conv_w, ssd_conv_b, ssd_dt_bias, ssd_a_log, ssd_d, ssd_g):
    depth = ada_w.shape[0]
    seq = x.shape[1]
    assert x.shape[0] == 1 and ctx.shape[1] == CTX_LEN and seq % ROW_TILE == 0
    lt = seq + CTX_LEN
    xs = None
    mods = _mod_vectors(c, c_ctx, ada_w, ada_b)
    rope_row0, rope_col0, rope_row1, rope_col1 = _rope_tables(seq)
    for i in range(depth):
        j = i // 2
        modx, modc = mods[i, 0], mods[i, 1]
        if xs is None and i % 2 == 1:
            xs = jnp.concatenate([ctx[0], x[0]], axis=0)
        if i % 2 == 0:
            stream = (x[0], ctx[0], 0) if xs is None else (xs, xs, 1)
            q, k, v, u = _even_in(stream, lt, modx, modc, e_w_in[j], mla_g_qa[j], mla_w_qb[j], mla_g_kva[j],
                                  mla_w_kvb[j], mla_g_qn[j], mla_g_kn[j], rope_row0, rope_col0)
            att = _mla_attention(q, k, v)
            mats = _s5_matrices(s5_lam_re[j], s5_lam_im[j], s5_log_dt[j], s5_b_re[j], s5_b_im[j],
                                s5_c_re[j], s5_c_im[j], s5_d[j])
            y = _s5_mixer(u, mats)
            xs = _even_out(stream, att, y, modx, modc, s5_w_glu[j], s5_b_glu[j], e_w_out[j])
        else:
            q, k, v, z, xbc, dtr = _odd_in(xs, modx, modc, o_w_in[j], diff_g_q[j], diff_g_k[j],
                                           rope_row1, rope_col1)
            lam_init = 0.8 - 0.6 * math.exp(-0.3 * i)
            lam = (jnp.exp(jnp.sum(diff_lq1[j] * diff_lk1[j])) - jnp.exp(jnp.sum(diff_lq2[j] * diff_lk2[j]))
                   + lam_init)
            att = _diff_attention(q, k, v, jnp.full((1, LANES), lam, F32), diff_g_o[j].reshape(1, -1),
                                  1.0 - lam_init)
            xc = _ssd_conv(xbc, ssd_conv_w[j], ssd_conv_b[j])
            yf, yr = _ssd_scan(xc, dtr, ssd_dt_bias[j], ssd_a_log[j], ssd_d[j])
            odd_mixer = (att, yf, yr, z, ssd_g[j], o_w_out[j])
        last = i == depth - 1
        if i % 2 == 1 and not last:
            xs = _odd_out(xs, att, yf, yr, z, modx, modc, ssd_g[j], o_w_out[j])
        xs = _ffn(xs, modx, modc, ffn_w1[i], ffn_w3[i], ffn_w2[i], latents_only=last,
                  odd_mixer=odd_mixer if (i % 2 == 1 and last) else None)
    return xs[None]
```
